```python
import math
import jax, jax.numpy as jnp
from jax import lax
import numpy as np

D_MODEL = 2048
BATCH = 1
SEQ = 8192
DEPTH = 2
DEC_BATCH = 128
DEC_SEQ = 4
PAST_LEN = 8192
PAGE_SIZE = 128

N_MIXERS = 2
N_HEADS = 32
N_KV_HEADS = 8
HEAD_DIM = D_MODEL // N_HEADS
GROUP = N_HEADS // N_KV_HEADS
WINDOW = 128
HG_EXPAND = 128
HG_HEADS = D_MODEL // HG_EXPAND
HG_DK = HG_EXPAND
HG_DV = D_MODEL // HG_HEADS
HG_CHUNK = 64
D_FF = 5632
CONV_W = 3
N_MOD = 6
EPS = 1e-6

kernel_name = 'hybrid_swa_sink_hgrn2_convffn_adaln_step'


def rmsnorm(x, g):
    xf = x.astype(jnp.float32)
    y = xf * lax.rsqrt(jnp.mean(xf * xf, axis=-1, keepdims=True) + EPS) * g.astype(jnp.float32)
    return y.astype(x.dtype)


def adaln(c, w, b):
    mod = jax.nn.silu(c) @ w + b
    return [m[:, None, :] for m in jnp.split(mod, N_MOD, axis=-1)]


def modulate(x, g, shift, scale):
    return rmsnorm(x, g) * (1 + scale) + shift


def alibi_slopes():
    m = jnp.exp2(-8.0 * jnp.arange(1, N_HEADS + 1, dtype=jnp.float32) / N_HEADS)
    return m.reshape(N_KV_HEADS, GROUP)


def swa_project(h, w_qkv):
    B, L, _ = h.shape
    q, k, v = jnp.split(h @ w_qkv, [N_HEADS * HEAD_DIM, (N_HEADS + N_KV_HEADS) * HEAD_DIM], axis=-1)
    return (q.reshape(B, L, N_KV_HEADS, GROUP, HEAD_DIM),
            k.reshape(B, L, N_KV_HEADS, HEAD_DIM),
            v.reshape(B, L, N_KV_HEADS, HEAD_DIM))


def swa_core(q, k, v, q_pos, k_pos, sinks):
    s = jnp.einsum('bnqkgd,bnskd->bnkgqs', q, k).astype(jnp.float32) * (HEAD_DIM ** -0.5)
    dist = q_pos[:, :, None] - k_pos[:, None, :]
    allowed = (dist >= 0) & (dist < WINDOW) & (k_pos[:, None, :] >= 0)
    s = s - alibi_slopes()[:, :, None, None] * dist[:, None, None].astype(jnp.float32)
    s = jnp.where(allowed[:, None, None], s, -jnp.inf)
    sink = jnp.broadcast_to(sinks.astype(jnp.float32).reshape(N_KV_HEADS, GROUP, 1, 1), s.shape[:-1] + (1,))
    p = jax.nn.softmax(jnp.concatenate([s, sink], axis=-1), axis=-1)[..., :-1]
    return jnp.einsum('bnkgqs,bnskd->bnqkgd', p.astype(v.dtype), v)


def swa_prompt(h, w_qkv, w_o, sinks):
    B, L, _ = h.shape
    q, k, v = swa_project(h, w_qkv)
    nb = L // WINDOW
    qb = q.reshape(B, nb, WINDOW, N_KV_HEADS, GROUP, HEAD_DIM)

    def band(a):
        ap = jnp.pad(a, ((0, 0), (WINDOW, 0), (0, 0), (0, 0))).reshape(B, nb + 1, WINDOW, N_KV_HEADS, HEAD_DIM)
        return jnp.concatenate([ap[:, :-1], ap[:, 1:]], axis=2)

    kp = jnp.arange(-WINDOW, L, dtype=jnp.int32).reshape(nb + 1, WINDOW)
    kpos = jnp.concatenate([kp[:-1], kp[1:]], axis=1)
    qpos = jnp.arange(L, dtype=jnp.int32).reshape(nb, WINDOW)
    o = swa_core(qb, band(k), band(v), qpos, kpos, sinks).reshape(B, L, N_HEADS * HEAD_DIM)
    keep = min(WINDOW, L)
    return o @ w_o, k[:, L - keep:], v[:, L - keep:]


def swa_sample(h, cache_k, cache_v, w_qkv, w_o, sinks):
    B, L, _ = h.shape
    q, k, v = swa_project(h, w_qkv)
    cr = cache_k.shape[1]
    kk = jnp.concatenate([cache_k.astype(k.dtype), k], axis=1)[:, None]
    vv = jnp.concatenate([cache_v.astype(v.dtype), v], axis=1)[:, None]
    qpos = (PAST_LEN + jnp.arange(L, dtype=jnp.int32))[None]
    kpos = jnp.concatenate([PAST_LEN - cr + jnp.arange(cr, dtype=jnp.int32), qpos[0]])[None]
    o = swa_core(q[:, None], kk, vv, qpos, kpos, sinks).reshape(B, L, N_HEADS * HEAD_DIM)
    return o @ w_o, k, v


def hgrn_lower_bound(p, layer):
    cs = jnp.cumsum(jax.nn.softmax(p.astype(jnp.float32), axis=0), axis=0)
    return cs[layer] - cs[0]


def hgrn2_chunkwise(q, k, v, logf, s0, chunk):
    B, L, H, DK = q.shape
    DV = v.shape[-1]
    n = L // chunk
    tri = jnp.tril(jnp.ones((chunk, chunk), dtype=bool))

    def to_chunks(a):
        return jnp.moveaxis(a.reshape(B, n, chunk, H, a.shape[-1]), 1, 0)

    def step(s, inp):
        qc, kc, vc, fc = inp
        b = jnp.cumsum(fc, axis=1)
        o_inter = jnp.einsum('bthk,bhkv->bthv', qc * jnp.exp(b), s)
        rel = jnp.where(tri[None, :, :, None, None], b[:, :, None] - b[:, None, :], -jnp.inf)
        a = jnp.einsum('bthk,bshk,btshk->bhts', qc, kc, jnp.exp(rel))
        o_intra = jnp.einsum('bhts,bshv->bthv', a, vc)
        b_end = b[:, -1]
        s_new = jnp.exp(b_end)[..., None] * s + jnp.einsum('bshk,bshv->bhkv', kc * jnp.exp(b_end[:, None] - b), vc)
        return s_new, o_inter + o_intra

    s_fin, o = lax.scan(step, s0, (to_chunks(q), to_chunks(k), to_chunks(v), to_chunks(logf)))
    return jnp.moveaxis(o, 0, 1).reshape(B, L, H, DV), s_fin


def hgrn2_mixer(h, w_in, lb, norm_g, w_o, s0):
    B, L, _ = h.shape
    fk = HG_HEADS * HG_DK
    qr, fr, ir, gr = jnp.split(h @ w_in, [fk, 2 * fk, 2 * fk + HG_HEADS * HG_DV], axis=-1)
    q = jax.nn.silu(qr.astype(jnp.float32)).reshape(B, L, HG_HEADS, HG_DK)
    fr = fr.astype(jnp.float32).reshape(B, L, HG_HEADS, HG_DK)
    lb = lb.reshape(HG_HEADS, HG_DK)
    logf = jnp.logaddexp(jnp.log(lb), jnp.log1p(-lb) + jax.nn.log_sigmoid(fr))
    k = (1.0 - lb) * jax.nn.sigmoid(-fr)
    v = ir.astype(jnp.float32).reshape(B, L, HG_HEADS, HG_DV)
    chunk = math.gcd(L, HG_CHUNK)
    o, s_fin = hgrn2_chunkwise(q, k, v, logf, s0.astype(jnp.float32), chunk)
    o = o * lax.rsqrt(jnp.mean(o * o, axis=-1, keepdims=True) + EPS) * norm_g.astype(jnp.float32).reshape(HG_HEADS, HG_DV)
    o = (o.reshape(B, L, HG_HEADS * HG_DV) * jax.nn.silu(gr.astype(jnp.float32))).astype(h.dtype)
    return o @ w_o, s_fin


def conv_ffn(h, w_in, conv_w, conv_b, w_out, buf):
    u, g = jnp.split(h @ w_in, 2, axis=-1)
    up = jnp.concatenate([buf.astype(u.dtype), u], axis=1)
    a = lax.conv_general_dilated(up, conv_w[:, None, :].astype(u.dtype), window_strides=(1,), padding='VALID',
                                 dimension_numbers=('NWC', 'WIO', 'NWC'), feature_group_count=D_FF) + conv_b
    y = (jax.nn.gelu(a, approximate=False) * g) @ w_out
    return y, up[:, up.shape[1] - (CONV_W - 1):]


def setup_inputs(seed: int = 0) -> dict:
    key = jax.random.key(seed)
    ks = jax.random.split(key, 24)
    D = D_MODEL

    def nrm(k, shape, scale):
        return jax.random.normal(k, shape, jnp.float32) * scale

    qkv = (N_HEADS + 2 * N_KV_HEADS) * HEAD_DIM
    hg_in = 2 * HG_HEADS * HG_DK + 2 * HG_HEADS * HG_DV
    cache_rows = min(WINDOW, PAST_LEN)
    return {
        'x_prompt': nrm(ks[0], (BATCH, SEQ, D), 1.0),
        'x_sample': nrm(ks[1], (DEC_BATCH, DEC_SEQ, D), 1.0),
        'cache_swa_k': nrm(ks[2], (DEC_BATCH, cache_rows, N_KV_HEADS, HEAD_DIM), 1.0),
        'cache_swa_v': nrm(ks[3], (DEC_BATCH, cache_rows, N_KV_HEADS, HEAD_DIM), 1.0),
        'state_hgrn': nrm(ks[4], (DEC_BATCH, HG_HEADS, HG_DK, HG_DV), 0.5),
        'state_ffn_conv': nrm(ks[5], (DEPTH, DEC_BATCH, CONV_W - 1, D_FF), 1.0),
        'c_prompt': nrm(ks[6], (BATCH, D), 1.0),
        'c_sample': nrm(ks[7], (DEC_BATCH, D), 1.0),
        'norm1_g': 1.0 + nrm(ks[8], (DEPTH, D), 0.02),
        'norm2_g': 1.0 + nrm(ks[9], (DEPTH, D), 0.02),
        'w_ada': nrm(ks[10], (DEPTH, D, N_MOD * D), 0.5 * D ** -0.5),
        'b_ada': nrm(ks[11], (DEPTH, N_MOD * D), 0.02),
        'attn_w_qkv': nrm(ks[12], (D, qkv), D ** -0.5),
        'attn_w_o': nrm(ks[13], (N_HEADS * HEAD_DIM, D), (N_HEADS * HEAD_DIM) ** -0.5),
        'attn_sinks': nrm(ks[14], (N_HEADS,), 1.0),
        'hgrn_w_in': nrm(ks[15], (D, hg_in), D ** -0.5),
        'hgrn_lower_bounds': nrm(ks[16], (DEPTH, HG_HEADS * HG_DK), 0.5),
        'hgrn_norm_g': 1.0 + nrm(ks[17], (HG_HEADS * HG_DV,), 0.02),
        'hgrn_w_o': nrm(ks[18], (HG_HEADS * HG_DV, D), (HG_HEADS * HG_DV) ** -0.5),
        'ffn_w_in': nrm(ks[19], (DEPTH, D, 2 * D_FF), D ** -0.5),
        'ffn_conv_w': nrm(ks[20], (DEPTH, CONV_W, D_FF), CONV_W ** -0.5),
        'ffn_conv_b': nrm(ks[21], (DEPTH, D_FF), 0.01),
        'ffn_w_out': nrm(ks[22], (DEPTH, D_FF, D), D_FF ** -0.5),
        'final_norm_g': 1.0 + nrm(ks[23], (D,), 0.02),
    }


def reference(x_prompt, x_sample, cache_swa_k, cache_swa_v, state_hgrn, state_ffn_conv, c_prompt, c_sample,
              norm1_g, norm2_g, w_ada, b_ada, attn_w_qkv, attn_w_o, attn_sinks,
              hgrn_w_in, hgrn_lower_bounds, hgrn_norm_g, hgrn_w_o,
              ffn_w_in, ffn_conv_w, ffn_conv_b, ffn_w_out, final_norm_g):
    xp, xs = x_prompt, x_sample
    bp_n, bs_n = xp.shape[0], xs.shape[0]
    conv_p, conv_s = [], []
    for l in range(DEPTH):
        sh1p, sc1p, g1p, sh2p, sc2p, g2p = adaln(c_prompt, w_ada[l], b_ada[l])
        sh1s, sc1s, g1s, sh2s, sc2s, g2s = adaln(c_sample, w_ada[l], b_ada[l])
        hp = modulate(xp, norm1_g[l], sh1p, sc1p)
        hs = modulate(xs, norm1_g[l], sh1s, sc1s)
        if l % N_MIXERS == 0:
            mp, swa_k_prompt, swa_v_prompt = swa_prompt(hp, attn_w_qkv, attn_w_o, attn_sinks)
            ms, swa_k_sample, swa_v_sample = swa_sample(hs, cache_swa_k, cache_swa_v, attn_w_qkv, attn_w_o, attn_sinks)
        else:
            lb = hgrn_lower_bound(hgrn_lower_bounds, l)
            s0p = jnp.zeros((bp_n, HG_HEADS, HG_DK, HG_DV), jnp.float32)
            mp, hgrn_state_prompt = hgrn2_mixer(hp, hgrn_w_in, lb, hgrn_norm_g, hgrn_w_o, s0p)
            ms, hgrn_state_sample = hgrn2_mixer(hs, hgrn_w_in, lb, hgrn_norm_g, hgrn_w_o, state_hgrn)
        xp = xp + g1p * mp
        xs = xs + g1s * ms
        hp = modulate(xp, norm2_g[l], sh2p, sc2p)
        hs = modulate(xs, norm2_g[l], sh2s, sc2s)
        fp, bufp = conv_ffn(hp, ffn_w_in[l], ffn_conv_w[l], ffn_conv_b[l], ffn_w_out[l],
                            jnp.zeros((bp_n, CONV_W - 1, D_FF), hp.dtype))
        fs, bufs = conv_ffn(hs, ffn_w_in[l], ffn_conv_w[l], ffn_conv_b[l], ffn_w_out[l], state_ffn_conv[l])
        xp = xp + g2p * fp
        xs = xs + g2s * fs
        conv_p.append(bufp)
        conv_s.append(bufs)
    y_prompt = rmsnorm(xp, final_norm_g)
    y_sample = rmsnorm(xs, final_norm_g)
    ffn_conv_prompt = jnp.stack(conv_p)
    ffn_conv_sample = jnp.stack(conv_s)
    return (y_prompt, y_sample, swa_k_prompt, swa_v_prompt, swa_k_sample, swa_v_sample,
            hgrn_state_prompt, hgrn_state_sample, ffn_conv_prompt, ffn_conv_sample)
```

```python
import functools
import math

import numpy as np
import jax
import jax.numpy as jnp
from jax import lax
from jax.experimental import pallas as pl
from jax.experimental.pallas import tpu as pltpu

F32 = jnp.float32
BF16 = jnp.bfloat16

EPS = 1e-6
N_MOD = 6
WINDOW = 128
CONV_W = 3
HG_CHUNK = 256
V7X_VMEM_LIMIT = 56 * 1024 * 1024


def _cparams(sem, vmem=None):
    return pltpu.CompilerParams(dimension_semantics=sem, vmem_limit_bytes=vmem)


def _pick(n, cands):
    for c in cands:
        if n % c == 0:
            return c
    return n


def _silu(x):
    return x * jax.nn.sigmoid(x)


def _log_sigmoid(x):
    return jnp.minimum(x, 0.0) - jnp.log1p(jnp.exp(-jnp.abs(x)))


def _logaddexp(a, b):
    return jnp.maximum(a, b) + jnp.log1p(jnp.exp(-jnp.abs(a - b)))


def _ada_kernel(c_ref, w_ref, b_ref, o_ref):
    a = _silu(c_ref[...]).astype(BF16)
    w = w_ref[0].astype(BF16)
    o_ref[0] = jnp.dot(a, w, preferred_element_type=F32) + b_ref[0]


def _ada(c_all, w_ada, b_ada):
    depth, d, n = w_ada.shape
    r = c_all.shape[0]
    tn = _pick(n, (1024, 512, 256, 128))
    return pl.pallas_call(
        _ada_kernel,
        out_shape=jax.ShapeDtypeStruct((depth, r, n), F32),
        grid=(depth, n // tn),
        in_specs=[pl.BlockSpec((r, d), lambda l, j: (0, 0)),
                  pl.BlockSpec((1, d, tn), lambda l, j: (l, 0, j)),
                  pl.BlockSpec((1, 1, tn), lambda l, j: (l, 0, j))],
        out_specs=pl.BlockSpec((1, r, tn), lambda l, j: (l, 0, j)),
        compiler_params=_cparams(("arbitrary", "arbitrary"), V7X_VMEM_LIMIT),
        name="ada",
    )(c_all, w_ada, b_ada.reshape(depth, 1, n))


def _modulate_kernel(x_ref, g_ref, sc_ref, sh_ref, o_ref):
    x = x_ref[...]
    ms = jnp.mean(x * x, axis=-1, keepdims=True)
    y = x * lax.rsqrt(ms + EPS) * g_ref[...]
    o_ref[...] = (y * (1.0 + sc_ref[...]) + sh_ref[...]).astype(o_ref.dtype)


def _modulate(x, g, sc, sh):
    m, d = x.shape
    tm = _pick(m, (512, 256, 128, 64, 32, 16, 8))
    per_tok = sc.shape[0] != 1
    mod_spec = (pl.BlockSpec((tm, d), lambda i: (i, 0)) if per_tok
                else pl.BlockSpec((1, d), lambda i: (0, 0)))
    return pl.pallas_call(
        _modulate_kernel,
        out_shape=jax.ShapeDtypeStruct((m, d), BF16),
        grid=(m // tm,),
        in_specs=[pl.BlockSpec((tm, d), lambda i: (i, 0)),
                  pl.BlockSpec((1, d), lambda i: (0, 0)),
                  mod_spec, mod_spec],
        out_specs=pl.BlockSpec((tm, d), lambda i: (i, 0)),
        compiler_params=_cparams(("arbitrary",), V7X_VMEM_LIMIT),
        name="modulate",
    )(x, g.reshape(1, d), sc, sh)


def _rmsnorm_kernel(x_ref, g_ref, o_ref):
    x = x_ref[...]
    ms = jnp.mean(x * x, axis=-1, keepdims=True)
    o_ref[...] = x * lax.rsqrt(ms + EPS) * g_ref[...]


def _rmsnorm(x, g):
    m, d = x.shape
    tm = _pick(m, (512, 256, 128, 64, 32, 16, 8))
    return pl.pallas_call(
        _rmsnorm_kernel,
        out_shape=jax.ShapeDtypeStruct((m, d), F32),
        grid=(m // tm,),
        in_specs=[pl.BlockSpec((tm, d), lambda i: (i, 0)),
                  pl.BlockSpec((1, d), lambda i: (0, 0))],
        out_specs=pl.BlockSpec((tm, d), lambda i: (i, 0)),
        compiler_params=_cparams(("arbitrary",), V7X_VMEM_LIMIT),
        name="final_norm",
    )(x, g.reshape(1, d))


def _mm_kernel(*refs, resid):
    if resid:
        a_ref, w_ref, x_ref, gate_ref, o_ref, wbf = refs
    else:
        a_ref, w_ref, o_ref, wbf = refs

    @pl.when(pl.program_id(1) == 0)
    def _():
        wbf[...] = w_ref[...].astype(BF16)

    acc = jnp.dot(a_ref[...], wbf[...], preferred_element_type=F32)
    if resid:
        o_ref[...] = x_ref[...] + gate_ref[...] * acc
    else:
        o_ref[...] = acc


def _matmul(a, w, *, x=None, gate=None, tm=None, tn=None, name="matmul"):
    m, k = a.shape
    n = w.shape[1]
    tm = tm or _pick(m, (1024, 512, 256, 128, 64, 32, 16, 8))
    tn = tn or _pick(n, (512, 256, 128))
    resid = x is not None
    in_specs = [pl.BlockSpec((tm, k), lambda j, i: (i, 0)),
                pl.BlockSpec((k, tn), lambda j, i: (0, j))]
    args = [a, w]
    if resid:
        per_tok = gate.shape[0] != 1
        in_specs.append(pl.BlockSpec((tm, tn), lambda j, i: (i, j)))
        in_specs.append(pl.BlockSpec((tm, tn), lambda j, i: (i, j)) if per_tok
                        else pl.BlockSpec((1, tn), lambda j, i: (0, j)))
        args += [x, gate]
    return pl.pallas_call(
        functools.partial(_mm_kernel, resid=resid),
        out_shape=jax.ShapeDtypeStruct((m, n), F32),
        grid=(n // tn, m // tm),
        in_specs=in_specs,
        out_specs=pl.BlockSpec((tm, tn), lambda j, i: (i, j)),
        scratch_shapes=[pltpu.VMEM((k, tn), BF16)],
        compiler_params=_cparams(("arbitrary", "arbitrary"), V7X_VMEM_LIMIT),
        name=name,
    )(*args)


def _gelu_exact(a):
    return 0.5 * a * (1.0 + lax.erf(a * (1.0 / math.sqrt(2.0))))


def _ffn_in_prompt_kernel(a_ref, wu_ref, wg_ref, cw_ref, cb_ref, act_ref, tail_ref, wbf, carry):
    i = pl.program_id(1)
    tf = wu_ref.shape[1]

    @pl.when(i == 0)
    def _():
        wbf[:, :tf] = wu_ref[...].astype(BF16)
        wbf[:, tf:] = wg_ref[...].astype(BF16)
        carry[...] = jnp.zeros_like(carry)

    ug = jnp.dot(a_ref[...], wbf[...], preferred_element_type=F32)
    u = ug[:, :tf]
    g = ug[:, tf:]
    tm = u.shape[0]
    row = lax.broadcasted_iota(jnp.int32, u.shape, 0)
    c = carry[...]
    u1 = jnp.where(row == 0, c[7:8, :], pltpu.roll(u, 1, 0))
    u2 = jnp.where(row == 0, c[6:7, :], jnp.where(row == 1, c[7:8, :], pltpu.roll(u, 2, 0)))
    cw = cw_ref[...]
    conv = cw[0:1, :] * u2 + cw[1:2, :] * u1 + cw[2:3, :] * u + cb_ref[...]
    act_ref[...] = (_gelu_exact(conv) * g).astype(act_ref.dtype)
    carry[...] = u[tm - 8:, :]
    tail_ref[...] = u[tm - 8:, :]


def _ffn_in_prompt(a, w_in, conv_w, conv_b):
    m, d = a.shape
    f = w_in.shape[1] // 2
    tf = _pick(f, (512, 256, 128))
    tm = _pick(m, (1024, 512, 256, 128, 64, 32, 16, 8))
    nf = f // tf
    act, tail = pl.pallas_call(
        _ffn_in_prompt_kernel,
        out_shape=(jax.ShapeDtypeStruct((m, f), BF16), jax.ShapeDtypeStruct((8, f), F32)),
        grid=(nf, m // tm),
        in_specs=[pl.BlockSpec((tm, d), lambda j, i: (i, 0)),
                  pl.BlockSpec((d, tf), lambda j, i: (0, j)),
                  pl.BlockSpec((d, tf), lambda j, i: (0, nf + j)),
                  pl.BlockSpec((CONV_W, tf), lambda j, i: (0, j)),
                  pl.BlockSpec((1, tf), lambda j, i: (0, j))],
        out_specs=(pl.BlockSpec((tm, tf), lambda j, i: (i, j)),
                   pl.BlockSpec((8, tf), lambda j, i: (0, j))),
        scratch_shapes=[pltpu.VMEM((d, 2 * tf), BF16), pltpu.VMEM((8, tf), F32)],
        compiler_params=_cparams(("arbitrary", "arbitrary"), V7X_VMEM_LIMIT),
        name="ffn_in_prompt",
    )(a, w_in, w_in, conv_w, conv_b.reshape(1, f))
    return act, tail[6:8]


def _ffn_in_sample_kernel(a_ref, wu_ref, wg_ref, cw_ref, cb_ref, p1_ref, p2_ref, act_ref, u_ref, *, seq):
    tf = wu_ref.shape[1]
    a = a_ref[...]
    u = jnp.dot(a, wu_ref[...].astype(BF16), preferred_element_type=F32)
    g = jnp.dot(a, wg_ref[...].astype(BF16), preferred_element_type=F32)
    row = lax.broadcasted_iota(jnp.int32, u.shape, 0) % seq
    u1 = jnp.where(row == 0, p1_ref[...], pltpu.roll(u, 1, 0))
    u2 = jnp.where(row < 2, p2_ref[...], pltpu.roll(u, 2, 0))
    cw = cw_ref[...]
    conv = cw[0:1, :] * u2 + cw[1:2, :] * u1 + cw[2:3, :] * u + cb_ref[...]
    act_ref[...] = (_gelu_exact(conv) * g).astype(act_ref.dtype)
    u_ref[...] = u


def _ffn_in_sample(a, w_in, conv_w, conv_b, buf, seq):
    m, d = a.shape
    f = w_in.shape[1] // 2
    b = buf.shape[0]
    tf = _pick(f, (512, 256, 128))
    nf = f // tf
    zeros = jnp.zeros((b, seq - 1, f), F32)
    p1 = jnp.concatenate([buf[:, 1:2], zeros], axis=1).reshape(m, f)
    p2 = jnp.concatenate([buf, zeros[:, :seq - 2]], axis=1).reshape(m, f)
    act, u = pl.pallas_call(
        functools.partial(_ffn_in_sample_kernel, seq=seq),
        out_shape=(jax.ShapeDtypeStruct((m, f), BF16), jax.ShapeDtypeStruct((m, f), F32)),
        grid=(nf,),
        in_specs=[pl.BlockSpec((m, d), lambda j: (0, 0)),
                  pl.BlockSpec((d, tf), lambda j: (0, j)),
                  pl.BlockSpec((d, tf), lambda j: (0, nf + j)),
                  pl.BlockSpec((CONV_W, tf), lambda j: (0, j)),
                  pl.BlockSpec((1, tf), lambda j: (0, j)),
                  pl.BlockSpec((m, tf), lambda j: (0, j)),
                  pl.BlockSpec((m, tf), lambda j: (0, j))],
        out_specs=(pl.BlockSpec((m, tf), lambda j: (0, j)),
                   pl.BlockSpec((m, tf), lambda j: (0, j))),
        compiler_params=_cparams(("arbitrary",), V7X_VMEM_LIMIT),
        name="ffn_in_sample",
    )(a, w_in, w_in, conv_w, conv_b.reshape(1, f), p1, p2)
    new_buf = u.reshape(b, seq, f)[:, seq - (CONV_W - 1):]
    return act, new_buf


def _alibi_slope(h, n_heads):
    return 2.0 ** (-8.0 * (h + 1) / n_heads)


def _swa_prompt_kernel(sink_ref, q_ref, kp_ref, kc_ref, vp_ref, vc_ref, o_ref, *, n_kv, group, hd):
    i = pl.program_id(0)
    n_heads = n_kv * group
    tq = q_ref.shape[0]
    t = lax.broadcasted_iota(jnp.int32, (tq, 2 * tq), 0)
    s = lax.broadcasted_iota(jnp.int32, (tq, 2 * tq), 1)
    dist = t + tq - s
    allowed = (dist >= 0) & (dist < WINDOW) & ((s >= tq) | (i > 0))
    distf = dist.astype(F32)
    scale = hd ** -0.5
    for kv in range(n_kv):
        ks = slice(kv * hd, (kv + 1) * hd)
        k2 = jnp.concatenate([kp_ref[:, ks], kc_ref[:, ks]], axis=0).astype(BF16)
        v2 = jnp.concatenate([vp_ref[:, ks], vc_ref[:, ks]], axis=0).astype(BF16)
        for g in range(group):
            h = kv * group + g
            hs = slice(h * hd, (h + 1) * hd)
            q = (q_ref[:, hs] * scale).astype(BF16)
            sc = lax.dot_general(q, k2, (((1,), (1,)), ((), ())), preferred_element_type=F32)
            sc = sc - _alibi_slope(h, n_heads) * distf
            sc = jnp.where(allowed, sc, -jnp.inf)
            sink = sink_ref[h]
            mx = jnp.maximum(jnp.max(sc, axis=-1, keepdims=True), sink)
            e = jnp.exp(sc - mx)
            den = jnp.sum(e, axis=-1, keepdims=True) + jnp.exp(sink - mx)
            o = jnp.dot(e.astype(BF16), v2, preferred_element_type=F32) / den
            o_ref[:, hs] = o.astype(o_ref.dtype)


def _swa_prompt(qkv, sinks, n_kv, hd):
    l = qkv.shape[0]
    n_heads = sinks.shape[0]
    group = n_heads // n_kv
    dq = n_heads * hd
    dk = n_kv * hd
    nb = l // WINDOW
    kblk = dq // dk
    return pl.pallas_call(
        functools.partial(_swa_prompt_kernel, n_kv=n_kv, group=group, hd=hd),
        out_shape=jax.ShapeDtypeStruct((l, dq), BF16),
        grid_spec=pltpu.PrefetchScalarGridSpec(
            num_scalar_prefetch=1,
            grid=(nb,),
            in_specs=[pl.BlockSpec((WINDOW, dq), lambda i, s: (i, 0)),
                      pl.BlockSpec((WINDOW, dk), lambda i, s: (jnp.maximum(i - 1, 0), kblk)),
                      pl.BlockSpec((WINDOW, dk), lambda i, s: (i, kblk)),
                      pl.BlockSpec((WINDOW, dk), lambda i, s: (jnp.maximum(i - 1, 0), kblk + 1)),
                      pl.BlockSpec((WINDOW, dk), lambda i, s: (i, kblk + 1))],
            out_specs=pl.BlockSpec((WINDOW, dq), lambda i, s: (i, 0)),
        ),
        compiler_params=_cparams(("arbitrary",), V7X_VMEM_LIMIT),
        name="swa_prompt",
    )(sinks, qkv, qkv, qkv, qkv, qkv)


def _swa_sample_kernel(q_ref, kn_ref, vn_ref, kc_ref, vc_ref, sink_ref, o_ref, *, n_kv, group, hd, seq):
    n_heads = n_kv * group
    bs = q_ref.shape[0]
    rows = group * seq
    cr = kc_ref.shape[1]
    scale = hd ** -0.5
    r = lax.broadcasted_iota(jnp.int32, (1, rows, cr), 1)
    t = r % seq
    gi = r // seq
    ci = lax.broadcasted_iota(jnp.int32, (1, rows, cr), 2)
    dist_c = (cr + t - ci).astype(F32)
    ok_c = ci > t + (cr - WINDOW)
    r1 = lax.broadcasted_iota(jnp.int32, (1, rows, 1), 1)
    t1 = r1 % seq
    g1 = (r1 // seq).astype(F32)
    for kv in range(n_kv):
        slope_c = jnp.exp2(-8.0 * (kv * group + gi.astype(F32) + 1.0) / n_heads)
        slope_1 = jnp.exp2(-8.0 * (kv * group + g1 + 1.0) / n_heads)
        ks = slice(kv * hd, (kv + 1) * hd)
        q = q_ref[:, kv] * scale
        kc = kc_ref[:, :, ks].astype(BF16)
        vc = vc_ref[:, :, ks].astype(BF16)
        kn = kn_ref[:, kv]
        vn = vn_ref[:, kv]
        sink = sink_ref[kv]
        s_c = jnp.einsum('bqd,bkd->bqk', q.astype(BF16), kc, preferred_element_type=F32)
        s_c = jnp.where(ok_c, s_c - slope_c * dist_c, -jnp.inf)
        mx = jnp.maximum(jnp.max(s_c, axis=-1, keepdims=True), sink[None])
        s_n = []
        for j in range(seq):
            sj = jnp.sum(q * kn[:, j:j + 1, :], axis=-1, keepdims=True)
            sj = jnp.where(t1 >= j, sj - slope_1 * (t1 - j).astype(F32), -jnp.inf)
            s_n.append(sj)
            mx = jnp.maximum(mx, sj)
        e_c = jnp.exp(s_c - mx)
        den = jnp.sum(e_c, axis=-1, keepdims=True) + jnp.exp(sink[None] - mx)
        o = jnp.einsum('bqk,bkd->bqd', e_c.astype(BF16), vc, preferred_element_type=F32)
        for j in range(seq):
            ej = jnp.exp(s_n[j] - mx)
            den = den + ej
            o = o + ej * vn[:, j:j + 1, :]
        o_ref[:, kv] = (o / den).astype(o_ref.dtype)


def _swa_sample(q, k_new, v_new, cache_k, cache_v, sinks, n_kv, hd, seq):
    b, cr = cache_k.shape[0], cache_k.shape[1]
    n_heads = sinks.shape[0]
    group = n_heads // n_kv
    rows = group * seq
    qr = q.reshape(b, seq, n_kv, group, hd).transpose(0, 2, 3, 1, 4).reshape(b, n_kv, rows, hd)
    kn = k_new.reshape(b, seq, n_kv, hd).transpose(0, 2, 1, 3)
    vn = v_new.reshape(b, seq, n_kv, hd).transpose(0, 2, 1, 3)
    kc = cache_k.reshape(b, cr, n_kv * hd)
    vc = cache_v.reshape(b, cr, n_kv * hd)
    sk = jnp.repeat(sinks.reshape(n_kv, group), seq, axis=1).reshape(n_kv, rows, 1)
    bs = _pick(b, (8, 4, 2, 1))
    o = pl.pallas_call(
        functools.partial(_swa_sample_kernel, n_kv=n_kv, group=group, hd=hd, seq=seq),
        out_shape=jax.ShapeDtypeStruct((b, n_kv, rows, hd), BF16),
        grid=(b // bs,),
        in_specs=[pl.BlockSpec((bs, n_kv, rows, hd), lambda i: (i, 0, 0, 0)),
                  pl.BlockSpec((bs, n_kv, seq, hd), lambda i: (i, 0, 0, 0)),
                  pl.BlockSpec((bs, n_kv, seq, hd), lambda i: (i, 0, 0, 0)),
                  pl.BlockSpec((bs, cr, n_kv * hd), lambda i: (i, 0, 0)),
                  pl.BlockSpec((bs, cr, n_kv * hd), lambda i: (i, 0, 0)),
                  pl.BlockSpec((n_kv, rows, 1), lambda i: (0, 0, 0))],
        out_specs=pl.BlockSpec((bs, n_kv, rows, hd), lambda i: (i, 0, 0, 0)),
        compiler_params=_cparams(("arbitrary",), V7X_VMEM_LIMIT),
        name="swa_sample",
    )(qr, kn, vn, kc, vc, sk)
    return o.reshape(b, n_kv, group, seq, hd).transpose(0, 3, 1, 2, 4).reshape(b * seq, n_heads * hd)


def _lower_bound(p_ref):
    p0 = p_ref[0:1, :]
    p1 = p_ref[1:2, :]
    mx = jnp.maximum(p0, p1)
    e0 = jnp.exp(p0 - mx)
    e1 = jnp.exp(p1 - mx)
    return e1 / (e0 + e1)


def _hgrn_gates(fr, lb):
    logf = _logaddexp(jnp.log(lb), jnp.log1p(-lb) + _log_sigmoid(fr))
    k = (1.0 - lb) * jax.nn.sigmoid(-fr)
    return logf, k


def _group_norm_gate(o, gr, ng):
    o = o * lax.rsqrt(jnp.mean(o * o, axis=-1, keepdims=True) + EPS) * ng
    return o * _silu(gr)


def _level_table(c):
    t = np.arange(c)[:, None]
    s = np.arange(c)[None, :]
    x = t ^ s
    lv = np.floor(np.log2(np.maximum(x, 1))).astype(np.int32)
    lv = np.where(s < t, lv, np.where(s == t, int(math.log2(c)), -1))
    return jnp.asarray(lv, dtype=jnp.int32)


def _group_ref_rows(p, m, row):
    c, w = p.shape
    if m == 1:
        return jnp.where((row & 1) == 1, pltpu.roll(p, 1, 0), p)
    if m == 2:
        r4 = row & 3
        return jnp.where(r4 == 0, pltpu.roll(p, c - 1, 0),
                         jnp.where(r4 == 1, p,
                                   jnp.where(r4 == 2, pltpu.roll(p, 1, 0), pltpu.roll(p, 2, 0))))
    g = c // (2 * m)
    p3 = p.reshape(g, 2 * m, w)
    return jnp.broadcast_to(p3[:, m - 1:m, :], (g, 2 * m, w)).reshape(c, w)


def _hgrn_chunk(q, k, v, logf, st, lvl):
    c = q.shape[0]
    nlev = int(math.log2(c))
    row = lax.broadcasted_iota(jnp.int32, q.shape, 0)
    nt = (((1,), (1,)), ((), ()))
    a = jnp.where(lvl == nlev,
                  lax.dot_general(q.astype(BF16), k.astype(BF16), nt, preferred_element_type=F32), 0.0)
    p = logf
    for l in range(nlev):
        m = 1 << l
        gref = _group_ref_rows(p, m, row)
        odd = ((row >> l) & 1) == 1
        e = jnp.exp(jnp.where(odd, p, gref - p))
        a_l = lax.dot_general((q * e).astype(BF16), (k * e).astype(BF16), nt, preferred_element_type=F32)
        a = jnp.where(lvl == l, a_l, a)
        p = p + jnp.where(odd, gref, 0.0)
    b = p
    b_end = b[c - 1:c, :]
    o = lax.dot_general((q * jnp.exp(b)).astype(BF16), st.astype(BF16), nt, preferred_element_type=F32)
    o = o + jnp.dot(a.astype(BF16), v.astype(BF16), preferred_element_type=F32)
    k_end = (k * jnp.exp(b_end - b)).astype(BF16)
    upd = lax.dot_general(v.astype(BF16), k_end, (((0,), (0,)), ((), ())), preferred_element_type=F32)
    st_new = st * jnp.exp(b_end) + upd
    return o, st_new


def _hgrn_prompt_kernel(a_ref, wq_ref, wf_ref, wi_ref, wg_ref, p_ref, ng_ref, lvl_ref,
                        o_ref, s_ref, wbf, st, *, chunk):
    i = pl.program_id(1)
    dk = wq_ref.shape[1]

    @pl.when(i == 0)
    def _():
        wbf[:, 0 * dk:1 * dk] = wq_ref[...].astype(BF16)
        wbf[:, 1 * dk:2 * dk] = wf_ref[...].astype(BF16)
        wbf[:, 2 * dk:3 * dk] = wi_ref[...].astype(BF16)
        wbf[:, 3 * dk:4 * dk] = wg_ref[...].astype(BF16)
        st[...] = jnp.zeros_like(st)

    proj = jnp.dot(a_ref[...], wbf[...], preferred_element_type=F32)
    lb = _lower_bound(p_ref)
    ng = ng_ref[...]
    lvl = lvl_ref[...]
    tm = proj.shape[0]
    s_t = st[...]
    for c0 in range(0, tm, chunk):
        rs = slice(c0, c0 + chunk)
        q = _silu(proj[rs, 0 * dk:1 * dk])
        logf, k = _hgrn_gates(proj[rs, 1 * dk:2 * dk], lb)
        v = proj[rs, 2 * dk:3 * dk]
        o, s_t = _hgrn_chunk(q, k, v, logf, s_t, lvl)
        o_ref[rs, :] = _group_norm_gate(o, proj[rs, 3 * dk:4 * dk], ng).astype(o_ref.dtype)
    st[...] = s_t
    s_ref[0, 0] = s_t.T


def _hgrn_prompt(a, w_in, lower_bounds, norm_g, n_heads, dk):
    m, d = a.shape
    chunk = min(HG_CHUNK, m)
    tm = _pick(m, (512, 256, 128, 64, 32, 16, 8))
    tm = max(tm, chunk)
    lvl = _level_table(chunk)
    w_spec = lambda blk: pl.BlockSpec((d, dk), lambda h, i: (0, blk * n_heads + h))
    o, s = pl.pallas_call(
        functools.partial(_hgrn_prompt_kernel, chunk=chunk),
        out_shape=(jax.ShapeDtypeStruct((m, n_heads * dk), BF16),
                   jax.ShapeDtypeStruct((1, n_heads, dk, dk), F32)),
        grid=(n_heads, m // tm),
        in_specs=[pl.BlockSpec((tm, d), lambda h, i: (i, 0)),
                  w_spec(0), w_spec(1), w_spec(2), w_spec(3),
                  pl.BlockSpec((lower_bounds.shape[0], dk), lambda h, i: (0, h)),
                  pl.BlockSpec((1, dk), lambda h, i: (0, h)),
                  pl.BlockSpec((chunk, chunk), lambda h, i: (0, 0))],
        out_specs=(pl.BlockSpec((tm, dk), lambda h, i: (i, h)),
                   pl.BlockSpec((1, 1, dk, dk), lambda h, i: (0, h, 0, 0))),
        scratch_shapes=[pltpu.VMEM((d, 4 * dk), BF16), pltpu.VMEM((dk, dk), F32)],
        compiler_params=_cparams(("arbitrary", "arbitrary"), V7X_VMEM_LIMIT),
        name="hgrn_prompt",
    )(a, w_in, w_in, w_in, w_in, lower_bounds, norm_g.reshape(1, -1), lvl)
    return o, s


def _hgrn_sample_kernel(qr_ref, fr_ref, ir_ref, gr_ref, p_ref, ng_ref, s0_ref, o_ref, s_ref, *, seq):
    bs = s0_ref.shape[0]
    rows = bs * seq
    lb = _lower_bound(p_ref)
    q = _silu(qr_ref[...])
    logf, k = _hgrn_gates(fr_ref[...], lb)
    v = ir_ref[...]
    row = lax.broadcasted_iota(jnp.int32, q.shape, 0)
    t = row % seq
    b = logf
    d = 1
    while d < seq:
        b = b + jnp.where(t >= d, pltpu.roll(b, d, 0), 0.0)
        d *= 2
    o = jnp.sum(q * k, axis=-1, keepdims=True) * v
    for d in range(1, seq):
        ok = t >= d
        arg = jnp.where(ok, b - pltpu.roll(b, d, 0), 0.0)
        w = jnp.sum(q * pltpu.roll(k, d, 0) * jnp.exp(arg), axis=-1, keepdims=True)
        o = o + jnp.where(ok, w, 0.0) * pltpu.roll(v, d, 0)
    qe = (q * jnp.exp(b)).astype(BF16)
    b_end = b
    d = 1
    while d < seq:
        b_end = jnp.where(t < seq - d, pltpu.roll(b_end, rows - d, 0), b_end)
        d *= 2
    k_end = k * jnp.exp(b_end - b)
    k_end_t = k_end.T
    dec_t = jnp.exp(b_end).T
    lane = lax.broadcasted_iota(jnp.int32, k_end_t.shape, 1)
    vb = v.astype(BF16)
    o_inter = []
    for s in range(bs):
        s0 = s0_ref[s, 0]
        in_seq = (lane >= s * seq) & (lane < (s + 1) * seq)
        upd = jnp.dot(jnp.where(in_seq, k_end_t, 0.0).astype(BF16), vb, preferred_element_type=F32)
        s_ref[s, 0] = s0 * dec_t[:, s * seq:s * seq + 1] + upd
        if s % (8 // seq) == 0:
            r0 = s * seq
            parts = []
        oi = jnp.dot(qe[r0:r0 + 8, :], s0.astype(BF16), preferred_element_type=F32)
        parts.append(oi)
        if s % (8 // seq) == (8 // seq) - 1:
            r8 = lax.broadcasted_iota(jnp.int32, oi.shape, 0) // seq
            acc = parts[0]
            for j in range(1, len(parts)):
                acc = jnp.where(r8 == j, parts[j], acc)
            o_inter.append(acc)
    o = o + jnp.concatenate(o_inter, axis=0)
    o_ref[...] = _group_norm_gate(o, gr_ref[...], ng_ref[...]).astype(o_ref.dtype)


def _hgrn_sample(proj, lower_bounds, norm_g, state, seq):
    b, n_heads, dk, dv = state.shape
    m = proj.shape[0]
    bs = _pick(b, (32, 16, 8, 4, 2))
    rows = bs * seq
    col = lambda blk: pl.BlockSpec((rows, dk), lambda h, i: (i, blk * n_heads + h))
    o, s = pl.pallas_call(
        functools.partial(_hgrn_sample_kernel, seq=seq),
        out_shape=(jax.ShapeDtypeStruct((m, n_heads * dv), BF16),
                   jax.ShapeDtypeStruct(state.shape, F32)),
        grid=(n_heads, b // bs),
        in_specs=[col(0), col(1), col(2), col(3),
                  pl.BlockSpec((lower_bounds.shape[0], dk), lambda h, i: (0, h)),
                  pl.BlockSpec((1, dv), lambda h, i: (0, h)),
                  pl.BlockSpec((bs, 1, dk, dv), lambda h, i: (i, h, 0, 0))],
        out_specs=(pl.BlockSpec((rows, dv), lambda h, i: (i, h)),
                   pl.BlockSpec((bs, 1, dk, dv), lambda h, i: (i, h, 0, 0))),
        compiler_params=_cparams(("arbitrary", "arbitrary"), V7X_VMEM_LIMIT),
        name="hgrn_sample",
    )(proj, proj, proj, proj, lower_bounds, norm_g.reshape(1, -1), state)
    return o, s


def kernel(x_prompt, x_sample, cache_swa_k, cache_swa_v, state_hgrn, state_ffn_conv, c_prompt, c_sample,
           norm1_g, norm2_g, w_ada, b_ada, attn_w_qkv, attn_w_o, attn_sinks,
           hgrn_w_in, hgrn_lower_bounds, hgrn_norm_g, hgrn_w_o,
           ffn_w_in, ffn_conv_w, ffn_conv_b, ffn_w_out, final_norm_g):
    bp, lp, d = x_prompt.shape
    bsmp, ls, _ = x_sample.shape
    assert bp == 1, "the prompt kernels carry one sequence"
    depth = w_ada.shape[0]
    n_kv, hd = cache_swa_k.shape[2], cache_swa_k.shape[3]
    n_heads = attn_sinks.shape[0]
    dq = n_heads * hd
    dkv = n_kv * hd
    hg_heads, hg_dk = state_hgrn.shape[1], state_hgrn.shape[2]
    keep = min(WINDOW, lp)

    nc = bp + bsmp
    pad = (-nc) % 8
    c_all = jnp.concatenate([c_prompt, c_sample, jnp.zeros((pad, d), F32)], axis=0)
    mod = _ada(c_all, w_ada, b_ada)

    xp = x_prompt.reshape(lp, d)
    xs = x_sample.reshape(bsmp * ls, d)
    conv_p, conv_s = [], []
    outs = {}
    for l in range(depth):
        mp = [mod[l, 0:1, j * d:(j + 1) * d] for j in range(N_MOD)]
        ms = [jnp.repeat(mod[l, bp:nc, j * d:(j + 1) * d], ls, axis=0) for j in range(N_MOD)]
        hp = _modulate(xp, norm1_g[l], mp[1], mp[0])
        hs = _modulate(xs, norm1_g[l], ms[1], ms[0])
        if l % 2 == 0:
            qkv_p = _matmul(hp, attn_w_qkv, name="qkv_prompt")
            qkv_s = _matmul(hs, attn_w_qkv, name="qkv_sample")
            op = _swa_prompt(qkv_p, attn_sinks, n_kv, hd)
            os_ = _swa_sample(qkv_s[:, :dq], qkv_s[:, dq:dq + dkv], qkv_s[:, dq + dkv:],
                              cache_swa_k, cache_swa_v, attn_sinks, n_kv, hd, ls)
            outs['swa_k_prompt'] = qkv_p[lp - keep:, dq:dq + dkv].reshape(bp, keep, n_kv, hd)
            outs['swa_v_prompt'] = qkv_p[lp - keep:, dq + dkv:].reshape(bp, keep, n_kv, hd)
            outs['swa_k_sample'] = qkv_s[:, dq:dq + dkv].reshape(bsmp, ls, n_kv, hd)
            outs['swa_v_sample'] = qkv_s[:, dq + dkv:].reshape(bsmp, ls, n_kv, hd)
            w_o = attn_w_o
        else:
            op, sp = _hgrn_prompt(hp, hgrn_w_in, hgrn_lower_bounds, hgrn_norm_g, hg_heads, hg_dk)
            proj_s = _matmul(hs, hgrn_w_in, name="hgrn_in_sample")
            os_, ss = _hgrn_sample(proj_s, hgrn_lower_bounds, hgrn_norm_g, state_hgrn, ls)
            outs['hgrn_state_prompt'] = sp
            outs['hgrn_state_sample'] = ss
            w_o = hgrn_w_o
        xp = _matmul(op, w_o, x=xp, gate=mp[2], name="mix_out_prompt")
        xs = _matmul(os_, w_o, x=xs, gate=ms[2], name="mix_out_sample")
        hp = _modulate(xp, norm2_g[l], mp[4], mp[3])
        hs = _modulate(xs, norm2_g[l], ms[4], ms[3])
        actp, bufp = _ffn_in_prompt(hp, ffn_w_in[l], ffn_conv_w[l], ffn_conv_b[l])
        acts, bufs = _ffn_in_sample(hs, ffn_w_in[l], ffn_conv_w[l], ffn_conv_b[l], state_ffn_conv[l], ls)
        xp = _matmul(actp, ffn_w_out[l], x=xp, gate=mp[5], tn=256, name="ffn_out_prompt")
        xs = _matmul(acts, ffn_w_out[l], x=xs, gate=ms[5], tn=256, name="ffn_out_sample")
        conv_p.append(bufp[None])
        conv_s.append(bufs)
    y_prompt = _rmsnorm(xp, final_norm_g).reshape(bp, lp, d)
    y_sample = _rmsnorm(xs, final_norm_g).reshape(bsmp, ls, d)
    return (y_prompt, y_sample,
            outs['swa_k_prompt'], outs['swa_v_prompt'], outs['swa_k_sample'], outs['swa_v_sample'],
            outs['hgrn_state_prompt'], outs['hgrn_state_sample'],
            jnp.stack(conv_p), jnp.stack(conv_s))
```

```python
import functools
import math

import numpy as np
import jax
import jax.numpy as jnp
from jax import lax
from jax.experimental import pallas as pl
from jax.experimental.pallas import tpu as pltpu

F32 = jnp.float32
BF16 = jnp.bfloat16

EPS = 1e-6
LOG2E = math.log2(math.e)
N_MOD = 6
WINDOW = 128
CONV_W = 3
HG_CHUNK = 256
SWA_WAVE = 8
V7X_VMEM_LIMIT = 56 * 1024 * 1024


def _cparams(sem, vmem=None):
    return pltpu.CompilerParams(dimension_semantics=sem, vmem_limit_bytes=vmem)


def _pick(n, cands):
    for c in cands:
        if n % c == 0:
            return c
    return n


def _silu(x):
    return x * jax.nn.sigmoid(x)


def _log_sigmoid(x):
    return jnp.minimum(x, 0.0) - jnp.log1p(jnp.exp(-jnp.abs(x)))


def _logaddexp(a, b):
    return jnp.maximum(a, b) + jnp.log1p(jnp.exp(-jnp.abs(a - b)))


def _ada_kernel(c_ref, w_ref, b_ref, o_ref):
    a = _silu(c_ref[...]).astype(BF16)
    w = w_ref[0].astype(BF16)
    o_ref[0] = jnp.dot(a, w, preferred_element_type=F32) + b_ref[0]


def _ada(c_all, w_ada, b_ada):
    depth, d, n = w_ada.shape
    r = c_all.shape[0]
    tn = _pick(n, (1024, 512, 256, 128))
    return pl.pallas_call(
        _ada_kernel,
        out_shape=jax.ShapeDtypeStruct((depth, r, n), F32),
        grid=(depth, n // tn),
        in_specs=[pl.BlockSpec((r, d), lambda l, j: (0, 0)),
                  pl.BlockSpec((1, d, tn), lambda l, j: (l, 0, j)),
                  pl.BlockSpec((1, 1, tn), lambda l, j: (l, 0, j))],
        out_specs=pl.BlockSpec((1, r, tn), lambda l, j: (l, 0, j)),
        compiler_params=_cparams(("arbitrary", "arbitrary"), V7X_VMEM_LIMIT),
        name="ada",
    )(c_all, w_ada, b_ada.reshape(depth, 1, n))


def _mod_rows(sample, tm, n_sample):
    if sample:
        return tm, (lambda i: i)
    return 8, (lambda i: n_sample // 8)


def _modulate_kernel(x_ref, g_ref, sc_ref, sh_ref, o_ref, *, sample):
    x = x_ref[...]
    ms = jnp.mean(x * x, axis=-1, keepdims=True)
    y = x * lax.rsqrt(ms + EPS) * g_ref[...]
    sc = sc_ref[...] if sample else sc_ref[0:1, :]
    sh = sh_ref[...] if sample else sh_ref[0:1, :]
    o_ref[...] = (y * (1.0 + sc) + sh).astype(o_ref.dtype)


def _modulate(x, g_all, mod, l, shift_col, scale_col, sample, n_sample):
    m, d = x.shape
    tm = _pick(m, (512, 256, 128, 64, 32, 16, 8))
    rows, rmap = _mod_rows(sample, tm, n_sample)
    mspec = lambda col: pl.BlockSpec((None, rows, d), lambda i: (l, rmap(i), col))
    return pl.pallas_call(
        functools.partial(_modulate_kernel, sample=sample),
        out_shape=jax.ShapeDtypeStruct((m, d), BF16),
        grid=(m // tm,),
        in_specs=[pl.BlockSpec((tm, d), lambda i: (i, 0)),
                  pl.BlockSpec((None, 1, d), lambda i: (l, 0, 0)),
                  mspec(scale_col), mspec(shift_col)],
        out_specs=pl.BlockSpec((tm, d), lambda i: (i, 0)),
        compiler_params=_cparams(("arbitrary",), V7X_VMEM_LIMIT),
        name="modulate",
    )(x, g_all.reshape(g_all.shape[0], 1, d), mod, mod)


def _rmsnorm_kernel(x_ref, g_ref, o_ref):
    x = x_ref[...]
    ms = jnp.mean(x * x, axis=-1, keepdims=True)
    o_ref[...] = x * lax.rsqrt(ms + EPS) * g_ref[...]


def _rmsnorm(x, g):
    m, d = x.shape
    tm = _pick(m, (512, 256, 128, 64, 32, 16, 8))
    return pl.pallas_call(
        _rmsnorm_kernel,
        out_shape=jax.ShapeDtypeStruct((m, d), F32),
        grid=(m // tm,),
        in_specs=[pl.BlockSpec((tm, d), lambda i: (i, 0)),
                  pl.BlockSpec((1, d), lambda i: (0, 0))],
        out_specs=pl.BlockSpec((tm, d), lambda i: (i, 0)),
        compiler_params=_cparams(("arbitrary",), V7X_VMEM_LIMIT),
        name="final_norm",
    )(x, g.reshape(1, d))


def _mm_kernel(*refs, resid, sample):
    if resid:
        a_ref, w_ref, x_ref, gate_ref, o_ref, wbf = refs
    else:
        a_ref, w_ref, o_ref, wbf = refs

    @pl.when(pl.program_id(1) == 0)
    def _():
        wbf[...] = w_ref[...].astype(BF16)

    acc = jnp.dot(a_ref[...], wbf[...], preferred_element_type=F32)
    if resid:
        gate = gate_ref[...] if sample else gate_ref[0:1, :]
        o_ref[...] = x_ref[...] + gate * acc
    else:
        o_ref[...] = acc


def _matmul(a, w, *, w_layer=None, x=None, mod=None, mod_layer=0, gate_col=None, sample=False,
            n_sample=None, tm=None, tn=None, name="matmul"):
    m, k = a.shape
    n = w.shape[-1]
    tm = min(tm, m) if tm else _pick(m, (1024, 512, 256, 128, 64, 32, 16, 8))
    tn = min(tn, n) if tn else _pick(n, (512, 256, 128))
    assert m % tm == 0 and n % tn == 0
    resid = x is not None
    if w_layer is None:
        w_spec = pl.BlockSpec((k, tn), lambda j, i: (0, j))
    else:
        w_spec = pl.BlockSpec((None, k, tn), lambda j, i: (w_layer, 0, j))
    in_specs = [pl.BlockSpec((tm, k), lambda j, i: (i, 0)), w_spec]
    args = [a, w]
    if resid:
        rows, rmap = _mod_rows(sample, tm, n_sample)
        cb = gate_col * (n // tn)
        in_specs.append(pl.BlockSpec((tm, tn), lambda j, i: (i, j)))
        in_specs.append(pl.BlockSpec((None, rows, tn), lambda j, i: (mod_layer, rmap(i), cb + j)))
        args += [x, mod]
    return pl.pallas_call(
        functools.partial(_mm_kernel, resid=resid, sample=sample),
        out_shape=jax.ShapeDtypeStruct((m, n), F32),
        grid=(n // tn, m // tm),
        in_specs=in_specs,
        out_specs=pl.BlockSpec((tm, tn), lambda j, i: (i, j)),
        scratch_shapes=[pltpu.VMEM((k, tn), BF16)],
        compiler_params=_cparams(("arbitrary", "arbitrary"), V7X_VMEM_LIMIT),
        name=name,
    )(*args)


def _gelu_exact(a):
    return 0.5 * a * (1.0 + lax.erf(a * (1.0 / math.sqrt(2.0))))


def _ffn_in_prompt_kernel(a_ref, wu_ref, wg_ref, cw_ref, cb_ref, act_ref, tail_ref, wbf, carry):
    i = pl.program_id(1)
    tf = wu_ref.shape[1]

    @pl.when(i == 0)
    def _():
        wbf[:, :tf] = wu_ref[...].astype(BF16)
        wbf[:, tf:] = wg_ref[...].astype(BF16)
        carry[...] = jnp.zeros_like(carry)

    ug = jnp.dot(a_ref[...], wbf[...], preferred_element_type=F32)
    u = ug[:, :tf]
    g = ug[:, tf:]
    tm = u.shape[0]
    row = lax.broadcasted_iota(jnp.int32, u.shape, 0)
    c = carry[...]
    u1 = jnp.where(row == 0, c[7:8, :], pltpu.roll(u, 1, 0))
    u2 = jnp.where(row == 0, c[6:7, :], jnp.where(row == 1, c[7:8, :], pltpu.roll(u, 2, 0)))
    cw = cw_ref[...]
    conv = cw[0:1, :] * u2 + cw[1:2, :] * u1 + cw[2:3, :] * u + cb_ref[...]
    act_ref[...] = (_gelu_exact(conv) * g).astype(act_ref.dtype)
    carry[...] = u[tm - 8:, :]
    tail_ref[...] = u[tm - 8:, :]


def _ffn_in_prompt(a, w_in, conv_w, conv_b, l):
    m, d = a.shape
    f = w_in.shape[-1] // 2
    tf = _pick(f, (512, 256, 128))
    tm = _pick(m, (1024, 512, 256, 128, 64, 32, 16, 8))
    nf = f // tf
    act, tail = pl.pallas_call(
        _ffn_in_prompt_kernel,
        out_shape=(jax.ShapeDtypeStruct((m, f), BF16), jax.ShapeDtypeStruct((8, f), F32)),
        grid=(nf, m // tm),
        in_specs=[pl.BlockSpec((tm, d), lambda j, i: (i, 0)),
                  pl.BlockSpec((None, d, tf), lambda j, i: (l, 0, j)),
                  pl.BlockSpec((None, d, tf), lambda j, i: (l, 0, nf + j)),
                  pl.BlockSpec((None, CONV_W, tf), lambda j, i: (l, 0, j)),
                  pl.BlockSpec((None, 1, tf), lambda j, i: (l, 0, j))],
        out_specs=(pl.BlockSpec((tm, tf), lambda j, i: (i, j)),
                   pl.BlockSpec((8, tf), lambda j, i: (0, j))),
        scratch_shapes=[pltpu.VMEM((d, 2 * tf), BF16), pltpu.VMEM((8, tf), F32)],
        compiler_params=_cparams(("arbitrary", "arbitrary"), V7X_VMEM_LIMIT),
        name="ffn_in_prompt",
    )(a, w_in, w_in, conv_w, conv_b.reshape(conv_b.shape[0], 1, f))
    return act, tail[6:8]


def _ffn_in_sample_kernel(a_ref, wu_ref, wg_ref, cw_ref, cb_ref, p1_ref, p2_ref, act_ref, u_ref, *, seq):
    tf = wu_ref.shape[1]
    a = a_ref[...]
    u = jnp.dot(a, wu_ref[...].astype(BF16), preferred_element_type=F32)
    g = jnp.dot(a, wg_ref[...].astype(BF16), preferred_element_type=F32)
    row = lax.broadcasted_iota(jnp.int32, u.shape, 0) % seq
    u1 = jnp.where(row == 0, p1_ref[...], pltpu.roll(u, 1, 0))
    u2 = jnp.where(row < 2, p2_ref[...], pltpu.roll(u, 2, 0))
    cw = cw_ref[...]
    conv = cw[0:1, :] * u2 + cw[1:2, :] * u1 + cw[2:3, :] * u + cb_ref[...]
    act_ref[...] = (_gelu_exact(conv) * g).astype(act_ref.dtype)
    u_ref[...] = u


def _ffn_in_sample(a, w_in, conv_w, conv_b, l, buf, seq):
    m, d = a.shape
    f = w_in.shape[-1] // 2
    b = buf.shape[0]
    tf = _pick(f, (512, 256, 128))
    nf = f // tf
    zeros = jnp.zeros((b, seq - 1, f), F32)
    p1 = jnp.concatenate([buf[:, 1:2], zeros], axis=1).reshape(m, f)
    p2 = jnp.concatenate([buf, zeros[:, :seq - 2]], axis=1).reshape(m, f)
    act, u = pl.pallas_call(
        functools.partial(_ffn_in_sample_kernel, seq=seq),
        out_shape=(jax.ShapeDtypeStruct((m, f), BF16), jax.ShapeDtypeStruct((m, f), F32)),
        grid=(nf,),
        in_specs=[pl.BlockSpec((m, d), lambda j: (0, 0)),
                  pl.BlockSpec((None, d, tf), lambda j: (l, 0, j)),
                  pl.BlockSpec((None, d, tf), lambda j: (l, 0, nf + j)),
                  pl.BlockSpec((None, CONV_W, tf), lambda j: (l, 0, j)),
                  pl.BlockSpec((None, 1, tf), lambda j: (l, 0, j)),
                  pl.BlockSpec((m, tf), lambda j: (0, j)),
                  pl.BlockSpec((m, tf), lambda j: (0, j))],
        out_specs=(pl.BlockSpec((m, tf), lambda j: (0, j)),
                   pl.BlockSpec((m, tf), lambda j: (0, j))),
        compiler_params=_cparams(("arbitrary",), V7X_VMEM_LIMIT),
        name="ffn_in_sample",
    )(a, w_in, w_in, conv_w, conv_b.reshape(conv_b.shape[0], 1, f), p1, p2)
    new_buf = u.reshape(b, seq, f)[:, seq - (CONV_W - 1):]
    return act, new_buf


def _alibi_slope(h, n_heads):
    return 2.0 ** (-8.0 * (h + 1) / n_heads)


def _swa_prompt_kernel(sink_ref, q_ref, kp_ref, kc_ref, vp_ref, vc_ref, o_ref, *, n_kv, group, hd):
    i = pl.program_id(0)
    n_heads = n_kv * group
    tq = q_ref.shape[0]
    t = lax.broadcasted_iota(jnp.int32, (tq, 2 * tq), 0)
    s = lax.broadcasted_iota(jnp.int32, (tq, 2 * tq), 1)
    dist = t + tq - s
    allowed = (dist >= 0) & (dist < WINDOW) & ((s >= tq) | (i > 0))
    sidx = s[0:1, :].astype(F32)
    trow = (t[:, 0:1] + tq).astype(F32)
    qscale = hd ** -0.5 * LOG2E
    nt = (((1,), (1,)), ((), ()))
    def scores(kv):
        ks = slice(kv * hd, (kv + 1) * hd)
        k2 = jnp.concatenate([kp_ref[:, ks], kc_ref[:, ks]], axis=0).astype(BF16)
        q4 = jnp.concatenate([(q_ref[:, h * hd:(h + 1) * hd] * qscale).astype(BF16)
                              for h in range(kv * group, (kv + 1) * group)], axis=0)
        return lax.dot_general(q4, k2, nt, preferred_element_type=F32)

    for kv0 in range(0, n_kv, SWA_WAVE):
        kvs = list(range(kv0, min(kv0 + SWA_WAVE, n_kv)))
        sc4s = [scores(kv) for kv in kvs]
        heads = [kv * group + g for kv in kvs for g in range(group)]
        slopes = [_alibi_slope(h, n_heads) * LOG2E for h in heads]
        scs = [jnp.where(allowed, sc4s[j // group][(j % group) * tq:(j % group + 1) * tq] + slopes[j] * sidx,
                         -jnp.inf) for j in range(len(heads))]
        sinks = [sink_ref[h] * LOG2E + slopes[j] * trow for j, h in enumerate(heads)]
        mxs = [jnp.maximum(jnp.max(scs[j], axis=-1, keepdims=True), sinks[j]) for j in range(len(heads))]
        es = [jnp.exp2(scs[j] - mxs[j]) for j in range(len(heads))]
        dens = [jnp.sum(es[j], axis=-1, keepdims=True) + jnp.exp2(sinks[j] - mxs[j])
                for j in range(len(heads))]
        es = [e.astype(BF16) for e in es]
        o4s = []
        for i, kv in enumerate(kvs):
            ks = slice(kv * hd, (kv + 1) * hd)
            v2 = jnp.concatenate([vp_ref[:, ks], vc_ref[:, ks]], axis=0).astype(BF16)
            o4s.append(jnp.dot(jnp.concatenate(es[i * group:(i + 1) * group], axis=0), v2,
                               preferred_element_type=F32))
        for j, h in enumerate(heads):
            o = o4s[j // group][(j % group) * tq:(j % group + 1) * tq]
            o_ref[:, h * hd:(h + 1) * hd] = (o / dens[j]).astype(o_ref.dtype)


def _swa_prompt(qkv, sinks, n_kv, hd):
    l = qkv.shape[0]
    n_heads = sinks.shape[0]
    group = n_heads // n_kv
    dq = n_heads * hd
    dk = n_kv * hd
    nb = l // WINDOW
    kblk = dq // dk
    return pl.pallas_call(
        functools.partial(_swa_prompt_kernel, n_kv=n_kv, group=group, hd=hd),
        out_shape=jax.ShapeDtypeStruct((l, dq), BF16),
        grid_spec=pltpu.PrefetchScalarGridSpec(
            num_scalar_prefetch=1,
            grid=(nb,),
            in_specs=[pl.BlockSpec((WINDOW, dq), lambda i, s: (i, 0)),
                      pl.BlockSpec((WINDOW, dk), lambda i, s: (jnp.maximum(i - 1, 0), kblk)),
                      pl.BlockSpec((WINDOW, dk), lambda i, s: (i, kblk)),
                      pl.BlockSpec((WINDOW, dk), lambda i, s: (jnp.maximum(i - 1, 0), kblk + 1)),
                      pl.BlockSpec((WINDOW, dk), lambda i, s: (i, kblk + 1))],
            out_specs=pl.BlockSpec((WINDOW, dq), lambda i, s: (i, 0)),
        ),
        compiler_params=_cparams(("arbitrary",), V7X_VMEM_LIMIT),
        name="swa_prompt",
    )(sinks, qkv, qkv, qkv, qkv, qkv)


def _swa_sample_kernel(q_ref, kn_ref, vn_ref, kc_ref, vc_ref, sink_ref, o_ref, *, n_kv, group, hd, seq):
    n_heads = n_kv * group
    bs = q_ref.shape[0]
    rows = group * seq
    cr = kc_ref.shape[1]
    scale = hd ** -0.5
    r = lax.broadcasted_iota(jnp.int32, (1, rows, cr), 1)
    t = r % seq
    gi = r // seq
    ci = lax.broadcasted_iota(jnp.int32, (1, rows, cr), 2)
    dist_c = (cr + t - ci).astype(F32)
    ok_c = ci > t + (cr - WINDOW)
    r1 = lax.broadcasted_iota(jnp.int32, (1, rows, 1), 1)
    t1 = r1 % seq
    g1 = (r1 // seq).astype(F32)
    for kv in range(n_kv):
        slope_c = jnp.exp2(-8.0 * (kv * group + gi.astype(F32) + 1.0) / n_heads)
        slope_1 = jnp.exp2(-8.0 * (kv * group + g1 + 1.0) / n_heads)
        ks = slice(kv * hd, (kv + 1) * hd)
        q = q_ref[:, kv] * scale
        kc = kc_ref[:, :, ks].astype(BF16)
        vc = vc_ref[:, :, ks].astype(BF16)
        kn = kn_ref[:, kv]
        vn = vn_ref[:, kv]
        sink = sink_ref[kv]
        s_c = jnp.einsum('bqd,bkd->bqk', q.astype(BF16), kc, preferred_element_type=F32)
        s_c = jnp.where(ok_c, s_c - slope_c * dist_c, -jnp.inf)
        mx = jnp.maximum(jnp.max(s_c, axis=-1, keepdims=True), sink[None])
        s_n = []
        for j in range(seq):
            sj = jnp.sum(q * kn[:, j:j + 1, :], axis=-1, keepdims=True)
            sj = jnp.where(t1 >= j, sj - slope_1 * (t1 - j).astype(F32), -jnp.inf)
            s_n.append(sj)
            mx = jnp.maximum(mx, sj)
        e_c = jnp.exp(s_c - mx)
        den = jnp.sum(e_c, axis=-1, keepdims=True) + jnp.exp(sink[None] - mx)
        o = jnp.einsum('bqk,bkd->bqd', e_c.astype(BF16), vc, preferred_element_type=F32)
        for j in range(seq):
            ej = jnp.exp(s_n[j] - mx)
            den = den + ej
            o = o + ej * vn[:, j:j + 1, :]
        o_ref[:, kv] = (o / den).astype(o_ref.dtype)


def _swa_sample(q, k_new, v_new, cache_k, cache_v, sinks, n_kv, hd, seq):
    b, cr = cache_k.shape[0], cache_k.shape[1]
    n_heads = sinks.shape[0]
    group = n_heads // n_kv
    rows = group * seq
    qr = q.reshape(b, seq, n_kv, group, hd).transpose(0, 2, 3, 1, 4).reshape(b, n_kv, rows, hd)
    kn = k_new.reshape(b, seq, n_kv, hd).transpose(0, 2, 1, 3)
    vn = v_new.reshape(b, seq, n_kv, hd).transpose(0, 2, 1, 3)
    kc = cache_k.reshape(b, cr, n_kv * hd)
    vc = cache_v.reshape(b, cr, n_kv * hd)
    sk = jnp.repeat(sinks.reshape(n_kv, group), seq, axis=1).reshape(n_kv, rows, 1)
    bs = _pick(b, (8, 4, 2, 1))
    o = pl.pallas_call(
        functools.partial(_swa_sample_kernel, n_kv=n_kv, group=group, hd=hd, seq=seq),
        out_shape=jax.ShapeDtypeStruct((b, n_kv, rows, hd), BF16),
        grid=(b // bs,),
        in_specs=[pl.BlockSpec((bs, n_kv, rows, hd), lambda i: (i, 0, 0, 0)),
                  pl.BlockSpec((bs, n_kv, seq, hd), lambda i: (i, 0, 0, 0)),
                  pl.BlockSpec((bs, n_kv, seq, hd), lambda i: (i, 0, 0, 0)),
                  pl.BlockSpec((bs, cr, n_kv * hd), lambda i: (i, 0, 0)),
                  pl.BlockSpec((bs, cr, n_kv * hd), lambda i: (i, 0, 0)),
                  pl.BlockSpec((n_kv, rows, 1), lambda i: (0, 0, 0))],
        out_specs=pl.BlockSpec((bs, n_kv, rows, hd), lambda i: (i, 0, 0, 0)),
        compiler_params=_cparams(("arbitrary",), V7X_VMEM_LIMIT),
        name="swa_sample",
    )(qr, kn, vn, kc, vc, sk)
    return o.reshape(b, n_kv, group, seq, hd).transpose(0, 3, 1, 2, 4).reshape(b * seq, n_heads * hd)


def _lower_bound(p_ref):
    p0 = p_ref[0:1, :]
    p1 = p_ref[1:2, :]
    mx = jnp.maximum(p0, p1)
    e0 = jnp.exp(p0 - mx)
    e1 = jnp.exp(p1 - mx)
    return e1 / (e0 + e1)


def _hgrn_gates(fr, lb):
    e = jnp.exp(-jnp.abs(fr))
    log_sig = jnp.minimum(fr, 0.0) - jnp.log1p(e)
    logf = _logaddexp(jnp.log(lb), jnp.log1p(-lb) + log_sig)
    r = 1.0 / (1.0 + e)
    k = (1.0 - lb) * jnp.where(fr >= 0.0, e * r, r)
    return logf, k


def _group_norm_gate(o, gr, ng):
    o = o * lax.rsqrt(jnp.mean(o * o, axis=-1, keepdims=True) + EPS) * ng
    return o * _silu(gr)


SMALL = 8


def _level_tables(c):
    g = min(c, 128)
    t = np.arange(g)[:, None]
    s = np.arange(g)[None, :]
    hb = np.floor(np.log2(np.maximum(t ^ s, 1))).astype(np.int32)
    same = (t // SMALL) == (s // SMALL)
    near = np.where(same & (s < t), hb, np.where(s == t, int(math.log2(SMALL)), -1))
    i = np.arange(c // 2)[:, None]
    j = np.arange(c // 2)[None, :]
    far = np.where(i == j, -1, np.floor(np.log2(np.maximum(i ^ j, 1)))).astype(np.int32)
    return jnp.asarray(near, dtype=jnp.int32), jnp.asarray(far, dtype=jnp.int32)


def _group_ref_rows(p, m, row):
    c, w = p.shape
    if m == 1:
        return jnp.where((row & 1) == 1, pltpu.roll(p, 1, 0), p)
    if m == 2:
        r4 = row & 3
        return jnp.where(r4 == 0, pltpu.roll(p, c - 1, 0),
                         jnp.where(r4 == 1, p,
                                   jnp.where(r4 == 2, pltpu.roll(p, 1, 0), pltpu.roll(p, 2, 0))))
    p3 = p.reshape(c // 8, 8, w)
    return jnp.broadcast_to(p3[:, 3:4, :], (c // 8, 8, w)).reshape(c, w)


def _halves(x, m):
    c = x.shape[0]
    ev = [x[r:r + m] for r in range(0, c, 2 * m)]
    od = [x[r + m:r + 2 * m] for r in range(0, c, 2 * m)]
    cat = lambda xs: xs[0] if len(xs) == 1 else jnp.concatenate(xs, axis=0)
    return cat(ev), cat(od)


def _interleave(ev, od, m):
    parts = []
    for r in range(0, ev.shape[0], m):
        parts += [ev[r:r + m], od[r:r + m]]
    return jnp.concatenate(parts, axis=0)


def _hgrn_chunks(chunks, st, near, far):
    c, w = chunks[0][0].shape
    nlev = int(math.log2(c))
    nsmall = int(math.log2(SMALL))
    g = near.shape[0]
    row = lax.broadcasted_iota(jnp.int32, (c, w), 0)
    nt = (((1,), (1,)), ((), ()))
    dot_nt = lambda x, y: lax.dot_general(x, y, nt, preferred_element_type=F32)
    dot_nn = lambda x, y: jnp.dot(x, y, preferred_element_type=F32)

    near_ops, far_ops, tails = [], [], []
    for q, k, v, logf in chunks:
        p = logf
        qs, ks = [q.astype(BF16)], [k.astype(BF16)]
        for l in range(nsmall):
            gref = _group_ref_rows(p, 1 << l, row)
            odd = ((row >> l) & 1) == 1
            e = jnp.exp(jnp.where(odd, p, gref - p))
            qs.append((q * e).astype(BF16))
            ks.append((k * e).astype(BF16))
            p = p + jnp.where(odd, gref, 0.0)
        near_ops.append((qs, ks, v.astype(BF16)))
        lev = []
        for l in range(nsmall, nlev):
            m = 1 << l
            pe, po = _halves(p, m)
            tot = jnp.concatenate([jnp.broadcast_to(pe[r + m - 1:r + m], (m, w))
                                   for r in range(0, c // 2, m)], axis=0)
            q_od = _halves(q, m)[1]
            k_ev = _halves(k, m)[0]
            v_ev = _halves(v, m)[0].astype(BF16)
            lev.append(((q_od * jnp.exp(po)).astype(BF16), (k_ev * jnp.exp(tot - pe)).astype(BF16), v_ev))
            p = _interleave(pe, po + tot, m)
        far_ops.append(lev)
        b = p
        b_end = b[c - 1:c, :]
        tails.append(((q * jnp.exp(b)).astype(BF16), (k * jnp.exp(b_end - b)).astype(BF16),
                      jnp.exp(b_end)))

    near_sc = [[[dot_nt(qs[j][r:r + g], ks[j][r:r + g]) for j in range(nsmall + 1)]
                for r in range(0, c, g)] for qs, ks, _ in near_ops]
    far_sc = [[dot_nt(qt, kt) for qt, kt, _ in lev] for lev in far_ops]
    upds = [lax.dot_general(vb, k_end, (((0,), (0,)), ((), ())), preferred_element_type=F32)
            for (_, _, vb), (_, k_end, _) in zip(near_ops, tails)]

    near_a = []
    for sc_chunk in near_sc:
        tiles = []
        for sc in sc_chunk:
            a = jnp.where(near == nsmall, sc[0], 0.0)
            for l in range(nsmall):
                a = jnp.where(near == l, sc[l + 1], a)
            tiles.append(a.astype(BF16))
        near_a.append(tiles)
    far_a = [[(jnp.where(far < nsmall + j, a, 0.0) if (1 << (nsmall + j)) < c // 2 else a).astype(BF16)
              for j, a in enumerate(lev)] for lev in far_sc]
    states = [st]
    for (_, _, dec), upd in zip(tails, upds):
        states.append(states[-1] * dec + upd)

    near_o = [[dot_nn(a, vb[i * g:(i + 1) * g]) for i, a in enumerate(tiles)]
              for tiles, (_, _, vb) in zip(near_a, near_ops)]
    far_o = [[dot_nn(a, ops[2]) for a, ops in zip(avs, lev)] for avs, lev in zip(far_a, far_ops)]
    inter_o = [dot_nt(qe, s.astype(BF16)) for (qe, _, _), s in zip(tails, states[:-1])]

    outs = []
    for no, fo, io in zip(near_o, far_o, inter_o):
        o = (no[0] if len(no) == 1 else jnp.concatenate(no, axis=0)) + io
        for j, x in enumerate(fo):
            m = 1 << (nsmall + j)
            o_ev, o_od = _halves(o, m)
            o = _interleave(o_ev, o_od + x, m)
        outs.append(o)
    return outs, states[-1]


def _hgrn_prompt_kernel(a_ref, wq_ref, wf_ref, wi_ref, wg_ref, p_ref, ng_ref, near_ref, far_ref,
                        o_ref, s_ref, wbf, st, *, chunk):
    i = pl.program_id(1)
    dk = wq_ref.shape[1]

    @pl.when(i == 0)
    def _():
        wbf[:, 0 * dk:1 * dk] = wq_ref[...].astype(BF16)
        wbf[:, 1 * dk:2 * dk] = wf_ref[...].astype(BF16)
        wbf[:, 2 * dk:3 * dk] = wi_ref[...].astype(BF16)
        wbf[:, 3 * dk:4 * dk] = wg_ref[...].astype(BF16)
        st[...] = jnp.zeros_like(st)

    tm = a_ref.shape[0]
    projs = [jnp.dot(a_ref[c0:c0 + chunk, :], wbf[...], preferred_element_type=F32)
             for c0 in range(0, tm, chunk)]
    lb = _lower_bound(p_ref)
    ng = ng_ref[...]
    chunks = []
    for proj in projs:
        q = _silu(proj[:, 0 * dk:1 * dk])
        logf, k = _hgrn_gates(proj[:, 1 * dk:2 * dk], lb)
        chunks.append((q, k, proj[:, 2 * dk:3 * dk], logf))
    outs, s_t = _hgrn_chunks(chunks, st[...], near_ref[...], far_ref[...])
    for j, (o, proj) in enumerate(zip(outs, projs)):
        o_ref[j * chunk:(j + 1) * chunk, :] = _group_norm_gate(o, proj[:, 3 * dk:4 * dk], ng).astype(o_ref.dtype)
    st[...] = s_t
    s_ref[0, 0] = s_t.T


def _hgrn_prompt(a, w_in, lower_bounds, norm_g, n_heads, dk):
    m, d = a.shape
    chunk = min(HG_CHUNK, m)
    tm = _pick(m, (512, 256, 128, 64, 32, 16, 8))
    tm = max(tm, chunk)
    near, far = _level_tables(chunk)
    w_spec = lambda blk: pl.BlockSpec((d, dk), lambda h, i: (0, blk * n_heads + h))
    o, s = pl.pallas_call(
        functools.partial(_hgrn_prompt_kernel, chunk=chunk),
        out_shape=(jax.ShapeDtypeStruct((m, n_heads * dk), BF16),
                   jax.ShapeDtypeStruct((1, n_heads, dk, dk), F32)),
        grid=(n_heads, m // tm),
        in_specs=[pl.BlockSpec((tm, d), lambda h, i: (i, 0)),
                  w_spec(0), w_spec(1), w_spec(2), w_spec(3),
                  pl.BlockSpec((lower_bounds.shape[0], dk), lambda h, i: (0, h)),
                  pl.BlockSpec((1, dk), lambda h, i: (0, h)),
                  pl.BlockSpec(near.shape, lambda h, i: (0, 0)),
                  pl.BlockSpec(far.shape, lambda h, i: (0, 0))],
        out_specs=(pl.BlockSpec((tm, dk), lambda h, i: (i, h)),
                   pl.BlockSpec((1, 1, dk, dk), lambda h, i: (0, h, 0, 0))),
        scratch_shapes=[pltpu.VMEM((d, 4 * dk), BF16), pltpu.VMEM((dk, dk), F32)],
        compiler_params=_cparams(("arbitrary", "arbitrary"), V7X_VMEM_LIMIT),
        name="hgrn_prompt",
    )(a, w_in, w_in, w_in, w_in, lower_bounds, norm_g.reshape(1, -1), near, far)
    return o, s


def _hgrn_sample_kernel(qr_ref, fr_ref, ir_ref, gr_ref, p_ref, ng_ref, s0_ref, o_ref, s_ref, *, seq):
    bs = s0_ref.shape[0]
    rows = bs * seq
    lb = _lower_bound(p_ref)
    q = _silu(qr_ref[...])
    logf, k = _hgrn_gates(fr_ref[...], lb)
    v = ir_ref[...]
    row = lax.broadcasted_iota(jnp.int32, q.shape, 0)
    t = row % seq
    b = logf
    d = 1
    while d < seq:
        b = b + jnp.where(t >= d, pltpu.roll(b, d, 0), 0.0)
        d *= 2
    o = jnp.sum(q * k, axis=-1, keepdims=True) * v
    for d in range(1, seq):
        ok = t >= d
        arg = jnp.where(ok, b - pltpu.roll(b, d, 0), 0.0)
        w = jnp.sum(q * pltpu.roll(k, d, 0) * jnp.exp(arg), axis=-1, keepdims=True)
        o = o + jnp.where(ok, w, 0.0) * pltpu.roll(v, d, 0)
    qe = (q * jnp.exp(b)).astype(BF16)
    b_end = b
    d = 1
    while d < seq:
        b_end = jnp.where(t < seq - d, pltpu.roll(b_end, rows - d, 0), b_end)
        d *= 2
    k_end = k * jnp.exp(b_end - b)
    k_end_t = k_end.T
    dec_t = jnp.exp(b_end).T
    lane = lax.broadcasted_iota(jnp.int32, k_end_t.shape, 1)
    vb = v.astype(BF16)
    o_inter = []
    for s in range(bs):
        s0 = s0_ref[s, 0]
        in_seq = (lane >= s * seq) & (lane < (s + 1) * seq)
        upd = jnp.dot(jnp.where(in_seq, k_end_t, 0.0).astype(BF16), vb, preferred_element_type=F32)
        s_ref[s, 0] = s0 * dec_t[:, s * seq:s * seq + 1] + upd
        if s % (8 // seq) == 0:
            r0 = s * seq
            parts = []
        oi = jnp.dot(qe[r0:r0 + 8, :], s0.astype(BF16), preferred_element_type=F32)
        parts.append(oi)
        if s % (8 // seq) == (8 // seq) - 1:
            r8 = lax.broadcasted_iota(jnp.int32, oi.shape, 0) // seq
            acc = parts[0]
            for j in range(1, len(parts)):
                acc = jnp.where(r8 == j, parts[j], acc)
            o_inter.append(acc)
    o = o + jnp.concatenate(o_inter, axis=0)
    o_ref[...] = _group_norm_gate(o, gr_ref[...], ng_ref[...]).astype(o_ref.dtype)


def _hgrn_sample(proj, lower_bounds, norm_g, state, seq):
    b, n_heads, dk, dv = state.shape
    m = proj.shape[0]
    bs = _pick(b, (32, 16, 8, 4, 2))
    rows = bs * seq
    col = lambda blk: pl.BlockSpec((rows, dk), lambda h, i: (i, blk * n_heads + h))
    o, s = pl.pallas_call(
        functools.partial(_hgrn_sample_kernel, seq=seq),
        out_shape=(jax.ShapeDtypeStruct((m, n_heads * dv), BF16),
                   jax.ShapeDtypeStruct(state.shape, F32)),
        grid=(n_heads, b // bs),
        in_specs=[col(0), col(1), col(2), col(3),
                  pl.BlockSpec((lower_bounds.shape[0], dk), lambda h, i: (0, h)),
                  pl.BlockSpec((1, dv), lambda h, i: (0, h)),
                  pl.BlockSpec((bs, 1, dk, dv), lambda h, i: (i, h, 0, 0))],
        out_specs=(pl.BlockSpec((rows, dv), lambda h, i: (i, h)),
                   pl.BlockSpec((bs, 1, dk, dv), lambda h, i: (i, h, 0, 0))),
        compiler_params=_cparams(("arbitrary", "arbitrary"), V7X_VMEM_LIMIT),
        name="hgrn_sample",
    )(proj, proj, proj, proj, lower_bounds, norm_g.reshape(1, -1), state)
    return o, s


def kernel(x_prompt, x_sample, cache_swa_k, cache_swa_v, state_hgrn, state_ffn_conv, c_prompt, c_sample,
           norm1_g, norm2_g, w_ada, b_ada, attn_w_qkv, attn_w_o, attn_sinks,
           hgrn_w_in, hgrn_lower_bounds, hgrn_norm_g, hgrn_w_o,
           ffn_w_in, ffn_conv_w, ffn_conv_b, ffn_w_out, final_norm_g):
    bp, lp, d = x_prompt.shape
    bsmp, ls, _ = x_sample.shape
    assert bp == 1, "the prompt kernels carry one sequence"
    depth = w_ada.shape[0]
    n_kv, hd = cache_swa_k.shape[2], cache_swa_k.shape[3]
    n_heads = attn_sinks.shape[0]
    dq = n_heads * hd
    dkv = n_kv * hd
    hg_heads, hg_dk = state_hgrn.shape[1], state_hgrn.shape[2]
    keep = min(WINDOW, lp)

    ns = bsmp * ls
    assert ns % 8 == 0
    c_all = jnp.concatenate([jnp.repeat(c_sample, ls, axis=0), c_prompt, jnp.zeros((7, d), F32)], axis=0)
    mod = _ada(c_all, w_ada, b_ada)
    pk = dict(mod=mod, sample=False, n_sample=ns)
    sk = dict(mod=mod, sample=True, n_sample=ns)

    xp = x_prompt.reshape(lp, d)
    xs = x_sample.reshape(ns, d)
    conv_p, conv_s = [], []
    outs = {}
    for l in range(depth):
        hp = _modulate(xp, norm1_g, mod, l, 0, 1, False, ns)
        hs = _modulate(xs, norm1_g, mod, l, 0, 1, True, ns)
        if l % 2 == 0:
            qkv_p = _matmul(hp, attn_w_qkv, name="qkv_prompt")
            qkv_s = _matmul(hs, attn_w_qkv, name="qkv_sample")
            op = _swa_prompt(qkv_p, attn_sinks, n_kv, hd)
            os_ = _swa_sample(qkv_s[:, :dq], qkv_s[:, dq:dq + dkv], qkv_s[:, dq + dkv:],
                              cache_swa_k, cache_swa_v, attn_sinks, n_kv, hd, ls)
            outs['swa_k_prompt'] = qkv_p[lp - keep:, dq:dq + dkv].reshape(bp, keep, n_kv, hd)
            outs['swa_v_prompt'] = qkv_p[lp - keep:, dq + dkv:].reshape(bp, keep, n_kv, hd)
            outs['swa_k_sample'] = qkv_s[:, dq:dq + dkv].reshape(bsmp, ls, n_kv, hd)
            outs['swa_v_sample'] = qkv_s[:, dq + dkv:].reshape(bsmp, ls, n_kv, hd)
            w_o = attn_w_o
        else:
            op, sp = _hgrn_prompt(hp, hgrn_w_in, hgrn_lower_bounds, hgrn_norm_g, hg_heads, hg_dk)
            proj_s = _matmul(hs, hgrn_w_in, name="hgrn_in_sample")
            os_, ss = _hgrn_sample(proj_s, hgrn_lower_bounds, hgrn_norm_g, state_hgrn, ls)
            outs['hgrn_state_prompt'] = sp
            outs['hgrn_state_sample'] = ss
            w_o = hgrn_w_o
        xp = _matmul(op, w_o, x=xp, mod_layer=l, gate_col=2, name="mix_out_prompt", **pk)
        xs = _matmul(os_, w_o, x=xs, mod_layer=l, gate_col=2, name="mix_out_sample", **sk)
        hp = _modulate(xp, norm2_g, mod, l, 3, 4, False, ns)
        hs = _modulate(xs, norm2_g, mod, l, 3, 4, True, ns)
        actp, bufp = _ffn_in_prompt(hp, ffn_w_in, ffn_conv_w, ffn_conv_b, l)
        acts, bufs = _ffn_in_sample(hs, ffn_w_in, ffn_conv_w, ffn_conv_b, l, state_ffn_conv[l], ls)
        xp = _matmul(actp, ffn_w_out, w_layer=l, x=xp, mod_layer=l, gate_col=5, tm=512, tn=512,
                     name="ffn_out_prompt", **pk)
        xs = _matmul(acts, ffn_w_out, w_layer=l, x=xs, mod_layer=l, gate_col=5, tm=512, tn=512,
                     name="ffn_out_sample", **sk)
        conv_p.append(bufp[None])
        conv_s.append(bufs)
    y_prompt = _rmsnorm(xp, final_norm_g).reshape(bp, lp, d)
    y_sample = _rmsnorm(xs, final_norm_g).reshape(bsmp, ls, d)
    return (y_prompt, y_sample,
            outs['swa_k_prompt'], outs['swa_v_prompt'], outs['swa_k_sample'], outs['swa_v_sample'],
            outs['hgrn_state_prompt'], outs['hgrn_state_sample'],
            jnp.stack(conv_p), jnp.stack(conv_s))
```

```python
import functools
import math

import numpy as np
import jax
import jax.numpy as jnp
from jax import lax
from jax.experimental import pallas as pl
from jax.experimental.pallas import tpu as pltpu

F32 = jnp.float32
BF16 = jnp.bfloat16

EPS = 1e-6
LOG2E = math.log2(math.e)
WINDOW = 128
CONV_W = 3
HG_CHUNK = 256
HG_ROWS = 1024
SWA_WAVE = 8
V7X_VMEM_LIMIT = 56 * 1024 * 1024


def _cparams(sem, vmem=None):
    return pltpu.CompilerParams(dimension_semantics=sem, vmem_limit_bytes=vmem)


def _pick(n, cands):
    for c in cands:
        if n % c == 0:
            return c
    return n


def _silu(x):
    return x * jax.nn.sigmoid(x)


def _logaddexp(a, b):
    return jnp.maximum(a, b) + jnp.log1p(jnp.exp(-jnp.abs(a - b)))


def _ada_kernel(c_ref, w_ref, b_ref, o_ref):
    a = _silu(c_ref[...]).astype(BF16)
    w = w_ref[0].astype(BF16)
    o_ref[0] = jnp.dot(a, w, preferred_element_type=F32) + b_ref[0]


def _ada(c_all, w_ada, b_ada):
    depth, d, n = w_ada.shape
    r = c_all.shape[0]
    tn = _pick(n, (1024, 512, 256, 128))
    return pl.pallas_call(
        _ada_kernel,
        out_shape=jax.ShapeDtypeStruct((depth, r, n), F32),
        grid=(depth, n // tn),
        in_specs=[pl.BlockSpec((r, d), lambda l, j: (0, 0)),
                  pl.BlockSpec((1, d, tn), lambda l, j: (l, 0, j)),
                  pl.BlockSpec((1, 1, tn), lambda l, j: (l, 0, j))],
        out_specs=pl.BlockSpec((1, r, tn), lambda l, j: (l, 0, j)),
        compiler_params=_cparams(("arbitrary", "arbitrary"), V7X_VMEM_LIMIT),
        name="ada",
    )(c_all, w_ada, b_ada.reshape(depth, 1, n))


def _mod_rows(sample, tm, n_sample):
    if sample:
        return tm, (lambda i: i)
    return 8, (lambda i: n_sample // 8)


def _modulate_kernel(x_ref, g_ref, sc_ref, sh_ref, o_ref, *, sample):
    x = x_ref[...]
    ms = jnp.mean(x * x, axis=-1, keepdims=True)
    y = x * lax.rsqrt(ms + EPS) * g_ref[...]
    sc = sc_ref[...] if sample else sc_ref[0:1, :]
    sh = sh_ref[...] if sample else sh_ref[0:1, :]
    o_ref[...] = (y * (1.0 + sc) + sh).astype(o_ref.dtype)


def _modulate(x, g_all, mod, l, shift_col, scale_col, sample, n_sample):
    m, d = x.shape
    tm = _pick(m, (512, 256, 128, 64, 32, 16, 8))
    rows, rmap = _mod_rows(sample, tm, n_sample)
    mspec = lambda col: pl.BlockSpec((None, rows, d), lambda i: (l, rmap(i), col))
    return pl.pallas_call(
        functools.partial(_modulate_kernel, sample=sample),
        out_shape=jax.ShapeDtypeStruct((m, d), BF16),
        grid=(m // tm,),
        in_specs=[pl.BlockSpec((tm, d), lambda i: (i, 0)),
                  pl.BlockSpec((None, 1, d), lambda i: (l, 0, 0)),
                  mspec(scale_col), mspec(shift_col)],
        out_specs=pl.BlockSpec((tm, d), lambda i: (i, 0)),
        compiler_params=_cparams(("arbitrary",), V7X_VMEM_LIMIT),
        name="modulate",
    )(x, g_all.reshape(g_all.shape[0], 1, d), mod, mod)


def _mm_kernel(a_ref, w_ref, o_ref, wbf):
    @pl.when(pl.program_id(1) == 0)
    def _():
        wbf[...] = w_ref[...].astype(BF16)

    o_ref[...] = jnp.dot(a_ref[...], wbf[...], preferred_element_type=F32)


def _matmul(a, w, *, tm=None, tn=None, name="matmul"):
    m, k = a.shape
    n = w.shape[-1]
    tm = min(tm, m) if tm else _pick(m, (1024, 512, 256, 128, 64, 32, 16, 8))
    tn = min(tn, n) if tn else _pick(n, (512, 256, 128))
    assert m % tm == 0 and n % tn == 0
    return pl.pallas_call(
        _mm_kernel,
        out_shape=jax.ShapeDtypeStruct((m, n), F32),
        grid=(n // tn, m // tm),
        in_specs=[pl.BlockSpec((tm, k), lambda j, i: (i, 0)),
                  pl.BlockSpec((k, tn), lambda j, i: (0, j))],
        out_specs=pl.BlockSpec((tm, tn), lambda j, i: (i, j)),
        scratch_shapes=[pltpu.VMEM((k, tn), BF16)],
        compiler_params=_cparams(("arbitrary", "arbitrary"), V7X_VMEM_LIMIT),
        name=name,
    )(a, w)


def _mm_rows_kernel(*refs, cast_w, sample, modulate, keep_x):
    it = iter(refs)
    a_ref, w_ref, x_ref, gate_ref, g_ref = (next(it) for _ in range(5))
    sc_ref, sh_ref = (next(it), next(it)) if modulate else (None, None)
    xo_ref = next(it) if keep_x else None
    h_ref = next(it)
    if cast_w:
        wbf = next(it)

        @pl.when(pl.program_id(0) == 0)
        def _():
            wbf[...] = w_ref[...].astype(BF16)

        w = wbf[...]
    else:
        w = w_ref[...]
    rows = (lambda r: r[...]) if sample else (lambda r: r[0:1, :])
    x = x_ref[...] + rows(gate_ref) * jnp.dot(a_ref[...], w, preferred_element_type=F32)
    if keep_x:
        xo_ref[...] = x
    y = x * lax.rsqrt(jnp.mean(x * x, axis=-1, keepdims=True) + EPS) * g_ref[...]
    if modulate:
        y = y * (1.0 + rows(sc_ref)) + rows(sh_ref)
    h_ref[...] = y.astype(h_ref.dtype)


def _mm_rows(a, w, x, mod, *, gate, norm_g, norm_layer, next_mod=None, w_layer=None, sample, n_sample,
             tm, keep_x=True, out_dtype=None, name):
    m, k = a.shape
    n = w.shape[-1]
    tm = min(tm, m)
    assert m % tm == 0
    cast_w = w.dtype != BF16
    rows, rmap = _mod_rows(sample, tm, n_sample)
    once = pl.Buffered(1)
    if w_layer is None:
        w_spec = pl.BlockSpec((k, n), lambda i: (0, 0), pipeline_mode=once)
    else:
        w_spec = pl.BlockSpec((None, k, n), lambda i: (w_layer, 0, 0), pipeline_mode=once)
    mspec = lambda layer, col: pl.BlockSpec((None, rows, n), lambda i: (layer, rmap(i), col))
    in_specs = [pl.BlockSpec((tm, k), lambda i: (i, 0)), w_spec,
                pl.BlockSpec((tm, n), lambda i: (i, 0)),
                mspec(*gate),
                pl.BlockSpec((None, 1, n), lambda i: (norm_layer, 0, 0))]
    args = [a, w, x, mod, norm_g.reshape(-1, 1, n)]
    if next_mod is not None:
        in_specs += [mspec(next_mod[0], next_mod[2]), mspec(next_mod[0], next_mod[1])]
        args += [mod, mod]
    out_shape, out_specs = [], []
    if keep_x:
        out_shape.append(jax.ShapeDtypeStruct((m, n), F32))
        out_specs.append(pl.BlockSpec((tm, n), lambda i: (i, 0)))
    out_shape.append(jax.ShapeDtypeStruct((m, n), out_dtype or BF16))
    out_specs.append(pl.BlockSpec((tm, n), lambda i: (i, 0)))
    res = pl.pallas_call(
        functools.partial(_mm_rows_kernel, cast_w=cast_w, sample=sample, modulate=next_mod is not None,
                          keep_x=keep_x),
        out_shape=tuple(out_shape),
        grid=(m // tm,),
        in_specs=in_specs,
        out_specs=tuple(out_specs),
        scratch_shapes=[pltpu.VMEM((k, n), BF16)] if cast_w else [],
        compiler_params=_cparams(("arbitrary",), V7X_VMEM_LIMIT),
        name=name,
    )(*args)
    return res if keep_x else (None, res[0])


def _gelu_exact(a):
    return 0.5 * a * (1.0 + lax.erf(a * (1.0 / math.sqrt(2.0))))


def _ffn_in_prompt_kernel(a_ref, wu_ref, wg_ref, cw_ref, cb_ref, act_ref, tail_ref, wbf, carry):
    i = pl.program_id(1)
    tf = wu_ref.shape[1]

    @pl.when(i == 0)
    def _():
        wbf[:, :tf] = wu_ref[...].astype(BF16)
        wbf[:, tf:] = wg_ref[...].astype(BF16)
        carry[...] = jnp.zeros_like(carry)

    ug = jnp.dot(a_ref[...], wbf[...], preferred_element_type=F32)
    u = ug[:, :tf]
    g = ug[:, tf:]
    tm = u.shape[0]
    row = lax.broadcasted_iota(jnp.int32, u.shape, 0)
    c = carry[...]
    u1 = jnp.where(row == 0, c[7:8, :], pltpu.roll(u, 1, 0))
    u2 = jnp.where(row == 0, c[6:7, :], jnp.where(row == 1, c[7:8, :], pltpu.roll(u, 2, 0)))
    cw = cw_ref[...]
    conv = cw[0:1, :] * u2 + cw[1:2, :] * u1 + cw[2:3, :] * u + cb_ref[...]
    act_ref[...] = (_gelu_exact(conv) * g).astype(act_ref.dtype)
    carry[...] = u[tm - 8:, :]
    tail_ref[...] = u[tm - 8:, :]


def _ffn_in_prompt(a, w_in, conv_w, conv_b, l):
    m, d = a.shape
    f = w_in.shape[-1] // 2
    tf = _pick(f, (512, 256, 128))
    tm = _pick(m, (1024, 512, 256, 128, 64, 32, 16, 8))
    nf = f // tf
    act, tail = pl.pallas_call(
        _ffn_in_prompt_kernel,
        out_shape=(jax.ShapeDtypeStruct((m, f), BF16), jax.ShapeDtypeStruct((8, f), F32)),
        grid=(nf, m // tm),
        in_specs=[pl.BlockSpec((tm, d), lambda j, i: (i, 0)),
                  pl.BlockSpec((None, d, tf), lambda j, i: (l, 0, j)),
                  pl.BlockSpec((None, d, tf), lambda j, i: (l, 0, nf + j)),
                  pl.BlockSpec((None, CONV_W, tf), lambda j, i: (l, 0, j)),
                  pl.BlockSpec((None, 1, tf), lambda j, i: (l, 0, j))],
        out_specs=(pl.BlockSpec((tm, tf), lambda j, i: (i, j)),
                   pl.BlockSpec((8, tf), lambda j, i: (0, j))),
        scratch_shapes=[pltpu.VMEM((d, 2 * tf), BF16), pltpu.VMEM((8, tf), F32)],
        compiler_params=_cparams(("arbitrary", "arbitrary"), V7X_VMEM_LIMIT),
        name="ffn_in_prompt",
    )(a, w_in, w_in, conv_w, conv_b.reshape(conv_b.shape[0], 1, f))
    return act, tail[6:8]


def _ffn_in_sample_kernel(a_ref, wu_ref, wg_ref, cw_ref, cb_ref, p1_ref, p2_ref, act_ref, u_ref, *, seq):
    tf = wu_ref.shape[1]
    a = a_ref[...]
    u = jnp.dot(a, wu_ref[...].astype(BF16), preferred_element_type=F32)
    g = jnp.dot(a, wg_ref[...].astype(BF16), preferred_element_type=F32)
    row = lax.broadcasted_iota(jnp.int32, u.shape, 0) % seq
    u1 = jnp.where(row == 0, p1_ref[...], pltpu.roll(u, 1, 0))
    u2 = jnp.where(row < 2, p2_ref[...], pltpu.roll(u, 2, 0))
    cw = cw_ref[...]
    conv = cw[0:1, :] * u2 + cw[1:2, :] * u1 + cw[2:3, :] * u + cb_ref[...]
    act_ref[...] = (_gelu_exact(conv) * g).astype(act_ref.dtype)
    u_ref[...] = u


def _ffn_in_sample(a, w_in, conv_w, conv_b, l, buf, seq):
    m, d = a.shape
    f = w_in.shape[-1] // 2
    b = buf.shape[0]
    tf = _pick(f, (512, 256, 128))
    nf = f // tf
    zeros = jnp.zeros((b, seq - 1, f), F32)
    p1 = jnp.concatenate([buf[:, 1:2], zeros], axis=1).reshape(m, f)
    p2 = jnp.concatenate([buf, zeros[:, :seq - 2]], axis=1).reshape(m, f)
    act, u = pl.pallas_call(
        functools.partial(_ffn_in_sample_kernel, seq=seq),
        out_shape=(jax.ShapeDtypeStruct((m, f), BF16), jax.ShapeDtypeStruct((m, f), F32)),
        grid=(nf,),
        in_specs=[pl.BlockSpec((m, d), lambda j: (0, 0)),
                  pl.BlockSpec((None, d, tf), lambda j: (l, 0, j)),
                  pl.BlockSpec((None, d, tf), lambda j: (l, 0, nf + j)),
                  pl.BlockSpec((None, CONV_W, tf), lambda j: (l, 0, j)),
                  pl.BlockSpec((None, 1, tf), lambda j: (l, 0, j)),
                  pl.BlockSpec((m, tf), lambda j: (0, j)),
                  pl.BlockSpec((m, tf), lambda j: (0, j))],
        out_specs=(pl.BlockSpec((m, tf), lambda j: (0, j)),
                   pl.BlockSpec((m, tf), lambda j: (0, j))),
        compiler_params=_cparams(("arbitrary",), V7X_VMEM_LIMIT),
        name="ffn_in_sample",
    )(a, w_in, w_in, conv_w, conv_b.reshape(conv_b.shape[0], 1, f), p1, p2)
    new_buf = u.reshape(b, seq, f)[:, seq - (CONV_W - 1):]
    return act, new_buf


def _alibi_slope(h, n_heads):
    return 2.0 ** (-8.0 * (h + 1) / n_heads)


def _swa_prompt_kernel(sink_ref, q_ref, kp_ref, kc_ref, vp_ref, vc_ref, o_ref, *, n_kv, group, hd):
    i = pl.program_id(0)
    n_heads = n_kv * group
    tq = q_ref.shape[0]
    t = lax.broadcasted_iota(jnp.int32, (tq, 2 * tq), 0)
    s = lax.broadcasted_iota(jnp.int32, (tq, 2 * tq), 1)
    dist = t + tq - s
    allowed = (dist >= 0) & (dist < WINDOW) & ((s >= tq) | (i > 0))
    sidx = s[0:1, :].astype(F32)
    trow = (t[:, 0:1] + tq).astype(F32)
    qscale = hd ** -0.5 * LOG2E
    nt = (((1,), (1,)), ((), ()))
    def scores(kv):
        ks = slice(kv * hd, (kv + 1) * hd)
        k2 = jnp.concatenate([kp_ref[:, ks], kc_ref[:, ks]], axis=0).astype(BF16)
        q4 = jnp.concatenate([(q_ref[:, h * hd:(h + 1) * hd] * qscale).astype(BF16)
                              for h in range(kv * group, (kv + 1) * group)], axis=0)
        return lax.dot_general(q4, k2, nt, preferred_element_type=F32)

    for kv0 in range(0, n_kv, SWA_WAVE):
        kvs = list(range(kv0, min(kv0 + SWA_WAVE, n_kv)))
        sc4s = [scores(kv) for kv in kvs]
        heads = [kv * group + g for kv in kvs for g in range(group)]
        slopes = [_alibi_slope(h, n_heads) * LOG2E for h in heads]
        scs = [jnp.where(allowed, sc4s[j // group][(j % group) * tq:(j % group + 1) * tq] + slopes[j] * sidx,
                         -jnp.inf) for j in range(len(heads))]
        sinks = [sink_ref[h] * LOG2E + slopes[j] * trow for j, h in enumerate(heads)]
        mxs = [jnp.maximum(jnp.max(scs[j], axis=-1, keepdims=True), sinks[j]) for j in range(len(heads))]
        es = [jnp.exp2(scs[j] - mxs[j]) for j in range(len(heads))]
        dens = [jnp.sum(es[j], axis=-1, keepdims=True) + jnp.exp2(sinks[j] - mxs[j])
                for j in range(len(heads))]
        es = [e.astype(BF16) for e in es]
        o4s = []
        for i, kv in enumerate(kvs):
            ks = slice(kv * hd, (kv + 1) * hd)
            v2 = jnp.concatenate([vp_ref[:, ks], vc_ref[:, ks]], axis=0).astype(BF16)
            o4s.append(jnp.dot(jnp.concatenate(es[i * group:(i + 1) * group], axis=0), v2,
                               preferred_element_type=F32))
        for j, h in enumerate(heads):
            o = o4s[j // group][(j % group) * tq:(j % group + 1) * tq]
            o_ref[:, h * hd:(h + 1) * hd] = (o / dens[j]).astype(o_ref.dtype)


def _swa_prompt(qkv, sinks, n_kv, hd):
    l = qkv.shape[0]
    n_heads = sinks.shape[0]
    group = n_heads // n_kv
    dq = n_heads * hd
    dk = n_kv * hd
    nb = l // WINDOW
    kblk = dq // dk
    return pl.pallas_call(
        functools.partial(_swa_prompt_kernel, n_kv=n_kv, group=group, hd=hd),
        out_shape=jax.ShapeDtypeStruct((l, dq), BF16),
        grid_spec=pltpu.PrefetchScalarGridSpec(
            num_scalar_prefetch=1,
            grid=(nb,),
            in_specs=[pl.BlockSpec((WINDOW, dq), lambda i, s: (i, 0)),
                      pl.BlockSpec((WINDOW, dk), lambda i, s: (jnp.maximum(i - 1, 0), kblk)),
                      pl.BlockSpec((WINDOW, dk), lambda i, s: (i, kblk)),
                      pl.BlockSpec((WINDOW, dk), lambda i, s: (jnp.maximum(i - 1, 0), kblk + 1)),
                      pl.BlockSpec((WINDOW, dk), lambda i, s: (i, kblk + 1))],
            out_specs=pl.BlockSpec((WINDOW, dq), lambda i, s: (i, 0)),
        ),
        compiler_params=_cparams(("arbitrary",), V7X_VMEM_LIMIT),
        name="swa_prompt",
    )(sinks, qkv, qkv, qkv, qkv, qkv)


def _swa_sample_kernel(q_ref, kn_ref, vn_ref, kc_ref, vc_ref, sink_ref, o_ref, *, n_kv, group, hd, seq):
    n_heads = n_kv * group
    bs = q_ref.shape[0]
    rows = group * seq
    cr = kc_ref.shape[1]
    scale = hd ** -0.5
    r = lax.broadcasted_iota(jnp.int32, (1, rows, cr), 1)
    t = r % seq
    gi = r // seq
    ci = lax.broadcasted_iota(jnp.int32, (1, rows, cr), 2)
    dist_c = (cr + t - ci).astype(F32)
    ok_c = ci > t + (cr - WINDOW)
    r1 = lax.broadcasted_iota(jnp.int32, (1, rows, 1), 1)
    t1 = r1 % seq
    g1 = (r1 // seq).astype(F32)
    for kv in range(n_kv):
        slope_c = jnp.exp2(-8.0 * (kv * group + gi.astype(F32) + 1.0) / n_heads)
        slope_1 = jnp.exp2(-8.0 * (kv * group + g1 + 1.0) / n_heads)
        ks = slice(kv * hd, (kv + 1) * hd)
        q = q_ref[:, kv] * scale
        kc = kc_ref[:, :, ks].astype(BF16)
        vc = vc_ref[:, :, ks].astype(BF16)
        kn = kn_ref[:, kv]
        vn = vn_ref[:, kv]
        sink = sink_ref[kv]
        s_c = jnp.einsum('bqd,bkd->bqk', q.astype(BF16), kc, preferred_element_type=F32)
        s_c = jnp.where(ok_c, s_c - slope_c * dist_c, -jnp.inf)
        mx = jnp.maximum(jnp.max(s_c, axis=-1, keepdims=True), sink[None])
        s_n = []
        for j in range(seq):
            sj = jnp.sum(q * kn[:, j:j + 1, :], axis=-1, keepdims=True)
            sj = jnp.where(t1 >= j, sj - slope_1 * (t1 - j).astype(F32), -jnp.inf)
            s_n.append(sj)
            mx = jnp.maximum(mx, sj)
        e_c = jnp.exp(s_c - mx)
        den = jnp.sum(e_c, axis=-1, keepdims=True) + jnp.exp(sink[None] - mx)
        o = jnp.einsum('bqk,bkd->bqd', e_c.astype(BF16), vc, preferred_element_type=F32)
        for j in range(seq):
            ej = jnp.exp(s_n[j] - mx)
            den = den + ej
            o = o + ej * vn[:, j:j + 1, :]
        o_ref[:, kv] = (o / den).astype(o_ref.dtype)


def _swa_sample(q, k_new, v_new, cache_k, cache_v, sinks, n_kv, hd, seq):
    b, cr = cache_k.shape[0], cache_k.shape[1]
    n_heads = sinks.shape[0]
    group = n_heads // n_kv
    rows = group * seq
    qr = q.reshape(b, seq, n_kv, group, hd).transpose(0, 2, 3, 1, 4).reshape(b, n_kv, rows, hd)
    kn = k_new.reshape(b, seq, n_kv, hd).transpose(0, 2, 1, 3)
    vn = v_new.reshape(b, seq, n_kv, hd).transpose(0, 2, 1, 3)
    sk = jnp.repeat(sinks.reshape(n_kv, group), seq, axis=1).reshape(n_kv, rows, 1)
    bs = _pick(b, (8, 4, 2, 1))
    o = pl.pallas_call(
        functools.partial(_swa_sample_kernel, n_kv=n_kv, group=group, hd=hd, seq=seq),
        out_shape=jax.ShapeDtypeStruct((b, n_kv, rows, hd), BF16),
        grid=(b // bs,),
        in_specs=[pl.BlockSpec((bs, n_kv, rows, hd), lambda i: (i, 0, 0, 0)),
                  pl.BlockSpec((bs, n_kv, seq, hd), lambda i: (i, 0, 0, 0)),
                  pl.BlockSpec((bs, n_kv, seq, hd), lambda i: (i, 0, 0, 0)),
                  pl.BlockSpec((bs, cr, n_kv * hd), lambda i: (i, 0, 0)),
                  pl.BlockSpec((bs, cr, n_kv * hd), lambda i: (i, 0, 0)),
                  pl.BlockSpec((n_kv, rows, 1), lambda i: (0, 0, 0))],
        out_specs=pl.BlockSpec((bs, n_kv, rows, hd), lambda i: (i, 0, 0, 0)),
        compiler_params=_cparams(("arbitrary",), V7X_VMEM_LIMIT),
        name="swa_sample",
    )(qr, kn, vn, cache_k.reshape(b, cr, n_kv * hd), cache_v.reshape(b, cr, n_kv * hd), sk)
    return o.reshape(b, n_kv, group, seq, hd).transpose(0, 3, 1, 2, 4).reshape(b * seq, n_heads * hd)


def _lower_bound(p_ref):
    p0 = p_ref[0:1, :]
    p1 = p_ref[1:2, :]
    mx = jnp.maximum(p0, p1)
    e0 = jnp.exp(p0 - mx)
    e1 = jnp.exp(p1 - mx)
    return e1 / (e0 + e1)


def _hgrn_gates(fr, lb):
    e = jnp.exp(-jnp.abs(fr))
    log_sig = jnp.minimum(fr, 0.0) - jnp.log1p(e)
    logf = _logaddexp(jnp.log(lb), jnp.log1p(-lb) + log_sig)
    r = 1.0 / (1.0 + e)
    k = (1.0 - lb) * jnp.where(fr >= 0.0, e * r, r)
    return logf, k


def _group_norm_gate(o, gr, ng):
    o = o * lax.rsqrt(jnp.mean(o * o, axis=-1, keepdims=True) + EPS) * ng
    return o * _silu(gr)


SMALL = 8


def _level_tables(c):
    g = min(c, 128)
    t = np.arange(g)[:, None]
    s = np.arange(g)[None, :]
    hb = np.floor(np.log2(np.maximum(t ^ s, 1))).astype(np.int32)
    same = (t // SMALL) == (s // SMALL)
    near = np.where(same & (s < t), hb, np.where(s == t, int(math.log2(SMALL)), -1))
    i = np.arange(c // 2)[:, None]
    j = np.arange(c // 2)[None, :]
    far = np.where(i == j, -1, np.floor(np.log2(np.maximum(i ^ j, 1)))).astype(np.int32)
    return jnp.asarray(near, dtype=jnp.int32), jnp.asarray(far, dtype=jnp.int32)


def _group_ref_rows(p, m, row):
    c, w = p.shape
    if m == 1:
        return jnp.where((row & 1) == 1, pltpu.roll(p, 1, 0), p)
    if m == 2:
        r4 = row & 3
        return jnp.where(r4 == 0, pltpu.roll(p, c - 1, 0),
                         jnp.where(r4 == 1, p,
                                   jnp.where(r4 == 2, pltpu.roll(p, 1, 0), pltpu.roll(p, 2, 0))))
    p3 = p.reshape(c // 8, 8, w)
    return jnp.broadcast_to(p3[:, 3:4, :], (c // 8, 8, w)).reshape(c, w)


def _halves(x, m):
    c = x.shape[0]
    ev = [x[r:r + m] for r in range(0, c, 2 * m)]
    od = [x[r + m:r + 2 * m] for r in range(0, c, 2 * m)]
    cat = lambda xs: xs[0] if len(xs) == 1 else jnp.concatenate(xs, axis=0)
    return cat(ev), cat(od)


def _interleave(ev, od, m):
    parts = []
    for r in range(0, ev.shape[0], m):
        parts += [ev[r:r + m], od[r:r + m]]
    return jnp.concatenate(parts, axis=0)


def _hgrn_chunks(chunks, st, near, far):
    c, w = chunks[0][0].shape
    nlev = int(math.log2(c))
    nsmall = int(math.log2(SMALL))
    g = near.shape[0]
    row = lax.broadcasted_iota(jnp.int32, (c, w), 0)
    nt = (((1,), (1,)), ((), ()))
    dot_nt = lambda x, y: lax.dot_general(x, y, nt, preferred_element_type=F32)
    dot_nn = lambda x, y: jnp.dot(x, y, preferred_element_type=F32)

    near_ops, far_ops, tails = [], [], []
    for q, k, v, logf in chunks:
        p = logf
        qs, ks = [q.astype(BF16)], [k.astype(BF16)]
        for l in range(nsmall):
            gref = _group_ref_rows(p, 1 << l, row)
            odd = ((row >> l) & 1) == 1
            e = jnp.exp(jnp.where(odd, p, gref - p))
            qs.append((q * e).astype(BF16))
            ks.append((k * e).astype(BF16))
            p = p + jnp.where(odd, gref, 0.0)
        near_ops.append((qs, ks, v.astype(BF16)))
        lev = []
        for l in range(nsmall, nlev):
            m = 1 << l
            pe, po = _halves(p, m)
            tot = jnp.concatenate([jnp.broadcast_to(pe[r + m - 1:r + m], (m, w))
                                   for r in range(0, c // 2, m)], axis=0)
            q_od = _halves(q, m)[1]
            k_ev = _halves(k, m)[0]
            v_ev = _halves(v, m)[0].astype(BF16)
            lev.append(((q_od * jnp.exp(po)).astype(BF16), (k_ev * jnp.exp(tot - pe)).astype(BF16), v_ev))
            p = _interleave(pe, po + tot, m)
        far_ops.append(lev)
        b = p
        b_end = b[c - 1:c, :]
        tails.append(((q * jnp.exp(b)).astype(BF16), (k * jnp.exp(b_end - b)).astype(BF16),
                      jnp.exp(b_end)))

    near_sc = [[[dot_nt(qs[j][r:r + g], ks[j][r:r + g]) for j in range(nsmall + 1)]
                for r in range(0, c, g)] for qs, ks, _ in near_ops]
    far_sc = [[dot_nt(qt, kt) for qt, kt, _ in lev] for lev in far_ops]
    upds = [lax.dot_general(vb, k_end, (((0,), (0,)), ((), ())), preferred_element_type=F32)
            for (_, _, vb), (_, k_end, _) in zip(near_ops, tails)]

    near_a = []
    for sc_chunk in near_sc:
        tiles = []
        for sc in sc_chunk:
            a = jnp.where(near == nsmall, sc[0], 0.0)
            for l in range(nsmall):
                a = jnp.where(near == l, sc[l + 1], a)
            tiles.append(a.astype(BF16))
        near_a.append(tiles)
    far_a = [[(jnp.where(far < nsmall + j, a, 0.0) if (1 << (nsmall + j)) < c // 2 else a).astype(BF16)
              for j, a in enumerate(lev)] for lev in far_sc]
    states = [st]
    for (_, _, dec), upd in zip(tails, upds):
        states.append(states[-1] * dec + upd)

    near_o = [[dot_nn(a, vb[i * g:(i + 1) * g]) for i, a in enumerate(tiles)]
              for tiles, (_, _, vb) in zip(near_a, near_ops)]
    far_o = [[dot_nn(a, ops[2]) for a, ops in zip(avs, lev)] for avs, lev in zip(far_a, far_ops)]
    inter_o = [dot_nt(qe, s.astype(BF16)) for (qe, _, _), s in zip(tails, states[:-1])]

    outs = []
    for no, fo, io in zip(near_o, far_o, inter_o):
        o = (no[0] if len(no) == 1 else jnp.concatenate(no, axis=0)) + io
        for j, x in enumerate(fo):
            m = 1 << (nsmall + j)
            o_ev, o_od = _halves(o, m)
            o = _interleave(o_ev, o_od + x, m)
        outs.append(o)
    return outs, states[-1]


def _hgrn_prompt_kernel(a_ref, wq_ref, wf_ref, wi_ref, wg_ref, p_ref, ng_ref, near_ref, far_ref,
                        o_ref, s_ref, wbf, st, *, chunk):
    i = pl.program_id(1)
    dk = wq_ref.shape[1]

    @pl.when(i == 0)
    def _():
        wbf[:, 0 * dk:1 * dk] = wq_ref[...].astype(BF16)
        wbf[:, 1 * dk:2 * dk] = wf_ref[...].astype(BF16)
        wbf[:, 2 * dk:3 * dk] = wi_ref[...].astype(BF16)
        wbf[:, 3 * dk:4 * dk] = wg_ref[...].astype(BF16)
        st[...] = jnp.zeros_like(st)

    tm = a_ref.shape[0]
    projs = [jnp.dot(a_ref[c0:c0 + chunk, :], wbf[...], preferred_element_type=F32)
             for c0 in range(0, tm, chunk)]
    lb = _lower_bound(p_ref)
    ng = ng_ref[...]
    chunks = []
    for proj in projs:
        q = _silu(proj[:, 0 * dk:1 * dk])
        logf, k = _hgrn_gates(proj[:, 1 * dk:2 * dk], lb)
        chunks.append((q, k, proj[:, 2 * dk:3 * dk], logf))
    outs, s_t = _hgrn_chunks(chunks, st[...], near_ref[...], far_ref[...])
    for j, (o, proj) in enumerate(zip(outs, projs)):
        o_ref[j * chunk:(j + 1) * chunk, :] = _group_norm_gate(o, proj[:, 3 * dk:4 * dk], ng).astype(o_ref.dtype)
    st[...] = s_t
    s_ref[0, 0] = s_t.T


def _hgrn_prompt(a, w_in, lower_bounds, norm_g, n_heads, dk):
    m, d = a.shape
    chunk = min(HG_CHUNK, m)
    tm = _pick(m, (HG_ROWS, 512, 256, 128, 64, 32, 16, 8))
    tm = max(tm, chunk)
    near, far = _level_tables(chunk)
    w_spec = lambda blk: pl.BlockSpec((d, dk), lambda h, i: (0, blk * n_heads + h))
    o, s = pl.pallas_call(
        functools.partial(_hgrn_prompt_kernel, chunk=chunk),
        out_shape=(jax.ShapeDtypeStruct((m, n_heads * dk), BF16),
                   jax.ShapeDtypeStruct((1, n_heads, dk, dk), F32)),
        grid=(n_heads, m // tm),
        in_specs=[pl.BlockSpec((tm, d), lambda h, i: (i, 0)),
                  w_spec(0), w_spec(1), w_spec(2), w_spec(3),
                  pl.BlockSpec((lower_bounds.shape[0], dk), lambda h, i: (0, h)),
                  pl.BlockSpec((1, dk), lambda h, i: (0, h)),
                  pl.BlockSpec(near.shape, lambda h, i: (0, 0)),
                  pl.BlockSpec(far.shape, lambda h, i: (0, 0))],
        out_specs=(pl.BlockSpec((tm, dk), lambda h, i: (i, h)),
                   pl.BlockSpec((1, 1, dk, dk), lambda h, i: (0, h, 0, 0))),
        scratch_shapes=[pltpu.VMEM((d, 4 * dk), BF16), pltpu.VMEM((dk, dk), F32)],
        compiler_params=_cparams(("arbitrary", "arbitrary"), V7X_VMEM_LIMIT),
        name="hgrn_prompt",
    )(a, w_in, w_in, w_in, w_in, lower_bounds, norm_g.reshape(1, -1), near, far)
    return o, s


def _hgrn_sample_kernel(qr_ref, fr_ref, ir_ref, gr_ref, p_ref, ng_ref, s0_ref, o_ref, s_ref, *, seq):
    bs = s0_ref.shape[0]
    rows = bs * seq
    lb = _lower_bound(p_ref)
    q = _silu(qr_ref[...])
    logf, k = _hgrn_gates(fr_ref[...], lb)
    v = ir_ref[...]
    row = lax.broadcasted_iota(jnp.int32, q.shape, 0)
    t = row % seq
    b = logf
    d = 1
    while d < seq:
        b = b + jnp.where(t >= d, pltpu.roll(b, d, 0), 0.0)
        d *= 2
    o = jnp.sum(q * k, axis=-1, keepdims=True) * v
    for d in range(1, seq):
        ok = t >= d
        arg = jnp.where(ok, b - pltpu.roll(b, d, 0), 0.0)
        w = jnp.sum(q * pltpu.roll(k, d, 0) * jnp.exp(arg), axis=-1, keepdims=True)
        o = o + jnp.where(ok, w, 0.0) * pltpu.roll(v, d, 0)
    qe = (q * jnp.exp(b)).astype(BF16)
    b_end = b
    d = 1
    while d < seq:
        b_end = jnp.where(t < seq - d, pltpu.roll(b_end, rows - d, 0), b_end)
        d *= 2
    k_end = k * jnp.exp(b_end - b)
    k_end_t = k_end.T
    dec_t = jnp.exp(b_end).T
    lane = lax.broadcasted_iota(jnp.int32, k_end_t.shape, 1)
    vb = v.astype(BF16)
    o_inter = []
    for s in range(bs):
        s0 = s0_ref[s, 0]
        in_seq = (lane >= s * seq) & (lane < (s + 1) * seq)
        upd = jnp.dot(jnp.where(in_seq, k_end_t, 0.0).astype(BF16), vb, preferred_element_type=F32)
        s_ref[s, 0] = s0 * dec_t[:, s * seq:s * seq + 1] + upd
        if s % (8 // seq) == 0:
            r0 = s * seq
            parts = []
        oi = jnp.dot(qe[r0:r0 + 8, :], s0.astype(BF16), preferred_element_type=F32)
        parts.append(oi)
        if s % (8 // seq) == (8 // seq) - 1:
            r8 = lax.broadcasted_iota(jnp.int32, oi.shape, 0) // seq
            acc = parts[0]
            for j in range(1, len(parts)):
                acc = jnp.where(r8 == j, parts[j], acc)
            o_inter.append(acc)
    o = o + jnp.concatenate(o_inter, axis=0)
    o_ref[...] = _group_norm_gate(o, gr_ref[...], ng_ref[...]).astype(o_ref.dtype)


def _hgrn_sample(proj, lower_bounds, norm_g, state, seq):
    b, n_heads, dk, dv = state.shape
    m = proj.shape[0]
    bs = _pick(b, (32, 16, 8, 4, 2))
    rows = bs * seq
    col = lambda blk: pl.BlockSpec((rows, dk), lambda h, i: (i, blk * n_heads + h))
    o, s = pl.pallas_call(
        functools.partial(_hgrn_sample_kernel, seq=seq),
        out_shape=(jax.ShapeDtypeStruct((m, n_heads * dv), BF16),
                   jax.ShapeDtypeStruct(state.shape, F32)),
        grid=(n_heads, b // bs),
        in_specs=[col(0), col(1), col(2), col(3),
                  pl.BlockSpec((lower_bounds.shape[0], dk), lambda h, i: (0, h)),
                  pl.BlockSpec((1, dv), lambda h, i: (0, h)),
                  pl.BlockSpec((bs, 1, dk, dv), lambda h, i: (i, h, 0, 0))],
        out_specs=(pl.BlockSpec((rows, dv), lambda h, i: (i, h)),
                   pl.BlockSpec((bs, 1, dk, dv), lambda h, i: (i, h, 0, 0))),
        compiler_params=_cparams(("arbitrary", "arbitrary"), V7X_VMEM_LIMIT),
        name="hgrn_sample",
    )(proj, proj, proj, proj, lower_bounds, norm_g.reshape(1, -1), state)
    return o, s


def kernel(x_prompt, x_sample, cache_swa_k, cache_swa_v, state_hgrn, state_ffn_conv, c_prompt, c_sample,
           norm1_g, norm2_g, w_ada, b_ada, attn_w_qkv, attn_w_o, attn_sinks,
           hgrn_w_in, hgrn_lower_bounds, hgrn_norm_g, hgrn_w_o,
           ffn_w_in, ffn_conv_w, ffn_conv_b, ffn_w_out, final_norm_g):
    bp, lp, d = x_prompt.shape
    bsmp, ls, _ = x_sample.shape
    assert bp == 1, "the prompt kernels carry one sequence"
    depth = w_ada.shape[0]
    n_kv, hd = cache_swa_k.shape[2], cache_swa_k.shape[3]
    n_heads = attn_sinks.shape[0]
    dq = n_heads * hd
    dkv = n_kv * hd
    hg_heads, hg_dk = state_hgrn.shape[1], state_hgrn.shape[2]
    keep = min(WINDOW, lp)

    ns = bsmp * ls
    assert ns % 8 == 0
    c_all = jnp.concatenate([jnp.repeat(c_sample, ls, axis=0), c_prompt, jnp.zeros((7, d), F32)], axis=0)
    mod = _ada(c_all, w_ada, b_ada)
    pk = dict(sample=False, n_sample=ns)
    sk = dict(sample=True, n_sample=ns)

    xp = x_prompt.reshape(lp, d)
    xs = x_sample.reshape(ns, d)
    conv_p, conv_s = [], []
    outs = {}
    w_out_bf = ffn_w_out.astype(BF16)
    hp = _modulate(xp, norm1_g, mod, 0, 0, 1, False, ns)
    hs = _modulate(xs, norm1_g, mod, 0, 0, 1, True, ns)
    for l in range(depth):
        if l % 2 == 0:
            qkv_p = _matmul(hp, attn_w_qkv, tn=1024, name="qkv_prompt")
            qkv_s = _matmul(hs, attn_w_qkv, name="qkv_sample")
            op = _swa_prompt(qkv_p, attn_sinks, n_kv, hd)
            os_ = _swa_sample(qkv_s[:, :dq], qkv_s[:, dq:dq + dkv], qkv_s[:, dq + dkv:],
                              cache_swa_k, cache_swa_v, attn_sinks, n_kv, hd, ls)
            outs['swa_k_prompt'] = qkv_p[lp - keep:, dq:dq + dkv].reshape(bp, keep, n_kv, hd)
            outs['swa_v_prompt'] = qkv_p[lp - keep:, dq + dkv:].reshape(bp, keep, n_kv, hd)
            outs['swa_k_sample'] = qkv_s[:, dq:dq + dkv].reshape(bsmp, ls, n_kv, hd)
            outs['swa_v_sample'] = qkv_s[:, dq + dkv:].reshape(bsmp, ls, n_kv, hd)
            w_o = attn_w_o
        else:
            op, sp = _hgrn_prompt(hp, hgrn_w_in, hgrn_lower_bounds, hgrn_norm_g, hg_heads, hg_dk)
            proj_s = _matmul(hs, hgrn_w_in, name="hgrn_in_sample")
            os_, ss = _hgrn_sample(proj_s, hgrn_lower_bounds, hgrn_norm_g, state_hgrn, ls)
            outs['hgrn_state_prompt'] = sp
            outs['hgrn_state_sample'] = ss
            w_o = hgrn_w_o
        xp, hp = _mm_rows(op, w_o, xp, mod, gate=(l, 2), norm_g=norm2_g, norm_layer=l, next_mod=(l, 3, 4),
                          tm=512, name="mix_out_prompt", **pk)
        xs, hs = _mm_rows(os_, w_o, xs, mod, gate=(l, 2), norm_g=norm2_g, norm_layer=l, next_mod=(l, 3, 4),
                          tm=512, name="mix_out_sample", **sk)
        actp, bufp = _ffn_in_prompt(hp, ffn_w_in, ffn_conv_w, ffn_conv_b, l)
        acts, bufs = _ffn_in_sample(hs, ffn_w_in, ffn_conv_w, ffn_conv_b, l, state_ffn_conv[l], ls)
        conv_p.append(bufp[None])
        conv_s.append(bufs)
        if l + 1 < depth:
            nxt = dict(norm_g=norm1_g, norm_layer=l + 1, next_mod=(l + 1, 0, 1))
        else:
            nxt = dict(norm_g=final_norm_g, norm_layer=0, next_mod=None, keep_x=False, out_dtype=F32)
        xp, hp = _mm_rows(actp, w_out_bf, xp, mod, gate=(l, 5), w_layer=l, tm=256, name="ffn_out_prompt",
                          **nxt, **pk)
        xs, hs = _mm_rows(acts, w_out_bf, xs, mod, gate=(l, 5), w_layer=l, tm=256, name="ffn_out_sample",
                          **nxt, **sk)
    y_prompt = hp.reshape(bp, lp, d)
    y_sample = hs.reshape(bsmp, ls, d)
    return (y_prompt, y_sample,
            outs['swa_k_prompt'], outs['swa_v_prompt'], outs['swa_k_sample'], outs['swa_v_sample'],
            outs['hgrn_state_prompt'], outs['hgrn_state_sample'],
            jnp.stack(conv_p), jnp.stack(conv_s))
```

```python
import functools
import math

import numpy as np
import jax
import jax.numpy as jnp
from jax import lax
from jax.experimental import pallas as pl
from jax.experimental.pallas import tpu as pltpu

F32 = jnp.float32
BF16 = jnp.bfloat16

EPS = 1e-6
LOG2E = math.log2(math.e)
WINDOW = 128
CONV_W = 3
HG_CHUNK = 256
HG_ROWS = 1024
SWA_WAVE = 8
V7X_VMEM_LIMIT = 56 * 1024 * 1024


def _cparams(sem, vmem=None):
    return pltpu.CompilerParams(dimension_semantics=sem, vmem_limit_bytes=vmem)


def _pick(n, cands):
    for c in cands:
        if n % c == 0:
            return c
    return n


def _silu(x):
    return x * jax.nn.sigmoid(x)


def _logaddexp(a, b):
    return jnp.maximum(a, b) + jnp.log1p(jnp.exp(-jnp.abs(a - b)))


def _ada_kernel(c_ref, w_ref, b_ref, o_ref, act):
    @pl.when((pl.program_id(0) == 0) & (pl.program_id(1) == 0))
    def _():
        act[...] = _silu(c_ref[...]).astype(BF16)

    w = w_ref[0].astype(BF16)
    o_ref[0] = jnp.dot(act[...], w, preferred_element_type=F32) + b_ref[0]


def _ada(c_all, w_ada, b_ada):
    depth, d, n = w_ada.shape
    r = c_all.shape[0]
    tn = _pick(n, (1024, 512, 256, 128))
    return pl.pallas_call(
        _ada_kernel,
        out_shape=jax.ShapeDtypeStruct((depth, r, n), F32),
        grid=(depth, n // tn),
        in_specs=[pl.BlockSpec((r, d), lambda l, j: (0, 0)),
                  pl.BlockSpec((1, d, tn), lambda l, j: (l, 0, j)),
                  pl.BlockSpec((1, 1, tn), lambda l, j: (l, 0, j))],
        out_specs=pl.BlockSpec((1, r, tn), lambda l, j: (l, 0, j)),
        scratch_shapes=[pltpu.VMEM((r, d), BF16)],
        compiler_params=_cparams(("arbitrary", "arbitrary"), V7X_VMEM_LIMIT),
        name="ada",
    )(c_all, w_ada, b_ada.reshape(depth, 1, n))


def _mod_rows(sample, tm, n_sample):
    if sample:
        return tm, (lambda i: i)
    return 8, (lambda i: n_sample // 8)


def _modulate_kernel(x_ref, g_ref, sc_ref, sh_ref, o_ref, *, sample):
    x = x_ref[...]
    ms = jnp.mean(x * x, axis=-1, keepdims=True)
    y = x * lax.rsqrt(ms + EPS) * g_ref[...]
    sc = sc_ref[...] if sample else sc_ref[0:1, :]
    sh = sh_ref[...] if sample else sh_ref[0:1, :]
    o_ref[...] = (y * (1.0 + sc) + sh).astype(o_ref.dtype)


def _modulate(x, g_all, mod, l, shift_col, scale_col, sample, n_sample):
    m, d = x.shape
    tm = _pick(m, (512, 256, 128, 64, 32, 16, 8))
    rows, rmap = _mod_rows(sample, tm, n_sample)
    mspec = lambda col: pl.BlockSpec((None, rows, d), lambda i: (l, rmap(i), col))
    return pl.pallas_call(
        functools.partial(_modulate_kernel, sample=sample),
        out_shape=jax.ShapeDtypeStruct((m, d), BF16),
        grid=(m // tm,),
        in_specs=[pl.BlockSpec((tm, d), lambda i: (i, 0)),
                  pl.BlockSpec((None, 1, d), lambda i: (l, 0, 0)),
                  mspec(scale_col), mspec(shift_col)],
        out_specs=pl.BlockSpec((tm, d), lambda i: (i, 0)),
        compiler_params=_cparams(("arbitrary",), V7X_VMEM_LIMIT),
        name="modulate",
    )(x, g_all.reshape(g_all.shape[0], 1, d), mod, mod)


def _mm_kernel(a_ref, w_ref, o_ref, wbf):
    @pl.when(pl.program_id(1) == 0)
    def _():
        wbf[...] = w_ref[...].astype(BF16)

    o_ref[...] = jnp.dot(a_ref[...], wbf[...], preferred_element_type=F32)


def _mm2_kernel(a_ref, b_ref, w_ref, oa_ref, ob_ref, wbf, *, n_a):
    i = pl.program_id(1)

    @pl.when(i == 0)
    def _():
        wbf[...] = w_ref[...].astype(BF16)

    @pl.when(i < n_a)
    def _():
        oa_ref[...] = jnp.dot(a_ref[...], wbf[...], preferred_element_type=F32)

    @pl.when(i == n_a)
    def _():
        ob_ref[...] = jnp.dot(b_ref[...], wbf[...], preferred_element_type=F32)


def _matmul2(a, b, w, *, tm, tn, name):
    ma, k = a.shape
    mb = b.shape[0]
    n = w.shape[-1]
    tm, tn = min(tm, ma), min(tn, n)
    assert ma % tm == 0 and n % tn == 0
    n_a = ma // tm
    last = n_a - 1
    return pl.pallas_call(
        functools.partial(_mm2_kernel, n_a=n_a),
        out_shape=(jax.ShapeDtypeStruct((ma, n), F32), jax.ShapeDtypeStruct((mb, n), F32)),
        grid=(n // tn, n_a + 1),
        in_specs=[pl.BlockSpec((tm, k), lambda j, i: (jnp.minimum(i, last), 0)),
                  pl.BlockSpec((mb, k), lambda j, i: (0, 0)),
                  pl.BlockSpec((k, tn), lambda j, i: (0, j))],
        out_specs=(pl.BlockSpec((tm, tn), lambda j, i: (jnp.minimum(i, last), j)),
                   pl.BlockSpec((mb, tn), lambda j, i: (0, j))),
        scratch_shapes=[pltpu.VMEM((k, tn), BF16)],
        compiler_params=_cparams(("arbitrary", "arbitrary"), V7X_VMEM_LIMIT),
        name=name,
    )(a, b, w)


def _matmul(a, w, *, tm=None, tn=None, name="matmul"):
    m, k = a.shape
    n = w.shape[-1]
    tm = min(tm, m) if tm else _pick(m, (1024, 512, 256, 128, 64, 32, 16, 8))
    tn = min(tn, n) if tn else _pick(n, (512, 256, 128))
    assert m % tm == 0 and n % tn == 0
    return pl.pallas_call(
        _mm_kernel,
        out_shape=jax.ShapeDtypeStruct((m, n), F32),
        grid=(n // tn, m // tm),
        in_specs=[pl.BlockSpec((tm, k), lambda j, i: (i, 0)),
                  pl.BlockSpec((k, tn), lambda j, i: (0, j))],
        out_specs=pl.BlockSpec((tm, tn), lambda j, i: (i, j)),
        scratch_shapes=[pltpu.VMEM((k, tn), BF16)],
        compiler_params=_cparams(("arbitrary", "arbitrary"), V7X_VMEM_LIMIT),
        name=name,
    )(a, w)


def _mm_rows_kernel(*refs, cast_w, sample, modulate, keep_x):
    it = iter(refs)
    a_ref, w_ref, x_ref, gate_ref, g_ref = (next(it) for _ in range(5))
    sc_ref, sh_ref = (next(it), next(it)) if modulate else (None, None)
    xo_ref = next(it) if keep_x else None
    h_ref = next(it)
    if cast_w:
        wbf = next(it)

        @pl.when(pl.program_id(0) == 0)
        def _():
            wbf[...] = w_ref[...].astype(BF16)

        w = wbf[...]
    else:
        w = w_ref[...]
    rows = (lambda r: r[...]) if sample else (lambda r: r[0:1, :])
    x = x_ref[...] + rows(gate_ref) * jnp.dot(a_ref[...], w, preferred_element_type=F32)
    if keep_x:
        xo_ref[...] = x
    y = x * lax.rsqrt(jnp.mean(x * x, axis=-1, keepdims=True) + EPS) * g_ref[...]
    if modulate:
        y = y * (1.0 + rows(sc_ref)) + rows(sh_ref)
    h_ref[...] = y.astype(h_ref.dtype)


def _mm_rows(a, w, x, mod, *, gate, norm_g, norm_layer, next_mod=None, w_layer=None, sample, n_sample,
             tm, keep_x=True, out_dtype=None, name):
    m, k = a.shape
    n = w.shape[-1]
    tm = min(tm, m)
    assert m % tm == 0
    cast_w = w.dtype != BF16
    rows, rmap = _mod_rows(sample, tm, n_sample)
    once = pl.Buffered(1)
    if w_layer is None:
        w_spec = pl.BlockSpec((k, n), lambda i: (0, 0), pipeline_mode=once)
    else:
        w_spec = pl.BlockSpec((None, k, n), lambda i: (w_layer, 0, 0), pipeline_mode=once)
    mspec = lambda layer, col: pl.BlockSpec((None, rows, n), lambda i: (layer, rmap(i), col))
    in_specs = [pl.BlockSpec((tm, k), lambda i: (i, 0)), w_spec,
                pl.BlockSpec((tm, n), lambda i: (i, 0)),
                mspec(*gate),
                pl.BlockSpec((None, 1, n), lambda i: (norm_layer, 0, 0))]
    args = [a, w, x, mod, norm_g.reshape(-1, 1, n)]
    if next_mod is not None:
        in_specs += [mspec(next_mod[0], next_mod[2]), mspec(next_mod[0], next_mod[1])]
        args += [mod, mod]
    out_shape, out_specs = [], []
    if keep_x:
        out_shape.append(jax.ShapeDtypeStruct((m, n), F32))
        out_specs.append(pl.BlockSpec((tm, n), lambda i: (i, 0)))
    out_shape.append(jax.ShapeDtypeStruct((m, n), out_dtype or BF16))
    out_specs.append(pl.BlockSpec((tm, n), lambda i: (i, 0)))
    res = pl.pallas_call(
        functools.partial(_mm_rows_kernel, cast_w=cast_w, sample=sample, modulate=next_mod is not None,
                          keep_x=keep_x),
        out_shape=tuple(out_shape),
        grid=(m // tm,),
        in_specs=in_specs,
        out_specs=tuple(out_specs),
        scratch_shapes=[pltpu.VMEM((k, n), BF16)] if cast_w else [],
        compiler_params=_cparams(("arbitrary",), V7X_VMEM_LIMIT),
        name=name,
    )(*args)
    return res if keep_x else (None, res[0])


def _gelu_exact(a):
    return 0.5 * a * (1.0 + lax.erf(a * (1.0 / math.sqrt(2.0))))


def _ffn_in_kernel(ap_ref, as_ref, wu_ref, wg_ref, cw_ref, cb_ref, p1_ref, p2_ref,
                   actp_ref, tail_ref, acts_ref, us_ref, wbf, carry, *, n_prompt, seq):
    i = pl.program_id(1)
    tf = wu_ref.shape[1]
    cw = cw_ref[...]

    @pl.when(i == 0)
    def _():
        wbf[:, :tf] = wu_ref[...].astype(BF16)
        wbf[:, tf:] = wg_ref[...].astype(BF16)
        carry[...] = jnp.zeros_like(carry)

    def gated(u, u1, u2, g):
        conv = cw[0:1, :] * u2 + cw[1:2, :] * u1 + cw[2:3, :] * u + cb_ref[...]
        return _gelu_exact(conv) * g

    @pl.when(i < n_prompt)
    def _():
        ug = jnp.dot(ap_ref[...], wbf[...], preferred_element_type=F32)
        u = ug[:, :tf]
        tm = u.shape[0]
        row = lax.broadcasted_iota(jnp.int32, u.shape, 0)
        c = carry[...]
        u1 = jnp.where(row == 0, c[7:8, :], pltpu.roll(u, 1, 0))
        u2 = jnp.where(row == 0, c[6:7, :], jnp.where(row == 1, c[7:8, :], pltpu.roll(u, 2, 0)))
        actp_ref[...] = gated(u, u1, u2, ug[:, tf:]).astype(actp_ref.dtype)
        carry[...] = u[tm - 8:, :]
        tail_ref[...] = u[tm - 8:, :]

    @pl.when(i == n_prompt)
    def _():
        ug = jnp.dot(as_ref[...], wbf[...], preferred_element_type=F32)
        u = ug[:, :tf]
        row = lax.broadcasted_iota(jnp.int32, u.shape, 0) % seq
        u1 = jnp.where(row == 0, p1_ref[...], pltpu.roll(u, 1, 0))
        u2 = jnp.where(row < 2, p2_ref[...], pltpu.roll(u, 2, 0))
        acts_ref[...] = gated(u, u1, u2, ug[:, tf:]).astype(acts_ref.dtype)
        us_ref[...] = u


def _ffn_in(ap, a_s, w_in, conv_w, conv_b, l, buf, seq):
    mp, d = ap.shape
    ms = a_s.shape[0]
    f = w_in.shape[-1] // 2
    b = buf.shape[0]
    tf = _pick(f, (512, 256, 128))
    tm = _pick(mp, (1024, 512, 256, 128, 64, 32, 16, 8))
    nf = f // tf
    n_prompt = mp // tm
    zeros = jnp.zeros((b, seq - 1, f), F32)
    p1 = jnp.concatenate([buf[:, 1:2], zeros], axis=1).reshape(ms, f)
    p2 = jnp.concatenate([buf, zeros[:, :seq - 2]], axis=1).reshape(ms, f)
    last = n_prompt - 1
    actp, tail, acts, us = pl.pallas_call(
        functools.partial(_ffn_in_kernel, n_prompt=n_prompt, seq=seq),
        out_shape=(jax.ShapeDtypeStruct((mp, f), BF16), jax.ShapeDtypeStruct((8, f), F32),
                   jax.ShapeDtypeStruct((ms, f), BF16), jax.ShapeDtypeStruct((ms, f), F32)),
        grid=(nf, n_prompt + 1),
        in_specs=[pl.BlockSpec((tm, d), lambda j, i: (jnp.minimum(i, last), 0)),
                  pl.BlockSpec((ms, d), lambda j, i: (0, 0)),
                  pl.BlockSpec((None, d, tf), lambda j, i: (l, 0, j)),
                  pl.BlockSpec((None, d, tf), lambda j, i: (l, 0, nf + j)),
                  pl.BlockSpec((None, CONV_W, tf), lambda j, i: (l, 0, j)),
                  pl.BlockSpec((None, 1, tf), lambda j, i: (l, 0, j)),
                  pl.BlockSpec((ms, tf), lambda j, i: (0, j)),
                  pl.BlockSpec((ms, tf), lambda j, i: (0, j))],
        out_specs=(pl.BlockSpec((tm, tf), lambda j, i: (jnp.minimum(i, last), j)),
                   pl.BlockSpec((8, tf), lambda j, i: (0, j)),
                   pl.BlockSpec((ms, tf), lambda j, i: (0, j)),
                   pl.BlockSpec((ms, tf), lambda j, i: (0, j))),
        scratch_shapes=[pltpu.VMEM((d, 2 * tf), BF16), pltpu.VMEM((8, tf), F32)],
        compiler_params=_cparams(("arbitrary", "arbitrary"), V7X_VMEM_LIMIT),
        name="ffn_in",
    )(ap, a_s, w_in, w_in, conv_w, conv_b.reshape(conv_b.shape[0], 1, f), p1, p2)
    return actp, tail[6:8], acts, us.reshape(b, seq, f)[:, seq - (CONV_W - 1):]


def _alibi_slope(h, n_heads):
    return 2.0 ** (-8.0 * (h + 1) / n_heads)


def _swa_prompt_kernel(sink_ref, q_ref, kp_ref, kc_ref, vp_ref, vc_ref, o_ref, *, n_kv, group, hd):
    i = pl.program_id(0)
    n_heads = n_kv * group
    tq = q_ref.shape[0]
    t = lax.broadcasted_iota(jnp.int32, (tq, 2 * tq), 0)
    s = lax.broadcasted_iota(jnp.int32, (tq, 2 * tq), 1)
    dist = t + tq - s
    allowed = (dist >= 0) & (dist < WINDOW) & ((s >= tq) | (i > 0))
    sidx = s[0:1, :].astype(F32)
    trow = (t[:, 0:1] + tq).astype(F32)
    qscale = hd ** -0.5 * LOG2E
    nt = (((1,), (1,)), ((), ()))
    def scores(kv):
        ks = slice(kv * hd, (kv + 1) * hd)
        k2 = jnp.concatenate([kp_ref[:, ks], kc_ref[:, ks]], axis=0).astype(BF16)
        q4 = jnp.concatenate([(q_ref[:, h * hd:(h + 1) * hd] * qscale).astype(BF16)
                              for h in range(kv * group, (kv + 1) * group)], axis=0)
        return lax.dot_general(q4, k2, nt, preferred_element_type=F32)

    for kv0 in range(0, n_kv, SWA_WAVE):
        kvs = list(range(kv0, min(kv0 + SWA_WAVE, n_kv)))
        sc4s = [scores(kv) for kv in kvs]
        heads = [kv * group + g for kv in kvs for g in range(group)]
        slopes = [_alibi_slope(h, n_heads) * LOG2E for h in heads]
        scs = [jnp.where(allowed, sc4s[j // group][(j % group) * tq:(j % group + 1) * tq] + slopes[j] * sidx,
                         -jnp.inf) for j in range(len(heads))]
        sinks = [sink_ref[h] * LOG2E + slopes[j] * trow for j, h in enumerate(heads)]
        mxs = [jnp.maximum(jnp.max(scs[j], axis=-1, keepdims=True), sinks[j]) for j in range(len(heads))]
        es = [jnp.exp2(scs[j] - mxs[j]) for j in range(len(heads))]
        dens = [jnp.sum(es[j], axis=-1, keepdims=True) + jnp.exp2(sinks[j] - mxs[j])
                for j in range(len(heads))]
        es = [e.astype(BF16) for e in es]
        o4s = []
        for i, kv in enumerate(kvs):
            ks = slice(kv * hd, (kv + 1) * hd)
            v2 = jnp.concatenate([vp_ref[:, ks], vc_ref[:, ks]], axis=0).astype(BF16)
            o4s.append(jnp.dot(jnp.concatenate(es[i * group:(i + 1) * group], axis=0), v2,
                               preferred_element_type=F32))
        for j, h in enumerate(heads):
            o = o4s[j // group][(j % group) * tq:(j % group + 1) * tq]
            o_ref[:, h * hd:(h + 1) * hd] = (o / dens[j]).astype(o_ref.dtype)


def _swa_prompt(qkv, sinks, n_kv, hd):
    l = qkv.shape[0]
    n_heads = sinks.shape[0]
    group = n_heads // n_kv
    dq = n_heads * hd
    dk = n_kv * hd
    nb = l // WINDOW
    kblk = dq // dk
    return pl.pallas_call(
        functools.partial(_swa_prompt_kernel, n_kv=n_kv, group=group, hd=hd),
        out_shape=jax.ShapeDtypeStruct((l, dq), BF16),
        grid_spec=pltpu.PrefetchScalarGridSpec(
            num_scalar_prefetch=1,
            grid=(nb,),
            in_specs=[pl.BlockSpec((WINDOW, dq), lambda i, s: (i, 0)),
                      pl.BlockSpec((WINDOW, dk), lambda i, s: (jnp.maximum(i - 1, 0), kblk)),
                      pl.BlockSpec((WINDOW, dk), lambda i, s: (i, kblk)),
                      pl.BlockSpec((WINDOW, dk), lambda i, s: (jnp.maximum(i - 1, 0), kblk + 1)),
                      pl.BlockSpec((WINDOW, dk), lambda i, s: (i, kblk + 1))],
            out_specs=pl.BlockSpec((WINDOW, dq), lambda i, s: (i, 0)),
        ),
        compiler_params=_cparams(("arbitrary",), V7X_VMEM_LIMIT),
        name="swa_prompt",
    )(sinks, qkv, qkv, qkv, qkv, qkv)


def _swa_sample_kernel(q_ref, kn_ref, vn_ref, kc_ref, vc_ref, sink_ref, o_ref, *, n_kv, group, hd, seq):
    n_heads = n_kv * group
    bs = q_ref.shape[0]
    rows = group * seq
    cr = kc_ref.shape[1]
    scale = hd ** -0.5
    r = lax.broadcasted_iota(jnp.int32, (1, rows, cr), 1)
    t = r % seq
    gi = r // seq
    ci = lax.broadcasted_iota(jnp.int32, (1, rows, cr), 2)
    dist_c = (cr + t - ci).astype(F32)
    ok_c = ci > t + (cr - WINDOW)
    r1 = lax.broadcasted_iota(jnp.int32, (1, rows, 1), 1)
    t1 = r1 % seq
    g1 = (r1 // seq).astype(F32)
    kvs = range(n_kv)
    slope_c = [jnp.exp2(-8.0 * (kv * group + gi.astype(F32) + 1.0) / n_heads) for kv in kvs]
    slope_1 = [jnp.exp2(-8.0 * (kv * group + g1 + 1.0) / n_heads) for kv in kvs]
    qs = [q_ref[:, kv] * scale for kv in kvs]
    s_c = [jnp.einsum('bqd,bkd->bqk', qs[kv].astype(BF16),
                      kc_ref[:, :, kv * hd:(kv + 1) * hd].astype(BF16), preferred_element_type=F32)
           for kv in kvs]
    s_c = [jnp.where(ok_c, s_c[kv] - slope_c[kv] * dist_c, -jnp.inf) for kv in kvs]
    s_n = [[jnp.where(t1 >= j,
                      jnp.sum(qs[kv] * kn_ref[:, kv, j:j + 1, :], axis=-1, keepdims=True)
                      - slope_1[kv] * (t1 - j).astype(F32), -jnp.inf)
            for j in range(seq)] for kv in kvs]
    mx = [jnp.maximum(jnp.max(s_c[kv], axis=-1, keepdims=True), sink_ref[kv][None]) for kv in kvs]
    mx = [functools.reduce(jnp.maximum, s_n[kv], mx[kv]) for kv in kvs]
    e_c = [jnp.exp(s_c[kv] - mx[kv]) for kv in kvs]
    e_n = [[jnp.exp(s_n[kv][j] - mx[kv]) for j in range(seq)] for kv in kvs]
    den = [jnp.sum(e_c[kv], axis=-1, keepdims=True) + jnp.exp(sink_ref[kv][None] - mx[kv])
           + functools.reduce(jnp.add, e_n[kv]) for kv in kvs]
    o = [jnp.einsum('bqk,bkd->bqd', e_c[kv].astype(BF16),
                    vc_ref[:, :, kv * hd:(kv + 1) * hd].astype(BF16), preferred_element_type=F32)
         for kv in kvs]
    for kv in kvs:
        on = functools.reduce(jnp.add, [e_n[kv][j] * vn_ref[:, kv, j:j + 1, :] for j in range(seq)])
        o_ref[:, kv] = ((o[kv] + on) / den[kv]).astype(o_ref.dtype)


def _swa_sample(q, k_new, v_new, cache_k, cache_v, sinks, n_kv, hd, seq):
    b, cr = cache_k.shape[0], cache_k.shape[1]
    n_heads = sinks.shape[0]
    group = n_heads // n_kv
    rows = group * seq
    qr = q.reshape(b, seq, n_kv, group, hd).transpose(0, 2, 3, 1, 4).reshape(b, n_kv, rows, hd)
    kn = k_new.reshape(b, seq, n_kv, hd).transpose(0, 2, 1, 3)
    vn = v_new.reshape(b, seq, n_kv, hd).transpose(0, 2, 1, 3)
    sk = jnp.repeat(sinks.reshape(n_kv, group), seq, axis=1).reshape(n_kv, rows, 1)
    bs = _pick(b, (8, 4, 2, 1))
    o = pl.pallas_call(
        functools.partial(_swa_sample_kernel, n_kv=n_kv, group=group, hd=hd, seq=seq),
        out_shape=jax.ShapeDtypeStruct((b, n_kv, rows, hd), BF16),
        grid=(b // bs,),
        in_specs=[pl.BlockSpec((bs, n_kv, rows, hd), lambda i: (i, 0, 0, 0)),
                  pl.BlockSpec((bs, n_kv, seq, hd), lambda i: (i, 0, 0, 0)),
                  pl.BlockSpec((bs, n_kv, seq, hd), lambda i: (i, 0, 0, 0)),
                  pl.BlockSpec((bs, cr, n_kv * hd), lambda i: (i, 0, 0)),
                  pl.BlockSpec((bs, cr, n_kv * hd), lambda i: (i, 0, 0)),
                  pl.BlockSpec((n_kv, rows, 1), lambda i: (0, 0, 0))],
        out_specs=pl.BlockSpec((bs, n_kv, rows, hd), lambda i: (i, 0, 0, 0)),
        compiler_params=_cparams(("arbitrary",), V7X_VMEM_LIMIT),
        name="swa_sample",
    )(qr, kn, vn, cache_k.reshape(b, cr, n_kv * hd), cache_v.reshape(b, cr, n_kv * hd), sk)
    return o.reshape(b, n_kv, group, seq, hd).transpose(0, 3, 1, 2, 4).reshape(b * seq, n_heads * hd)


def _lower_bound(p_ref):
    p0 = p_ref[0:1, :]
    p1 = p_ref[1:2, :]
    mx = jnp.maximum(p0, p1)
    e0 = jnp.exp(p0 - mx)
    e1 = jnp.exp(p1 - mx)
    return e1 / (e0 + e1)


def _hgrn_gates(fr, lb):
    e = jnp.exp(-jnp.abs(fr))
    log_sig = jnp.minimum(fr, 0.0) - jnp.log1p(e)
    logf = _logaddexp(jnp.log(lb), jnp.log1p(-lb) + log_sig)
    r = 1.0 / (1.0 + e)
    k = (1.0 - lb) * jnp.where(fr >= 0.0, e * r, r)
    return logf, k


def _group_norm_gate(o, gr, ng):
    o = o * lax.rsqrt(jnp.mean(o * o, axis=-1, keepdims=True) + EPS) * ng
    return o * _silu(gr)


SMALL = 8

def _level_tables(c):
    g = min(c, 128)
    t = np.arange(g)[:, None]
    s = np.arange(g)[None, :]
    hb = np.floor(np.log2(np.maximum(t ^ s, 1))).astype(np.int32)
    same = (t // SMALL) == (s // SMALL)
    near = np.where(same & (s < t), hb, np.where(s == t, int(math.log2(SMALL)), -1))
    i = np.arange(c // 2)[:, None]
    j = np.arange(c // 2)[None, :]
    far = np.where(i == j, -1, np.floor(np.log2(np.maximum(i ^ j, 1)))).astype(np.int32)
    return jnp.asarray(near, dtype=jnp.int32), jnp.asarray(far, dtype=jnp.int32)


def _group_ref_rows(p, m, row):
    c, w = p.shape
    if m == 1:
        return jnp.where((row & 1) == 1, pltpu.roll(p, 1, 0), p)
    if m == 2:
        r4 = row & 3
        return jnp.where(r4 == 0, pltpu.roll(p, c - 1, 0),
                         jnp.where(r4 == 1, p,
                                   jnp.where(r4 == 2, pltpu.roll(p, 1, 0), pltpu.roll(p, 2, 0))))
    p3 = p.reshape(c // 8, 8, w)
    return jnp.broadcast_to(p3[:, 3:4, :], (c // 8, 8, w)).reshape(c, w)


def _halves(x, m):
    c = x.shape[0]
    ev = [x[r:r + m] for r in range(0, c, 2 * m)]
    od = [x[r + m:r + 2 * m] for r in range(0, c, 2 * m)]
    cat = lambda xs: xs[0] if len(xs) == 1 else jnp.concatenate(xs, axis=0)
    return cat(ev), cat(od)


def _interleave(ev, od, m):
    parts = []
    for r in range(0, ev.shape[0], m):
        parts += [ev[r:r + m], od[r:r + m]]
    return jnp.concatenate(parts, axis=0)


def _hgrn_chunks(chunks, st, near, far):
    c, w = chunks[0][0].shape
    nlev = int(math.log2(c))
    nsmall = int(math.log2(SMALL))
    g = near.shape[0]
    row = lax.broadcasted_iota(jnp.int32, (c, w), 0)
    nt = (((1,), (1,)), ((), ()))
    dot_nt = lambda x, y: lax.dot_general(x, y, nt, preferred_element_type=F32)
    dot_nn = lambda x, y: jnp.dot(x, y, preferred_element_type=F32)

    near_ops, far_ops, tails = [], [], []
    for q, k, v, logf in chunks:
        p = logf
        qs, ks = [q.astype(BF16)], [k.astype(BF16)]
        for l in range(nsmall):
            gref = _group_ref_rows(p, 1 << l, row)
            odd = ((row >> l) & 1) == 1
            e = jnp.exp(jnp.where(odd, p, gref - p))
            qs.append((q * e).astype(BF16))
            ks.append((k * e).astype(BF16))
            p = p + jnp.where(odd, gref, 0.0)
        near_ops.append((qs, ks, v.astype(BF16)))
        lev = []
        for l in range(nsmall, nlev):
            m = 1 << l
            pe, po = _halves(p, m)
            tot = jnp.concatenate([jnp.broadcast_to(pe[r + m - 1:r + m], (m, w))
                                   for r in range(0, c // 2, m)], axis=0)
            q_od = _halves(q, m)[1]
            k_ev = _halves(k, m)[0]
            v_ev = _halves(v, m)[0].astype(BF16)
            lev.append(((q_od * jnp.exp(po)).astype(BF16), (k_ev * jnp.exp(tot - pe)).astype(BF16), v_ev))
            p = _interleave(pe, po + tot, m)
        far_ops.append(lev)
        b = p
        b_end = b[c - 1:c, :]
        tails.append(((q * jnp.exp(b)).astype(BF16), (k * jnp.exp(b_end - b)).astype(BF16),
                      jnp.exp(b_end)))

    near_sc = [[[dot_nt(qs[j][r:r + g], ks[j][r:r + g]) for j in range(nsmall + 1)]
                for r in range(0, c, g)] for qs, ks, _ in near_ops]
    far_sc = [[dot_nt(qt, kt) for qt, kt, _ in lev] for lev in far_ops]
    upds = [lax.dot_general(vb, k_end, (((0,), (0,)), ((), ())), preferred_element_type=F32)
            for (_, _, vb), (_, k_end, _) in zip(near_ops, tails)]

    near_a = []
    for sc_chunk in near_sc:
        tiles = []
        for sc in sc_chunk:
            a = jnp.where(near == nsmall, sc[0], 0.0)
            for l in range(nsmall):
                a = jnp.where(near == l, sc[l + 1], a)
            tiles.append(a.astype(BF16))
        near_a.append(tiles)
    far_a = [[(jnp.where(far < nsmall + j, a, 0.0) if (1 << (nsmall + j)) < c // 2 else a).astype(BF16)
              for j, a in enumerate(lev)] for lev in far_sc]
    states = [st]
    for (_, _, dec), upd in zip(tails, upds):
        states.append(states[-1] * dec + upd)

    near_o = [[dot_nn(a, vb[i * g:(i + 1) * g]) for i, a in enumerate(tiles)]
              for tiles, (_, _, vb) in zip(near_a, near_ops)]
    far_o = [[dot_nn(a, ops[2]) for a, ops in zip(avs, lev)] for avs, lev in zip(far_a, far_ops)]
    inter_o = [dot_nt(qe, s.astype(BF16)) for (qe, _, _), s in zip(tails, states[:-1])]

    outs = []
    for no, fo, io in zip(near_o, far_o, inter_o):
        o = (no[0] if len(no) == 1 else jnp.concatenate(no, axis=0)) + io
        for j, x in enumerate(fo):
            m = 1 << (nsmall + j)
            o_ev, o_od = _halves(o, m)
            o = _interleave(o_ev, o_od + x, m)
        outs.append(o)
    return outs, states[-1]


def _hgrn_prompt_kernel(a_ref, wq_ref, wf_ref, wi_ref, wg_ref, p_ref, ng_ref, near_ref, far_ref,
                        o_ref, s_ref, wbf, st, *, chunk):
    i = pl.program_id(1)
    dk = wq_ref.shape[1]

    @pl.when(i == 0)
    def _():
        wbf[:, 0 * dk:1 * dk] = wq_ref[...].astype(BF16)
        wbf[:, 1 * dk:2 * dk] = wf_ref[...].astype(BF16)
        wbf[:, 2 * dk:3 * dk] = wi_ref[...].astype(BF16)
        wbf[:, 3 * dk:4 * dk] = wg_ref[...].astype(BF16)
        st[...] = jnp.zeros_like(st)

    tm = a_ref.shape[0]
    projs = [jnp.dot(a_ref[c0:c0 + chunk, :], wbf[...], preferred_element_type=F32)
             for c0 in range(0, tm, chunk)]
    lb = _lower_bound(p_ref)
    ng = ng_ref[...]
    chunks = []
    for proj in projs:
        q = _silu(proj[:, 0 * dk:1 * dk])
        logf, k = _hgrn_gates(proj[:, 1 * dk:2 * dk], lb)
        chunks.append((q, k, proj[:, 2 * dk:3 * dk], logf))
    outs, s_t = _hgrn_chunks(chunks, st[...], near_ref[...], far_ref[...])
    for j, (o, proj) in enumerate(zip(outs, projs)):
        o_ref[j * chunk:(j + 1) * chunk, :] = _group_norm_gate(o, proj[:, 3 * dk:4 * dk], ng).astype(o_ref.dtype)
    st[...] = s_t
    s_ref[0, 0] = s_t.T


def _hgrn_prompt(a, w_in, lower_bounds, norm_g, n_heads, dk):
    m, d = a.shape
    chunk = min(HG_CHUNK, m)
    tm = _pick(m, (HG_ROWS, 512, 256, 128, 64, 32, 16, 8))
    tm = max(tm, chunk)
    near, far = _level_tables(chunk)
    w_spec = lambda blk: pl.BlockSpec((d, dk), lambda h, i: (0, blk * n_heads + h))
    o, s = pl.pallas_call(
        functools.partial(_hgrn_prompt_kernel, chunk=chunk),
        out_shape=(jax.ShapeDtypeStruct((m, n_heads * dk), BF16),
                   jax.ShapeDtypeStruct((1, n_heads, dk, dk), F32)),
        grid=(n_heads, m // tm),
        in_specs=[pl.BlockSpec((tm, d), lambda h, i: (i, 0)),
                  w_spec(0), w_spec(1), w_spec(2), w_spec(3),
                  pl.BlockSpec((lower_bounds.shape[0], dk), lambda h, i: (0, h)),
                  pl.BlockSpec((1, dk), lambda h, i: (0, h)),
                  pl.BlockSpec(near.shape, lambda h, i: (0, 0)),
                  pl.BlockSpec(far.shape, lambda h, i: (0, 0))],
        out_specs=(pl.BlockSpec((tm, dk), lambda h, i: (i, h)),
                   pl.BlockSpec((1, 1, dk, dk), lambda h, i: (0, h, 0, 0))),
        scratch_shapes=[pltpu.VMEM((d, 4 * dk), BF16), pltpu.VMEM((dk, dk), F32)],
        compiler_params=_cparams(("arbitrary", "arbitrary"), V7X_VMEM_LIMIT),
        name="hgrn_prompt",
    )(a, w_in, w_in, w_in, w_in, lower_bounds, norm_g.reshape(1, -1), near, far)
    return o, s


def _hgrn_sample_kernel(qr_ref, fr_ref, ir_ref, gr_ref, p_ref, ng_ref, s0_ref, o_ref, s_ref, *, seq):
    bs = s0_ref.shape[0]
    rows = bs * seq
    lb = _lower_bound(p_ref)
    q = _silu(qr_ref[...])
    logf, k = _hgrn_gates(fr_ref[...], lb)
    v = ir_ref[...]
    row = lax.broadcasted_iota(jnp.int32, q.shape, 0)
    t = row % seq
    b = logf
    d = 1
    while d < seq:
        b = b + jnp.where(t >= d, pltpu.roll(b, d, 0), 0.0)
        d *= 2
    o = jnp.sum(q * k, axis=-1, keepdims=True) * v
    for d in range(1, seq):
        ok = t >= d
        arg = jnp.where(ok, b - pltpu.roll(b, d, 0), 0.0)
        w = jnp.sum(q * pltpu.roll(k, d, 0) * jnp.exp(arg), axis=-1, keepdims=True)
        o = o + jnp.where(ok, w, 0.0) * pltpu.roll(v, d, 0)
    qe = (q * jnp.exp(b)).astype(BF16)
    b_end = b
    d = 1
    while d < seq:
        b_end = jnp.where(t < seq - d, pltpu.roll(b_end, rows - d, 0), b_end)
        d *= 2
    k_end = k * jnp.exp(b_end - b)
    k_end_t = k_end.T
    dec_t = jnp.exp(b_end).T
    lane = lax.broadcasted_iota(jnp.int32, k_end_t.shape, 1)
    vb = v.astype(BF16)
    o_inter = []
    for s in range(bs):
        s0 = s0_ref[s, 0]
        in_seq = (lane >= s * seq) & (lane < (s + 1) * seq)
        upd = jnp.dot(jnp.where(in_seq, k_end_t, 0.0).astype(BF16), vb, preferred_element_type=F32)
        s_ref[s, 0] = s0 * dec_t[:, s * seq:s * seq + 1] + upd
        if s % (8 // seq) == 0:
            r0 = s * seq
            parts = []
        oi = jnp.dot(qe[r0:r0 + 8, :], s0.astype(BF16), preferred_element_type=F32)
        parts.append(oi)
        if s % (8 // seq) == (8 // seq) - 1:
            r8 = lax.broadcasted_iota(jnp.int32, oi.shape, 0) // seq
            acc = parts[0]
            for j in range(1, len(parts)):
                acc = jnp.where(r8 == j, parts[j], acc)
            o_inter.append(acc)
    o = o + jnp.concatenate(o_inter, axis=0)
    o_ref[...] = _group_norm_gate(o, gr_ref[...], ng_ref[...]).astype(o_ref.dtype)


def _hgrn_sample(proj, lower_bounds, norm_g, state, seq):
    b, n_heads, dk, dv = state.shape
    m = proj.shape[0]
    bs = _pick(b, (64, 32, 16, 8, 4, 2))
    rows = bs * seq
    col = lambda blk: pl.BlockSpec((rows, dk), lambda h, i: (i, blk * n_heads + h))
    o, s = pl.pallas_call(
        functools.partial(_hgrn_sample_kernel, seq=seq),
        out_shape=(jax.ShapeDtypeStruct((m, n_heads * dv), BF16),
                   jax.ShapeDtypeStruct(state.shape, F32)),
        grid=(n_heads, b // bs),
        in_specs=[col(0), col(1), col(2), col(3),
                  pl.BlockSpec((lower_bounds.shape[0], dk), lambda h, i: (0, h)),
                  pl.BlockSpec((1, dv), lambda h, i: (0, h)),
                  pl.BlockSpec((bs, 1, dk, dv), lambda h, i: (i, h, 0, 0))],
        out_specs=(pl.BlockSpec((rows, dv), lambda h, i: (i, h)),
                   pl.BlockSpec((bs, 1, dk, dv), lambda h, i: (i, h, 0, 0))),
        compiler_params=_cparams(("arbitrary", "arbitrary"), V7X_VMEM_LIMIT),
        name="hgrn_sample",
    )(proj, proj, proj, proj, lower_bounds, norm_g.reshape(1, -1), state)
    return o, s


def kernel(x_prompt, x_sample, cache_swa_k, cache_swa_v, state_hgrn, state_ffn_conv, c_prompt, c_sample,
           norm1_g, norm2_g, w_ada, b_ada, attn_w_qkv, attn_w_o, attn_sinks,
           hgrn_w_in, hgrn_lower_bounds, hgrn_norm_g, hgrn_w_o,
           ffn_w_in, ffn_conv_w, ffn_conv_b, ffn_w_out, final_norm_g):
    bp, lp, d = x_prompt.shape
    bsmp, ls, _ = x_sample.shape
    assert bp == 1, "the prompt kernels carry one sequence"
    depth = w_ada.shape[0]
    n_kv, hd = cache_swa_k.shape[2], cache_swa_k.shape[3]
    n_heads = attn_sinks.shape[0]
    dq = n_heads * hd
    dkv = n_kv * hd
    hg_heads, hg_dk = state_hgrn.shape[1], state_hgrn.shape[2]
    keep = min(WINDOW, lp)

    ns = bsmp * ls
    assert ns % 8 == 0
    c_all = jnp.concatenate([jnp.repeat(c_sample, ls, axis=0), c_prompt, jnp.zeros((7, d), F32)], axis=0)
    mod = _ada(c_all, w_ada, b_ada)
    pk = dict(sample=False, n_sample=ns)
    sk = dict(sample=True, n_sample=ns)

    xp = x_prompt.reshape(lp, d)
    xs = x_sample.reshape(ns, d)
    conv_p, conv_s = [], []
    outs = {}
    w_out_bf = ffn_w_out.astype(BF16)
    hp = _modulate(xp, norm1_g, mod, 0, 0, 1, False, ns)
    hs = _modulate(xs, norm1_g, mod, 0, 0, 1, True, ns)
    for l in range(depth):
        if l % 2 == 0:
            qkv_p, qkv_s = _matmul2(hp, hs, attn_w_qkv, tm=1024, tn=1024, name="qkv")
            op = _swa_prompt(qkv_p, attn_sinks, n_kv, hd)
            os_ = _swa_sample(qkv_s[:, :dq], qkv_s[:, dq:dq + dkv], qkv_s[:, dq + dkv:],
                              cache_swa_k, cache_swa_v, attn_sinks, n_kv, hd, ls)
            outs['swa_k_prompt'] = qkv_p[lp - keep:, dq:dq + dkv].reshape(bp, keep, n_kv, hd)
            outs['swa_v_prompt'] = qkv_p[lp - keep:, dq + dkv:].reshape(bp, keep, n_kv, hd)
            outs['swa_k_sample'] = qkv_s[:, dq:dq + dkv].reshape(bsmp, ls, n_kv, hd)
            outs['swa_v_sample'] = qkv_s[:, dq + dkv:].reshape(bsmp, ls, n_kv, hd)
            w_o = attn_w_o
        else:
            op, sp = _hgrn_prompt(hp, hgrn_w_in, hgrn_lower_bounds, hgrn_norm_g, hg_heads, hg_dk)
            proj_s = _matmul(hs, hgrn_w_in, name="hgrn_in_sample")
            os_, ss = _hgrn_sample(proj_s, hgrn_lower_bounds, hgrn_norm_g, state_hgrn, ls)
            outs['hgrn_state_prompt'] = sp
            outs['hgrn_state_sample'] = ss
            w_o = hgrn_w_o
        xp, hp = _mm_rows(op, w_o, xp, mod, gate=(l, 2), norm_g=norm2_g, norm_layer=l, next_mod=(l, 3, 4),
                          tm=512, name="mix_out_prompt", **pk)
        xs, hs = _mm_rows(os_, w_o, xs, mod, gate=(l, 2), norm_g=norm2_g, norm_layer=l, next_mod=(l, 3, 4),
                          tm=512, name="mix_out_sample", **sk)
        actp, bufp, acts, bufs = _ffn_in(hp, hs, ffn_w_in, ffn_conv_w, ffn_conv_b, l, state_ffn_conv[l], ls)
        conv_p.append(bufp[None])
        conv_s.append(bufs)
        if l + 1 < depth:
            nxt = dict(norm_g=norm1_g, norm_layer=l + 1, next_mod=(l + 1, 0, 1))
        else:
            nxt = dict(norm_g=final_norm_g, norm_layer=0, next_mod=None, keep_x=False, out_dtype=F32)
        xp, hp = _mm_rows(actp, w_out_bf, xp, mod, gate=(l, 5), w_layer=l, tm=256, name="ffn_out_prompt",
                          **nxt, **pk)
        xs, hs = _mm_rows(acts, w_out_bf, xs, mod, gate=(l, 5), w_layer=l, tm=256, name="ffn_out_sample",
                          **nxt, **sk)
    y_prompt = hp.reshape(bp, lp, d)
    y_sample = hs.reshape(bsmp, ls, d)
    return (y_prompt, y_sample,
            outs['swa_k_prompt'], outs['swa_v_prompt'], outs['swa_k_sample'], outs['swa_v_sample'],
            outs['hgrn_state_prompt'], outs['hgrn_state_sample'],
            jnp.stack(conv_p), jnp.stack(conv_s))
```

```python
import functools
import math

import numpy as np
import jax
import jax.numpy as jnp
from jax import lax
from jax.experimental import pallas as pl
from jax.experimental.pallas import tpu as pltpu

F32 = jnp.float32
BF16 = jnp.bfloat16

EPS = 1e-6
LOG2E = math.log2(math.e)
WINDOW = 128
CONV_W = 3
LANES = 128
HG_CHUNK = 256
HG_ROWS = 1024
SWA_WAVE = 8
V7X_VMEM_LIMIT = 56 * 1024 * 1024


def _cparams(sem, vmem=None):
    return pltpu.CompilerParams(dimension_semantics=sem, vmem_limit_bytes=vmem)


def _pick(n, cands):
    for c in cands:
        if n % c == 0:
            return c
    return n


def _silu(x):
    return x * jax.nn.sigmoid(x)


def _logaddexp(a, b):
    return jnp.maximum(a, b) + jnp.log1p(jnp.exp(-jnp.abs(a - b)))


def _ada_kernel(c_ref, w_ref, b_ref, o_ref, act):
    @pl.when((pl.program_id(0) == 0) & (pl.program_id(1) == 0))
    def _():
        act[...] = _silu(c_ref[...]).astype(BF16)

    w = w_ref[0].astype(BF16)
    o_ref[0] = jnp.dot(act[...], w, preferred_element_type=F32) + b_ref[0]


def _ada(c_all, w_ada, b_ada):
    depth, d, n = w_ada.shape
    r = c_all.shape[0]
    tn = _pick(n, (1024, 512, 256, 128))
    return pl.pallas_call(
        _ada_kernel,
        out_shape=jax.ShapeDtypeStruct((depth, r, n), F32),
        grid=(depth, n // tn),
        in_specs=[pl.BlockSpec((r, d), lambda l, j: (0, 0)),
                  pl.BlockSpec((1, d, tn), lambda l, j: (l, 0, j)),
                  pl.BlockSpec((1, 1, tn), lambda l, j: (l, 0, j))],
        out_specs=pl.BlockSpec((1, r, tn), lambda l, j: (l, 0, j)),
        scratch_shapes=[pltpu.VMEM((r, d), BF16)],
        compiler_params=_cparams(("arbitrary", "arbitrary"), V7X_VMEM_LIMIT),
        name="ada",
    )(c_all, w_ada, b_ada.reshape(depth, 1, n))


def _mod_rows(sample, tm, n_sample):
    if sample:
        return tm, (lambda i: i)
    return 8, (lambda i: n_sample // 8)


def _modulate_kernel(x_ref, g_ref, sc_ref, sh_ref, o_ref, *, sample):
    x = x_ref[...]
    ms = jnp.mean(x * x, axis=-1, keepdims=True)
    y = x * lax.rsqrt(ms + EPS) * g_ref[...]
    sc = sc_ref[...] if sample else sc_ref[0:1, :]
    sh = sh_ref[...] if sample else sh_ref[0:1, :]
    o_ref[...] = (y * (1.0 + sc) + sh).astype(o_ref.dtype)


def _modulate(x, g_all, mod, l, shift_col, scale_col, sample, n_sample):
    m, d = x.shape
    tm = _pick(m, (512, 256, 128, 64, 32, 16, 8))
    rows, rmap = _mod_rows(sample, tm, n_sample)
    mspec = lambda col: pl.BlockSpec((None, rows, d), lambda i: (l, rmap(i), col))
    return pl.pallas_call(
        functools.partial(_modulate_kernel, sample=sample),
        out_shape=jax.ShapeDtypeStruct((m, d), BF16),
        grid=(m // tm,),
        in_specs=[pl.BlockSpec((tm, d), lambda i: (i, 0)),
                  pl.BlockSpec((None, 1, d), lambda i: (l, 0, 0)),
                  mspec(scale_col), mspec(shift_col)],
        out_specs=pl.BlockSpec((tm, d), lambda i: (i, 0)),
        compiler_params=_cparams(("arbitrary",), V7X_VMEM_LIMIT),
        name="modulate",
    )(x, g_all.reshape(g_all.shape[0], 1, d), mod, mod)


def _mm_kernel(a_ref, w_ref, o_ref, wbf):
    @pl.when(pl.program_id(1) == 0)
    def _():
        wbf[...] = w_ref[...].astype(BF16)

    o_ref[...] = jnp.dot(a_ref[...], wbf[...], preferred_element_type=F32)


def _mm2_kernel(a_ref, b_ref, w_ref, oa_ref, ob_ref, wbf, *, n_a):
    i = pl.program_id(1)

    @pl.when(i == 0)
    def _():
        wbf[...] = w_ref[...].astype(BF16)

    @pl.when(i < n_a)
    def _():
        oa_ref[...] = jnp.dot(a_ref[...], wbf[...], preferred_element_type=F32)

    @pl.when(i == n_a)
    def _():
        ob_ref[...] = jnp.dot(b_ref[...], wbf[...], preferred_element_type=F32)


def _matmul2(a, b, w, *, tm, tn, name):
    ma, k = a.shape
    mb = b.shape[0]
    n = w.shape[-1]
    tm, tn = min(tm, ma), min(tn, n)
    assert ma % tm == 0 and n % tn == 0
    n_a = ma // tm
    last = n_a - 1
    return pl.pallas_call(
        functools.partial(_mm2_kernel, n_a=n_a),
        out_shape=(jax.ShapeDtypeStruct((ma, n), F32), jax.ShapeDtypeStruct((mb, n), F32)),
        grid=(n // tn, n_a + 1),
        in_specs=[pl.BlockSpec((tm, k), lambda j, i: (jnp.minimum(i, last), 0)),
                  pl.BlockSpec((mb, k), lambda j, i: (0, 0)),
                  pl.BlockSpec((k, tn), lambda j, i: (0, j))],
        out_specs=(pl.BlockSpec((tm, tn), lambda j, i: (jnp.minimum(i, last), j)),
                   pl.BlockSpec((mb, tn), lambda j, i: (0, j))),
        scratch_shapes=[pltpu.VMEM((k, tn), BF16)],
        compiler_params=_cparams(("arbitrary", "arbitrary"), V7X_VMEM_LIMIT),
        name=name,
    )(a, b, w)


def _matmul(a, w, *, tm=None, tn=None, name="matmul"):
    m, k = a.shape
    n = w.shape[-1]
    tm = min(tm, m) if tm else _pick(m, (1024, 512, 256, 128, 64, 32, 16, 8))
    tn = min(tn, n) if tn else _pick(n, (512, 256, 128))
    assert m % tm == 0 and n % tn == 0
    return pl.pallas_call(
        _mm_kernel,
        out_shape=jax.ShapeDtypeStruct((m, n), F32),
        grid=(n // tn, m // tm),
        in_specs=[pl.BlockSpec((tm, k), lambda j, i: (i, 0)),
                  pl.BlockSpec((k, tn), lambda j, i: (0, j))],
        out_specs=pl.BlockSpec((tm, tn), lambda j, i: (i, j)),
        scratch_shapes=[pltpu.VMEM((k, tn), BF16)],
        compiler_params=_cparams(("arbitrary", "arbitrary"), V7X_VMEM_LIMIT),
        name=name,
    )(a, w)


def _mm_rows_kernel(*refs, cast_w, sample, modulate, keep_x):
    it = iter(refs)
    a_ref, w_ref, x_ref, gate_ref, g_ref = (next(it) for _ in range(5))
    sc_ref, sh_ref = (next(it), next(it)) if modulate else (None, None)
    xo_ref = next(it) if keep_x else None
    h_ref = next(it)
    if cast_w:
        wbf = next(it)

        @pl.when(pl.program_id(0) == 0)
        def _():
            wbf[...] = w_ref[...].astype(BF16)

        w = wbf[...]
    else:
        w = w_ref[...]
    rows = (lambda r: r[...]) if sample else (lambda r: r[0:1, :])
    x = x_ref[...] + rows(gate_ref) * jnp.dot(a_ref[...], w, preferred_element_type=F32)
    if keep_x:
        xo_ref[...] = x
    y = x * lax.rsqrt(jnp.mean(x * x, axis=-1, keepdims=True) + EPS) * g_ref[...]
    if modulate:
        y = y * (1.0 + rows(sc_ref)) + rows(sh_ref)
    h_ref[...] = y.astype(h_ref.dtype)


def _mm_rows(a, w, x, mod, *, gate, norm_g, norm_layer, next_mod=None, w_layer=None, sample, n_sample,
             tm, keep_x=True, out_dtype=None, name):
    m, k = a.shape
    n = w.shape[-1]
    tm = min(tm, m)
    assert m % tm == 0
    cast_w = w.dtype != BF16
    rows, rmap = _mod_rows(sample, tm, n_sample)
    once = pl.Buffered(1)
    if w_layer is None:
        w_spec = pl.BlockSpec((k, n), lambda i: (0, 0), pipeline_mode=once)
    else:
        w_spec = pl.BlockSpec((None, k, n), lambda i: (w_layer, 0, 0), pipeline_mode=once)
    mspec = lambda layer, col: pl.BlockSpec((None, rows, n), lambda i: (layer, rmap(i), col))
    in_specs = [pl.BlockSpec((tm, k), lambda i: (i, 0)), w_spec,
                pl.BlockSpec((tm, n), lambda i: (i, 0)),
                mspec(*gate),
                pl.BlockSpec((None, 1, n), lambda i: (norm_layer, 0, 0))]
    args = [a, w, x, mod, norm_g.reshape(-1, 1, n)]
    if next_mod is not None:
        in_specs += [mspec(next_mod[0], next_mod[2]), mspec(next_mod[0], next_mod[1])]
        args += [mod, mod]
    out_shape, out_specs = [], []
    if keep_x:
        out_shape.append(jax.ShapeDtypeStruct((m, n), F32))
        out_specs.append(pl.BlockSpec((tm, n), lambda i: (i, 0)))
    out_shape.append(jax.ShapeDtypeStruct((m, n), out_dtype or BF16))
    out_specs.append(pl.BlockSpec((tm, n), lambda i: (i, 0)))
    res = pl.pallas_call(
        functools.partial(_mm_rows_kernel, cast_w=cast_w, sample=sample, modulate=next_mod is not None,
                          keep_x=keep_x),
        out_shape=tuple(out_shape),
        grid=(m // tm,),
        in_specs=in_specs,
        out_specs=tuple(out_specs),
        scratch_shapes=[pltpu.VMEM((k, n), BF16)] if cast_w else [],
        compiler_params=_cparams(("arbitrary",), V7X_VMEM_LIMIT),
        name=name,
    )(*args)
    return res if keep_x else (None, res[0])


def _gelu_exact(a):
    return 0.5 * a * (1.0 + lax.erf(a * (1.0 / math.sqrt(2.0))))


def _ffn_in_prompt_kernel(a_ref, wu_ref, wg_ref, cw_ref, cb_ref, act_ref, tail_ref, wbf, carry):
    i = pl.program_id(1)
    tf = wu_ref.shape[1]

    @pl.when(i == 0)
    def _():
        wbf[:, :tf] = wu_ref[...].astype(BF16)
        wbf[:, tf:] = wg_ref[...].astype(BF16)
        carry[...] = jnp.zeros_like(carry)

    ug = jnp.dot(a_ref[...], wbf[...], preferred_element_type=F32)
    u = ug[:, :tf]
    g = ug[:, tf:]
    tm = u.shape[0]
    row = lax.broadcasted_iota(jnp.int32, u.shape, 0)
    c = carry[...]
    u1 = jnp.where(row == 0, c[7:8, :], pltpu.roll(u, 1, 0))
    u2 = jnp.where(row == 0, c[6:7, :], jnp.where(row == 1, c[7:8, :], pltpu.roll(u, 2, 0)))
    cw = cw_ref[...]
    conv = cw[0:1, :] * u2 + cw[1:2, :] * u1 + cw[2:3, :] * u + cb_ref[...]
    act_ref[...] = (_gelu_exact(conv) * g).astype(act_ref.dtype)
    carry[...] = u[tm - 8:, :]
    tail_ref[...] = u[tm - 8:, :]


def _ffn_in_sample_kernel(a_ref, wu_ref, wg_ref, cw_ref, cb_ref, buf_ref, act_ref, nbuf_ref,
                          us, gs, acts, *, seq):
    a = a_ref[...]
    b = buf_ref.shape[0]
    u_all = jnp.dot(a, wu_ref[...].astype(BF16), preferred_element_type=F32)
    g_all = jnp.dot(a, wg_ref[...].astype(BF16), preferred_element_type=F32)
    cw = cw_ref[...]
    cb = cb_ref[...]
    for c in range(us.shape[0]):
        cs = slice(c * LANES, (c + 1) * LANES)
        us[c] = u_all[:, cs]
        gs[c] = g_all[:, cs]
        prev2, prev1 = buf_ref[:, 0, cs], buf_ref[:, 1, cs]
        for t in range(seq):
            u = us[c, pl.ds(t, b, stride=seq), :]
            conv = cw[0:1, cs] * prev2 + cw[1:2, cs] * prev1 + cw[2:3, cs] * u + cb[:, cs]
            acts[c, pl.ds(t, b, stride=seq), :] = _gelu_exact(conv) * gs[c, pl.ds(t, b, stride=seq), :]
            prev2, prev1 = prev1, u
        act_ref[:, cs] = acts[c].astype(act_ref.dtype)
        nbuf_ref[:, 0, cs] = prev2
        nbuf_ref[:, 1, cs] = prev1


def _ffn_in_prompt(a, w_in, conv_w, conv_b, l):
    m, d = a.shape
    f = w_in.shape[-1] // 2
    tf = _pick(f, (512, 256, 128))
    tm = _pick(m, (1024, 512, 256, 128, 64, 32, 16, 8))
    nf = f // tf
    act, tail = pl.pallas_call(
        _ffn_in_prompt_kernel,
        out_shape=(jax.ShapeDtypeStruct((m, f), BF16), jax.ShapeDtypeStruct((8, f), F32)),
        grid=(nf, m // tm),
        in_specs=[pl.BlockSpec((tm, d), lambda j, i: (i, 0)),
                  pl.BlockSpec((None, d, tf), lambda j, i: (l, 0, j)),
                  pl.BlockSpec((None, d, tf), lambda j, i: (l, 0, nf + j)),
                  pl.BlockSpec((None, CONV_W, tf), lambda j, i: (l, 0, j)),
                  pl.BlockSpec((None, 1, tf), lambda j, i: (l, 0, j))],
        out_specs=(pl.BlockSpec((tm, tf), lambda j, i: (i, j)),
                   pl.BlockSpec((8, tf), lambda j, i: (0, j))),
        scratch_shapes=[pltpu.VMEM((d, 2 * tf), BF16), pltpu.VMEM((8, tf), F32)],
        compiler_params=_cparams(("arbitrary", "arbitrary"), V7X_VMEM_LIMIT),
        name="ffn_in_prompt",
    )(a, w_in, w_in, conv_w, conv_b.reshape(conv_b.shape[0], 1, f))
    return act, tail[6:8]


def _ffn_in_sample(a, w_in, conv_w, conv_b, l, state, seq):
    m, d = a.shape
    f = w_in.shape[-1] // 2
    b = state.shape[1]
    tf = _pick(f, (512, 256, 128))
    nf = f // tf
    return pl.pallas_call(
        functools.partial(_ffn_in_sample_kernel, seq=seq),
        out_shape=(jax.ShapeDtypeStruct((m, f), BF16), jax.ShapeDtypeStruct((b, CONV_W - 1, f), F32)),
        grid=(nf,),
        in_specs=[pl.BlockSpec((m, d), lambda j: (0, 0)),
                  pl.BlockSpec((None, d, tf), lambda j: (l, 0, j)),
                  pl.BlockSpec((None, d, tf), lambda j: (l, 0, nf + j)),
                  pl.BlockSpec((None, CONV_W, tf), lambda j: (l, 0, j)),
                  pl.BlockSpec((None, 1, tf), lambda j: (l, 0, j)),
                  pl.BlockSpec((None, b, CONV_W - 1, tf), lambda j: (l, 0, 0, j))],
        out_specs=(pl.BlockSpec((m, tf), lambda j: (0, j)),
                   pl.BlockSpec((b, CONV_W - 1, tf), lambda j: (0, 0, j))),
        scratch_shapes=[pltpu.VMEM((tf // LANES, m, LANES), F32)] * 3,
        compiler_params=_cparams(("arbitrary",), V7X_VMEM_LIMIT),
        name="ffn_in_sample",
    )(a, w_in, w_in, conv_w, conv_b.reshape(conv_b.shape[0], 1, f), state)


def _alibi_slope(h, n_heads):
    return 2.0 ** (-8.0 * (h + 1) / n_heads)


def _swa_prompt_kernel(sink_ref, q_ref, kp_ref, kc_ref, vp_ref, vc_ref, o_ref, *, n_kv, group, hd):
    i = pl.program_id(0)
    n_heads = n_kv * group
    tq = q_ref.shape[0]
    t = lax.broadcasted_iota(jnp.int32, (tq, 2 * tq), 0)
    s = lax.broadcasted_iota(jnp.int32, (tq, 2 * tq), 1)
    dist = t + tq - s
    allowed = (dist >= 0) & (dist < WINDOW) & ((s >= tq) | (i > 0))
    sidx = s[0:1, :].astype(F32)
    trow = (t[:, 0:1] + tq).astype(F32)
    qscale = hd ** -0.5 * LOG2E
    nt = (((1,), (1,)), ((), ()))
    def scores(kv):
        ks = slice(kv * hd, (kv + 1) * hd)
        k2 = jnp.concatenate([kp_ref[:, ks], kc_ref[:, ks]], axis=0).astype(BF16)
        q4 = jnp.concatenate([(q_ref[:, h * hd:(h + 1) * hd] * qscale).astype(BF16)
                              for h in range(kv * group, (kv + 1) * group)], axis=0)
        return lax.dot_general(q4, k2, nt, preferred_element_type=F32)

    for kv0 in range(0, n_kv, SWA_WAVE):
        kvs = list(range(kv0, min(kv0 + SWA_WAVE, n_kv)))
        sc4s = [scores(kv) for kv in kvs]
        heads = [kv * group + g for kv in kvs for g in range(group)]
        slopes = [_alibi_slope(h, n_heads) * LOG2E for h in heads]
        scs = [jnp.where(allowed, sc4s[j // group][(j % group) * tq:(j % group + 1) * tq] + slopes[j] * sidx,
                         -jnp.inf) for j in range(len(heads))]
        sinks = [sink_ref[h] * LOG2E + slopes[j] * trow for j, h in enumerate(heads)]
        mxs = [jnp.maximum(jnp.max(scs[j], axis=-1, keepdims=True), sinks[j]) for j in range(len(heads))]
        es = [jnp.exp2(scs[j] - mxs[j]) for j in range(len(heads))]
        dens = [jnp.sum(es[j], axis=-1, keepdims=True) + jnp.exp2(sinks[j] - mxs[j])
                for j in range(len(heads))]
        es = [e.astype(BF16) for e in es]
        o4s = []
        for i, kv in enumerate(kvs):
            ks = slice(kv * hd, (kv + 1) * hd)
            v2 = jnp.concatenate([vp_ref[:, ks], vc_ref[:, ks]], axis=0).astype(BF16)
            o4s.append(jnp.dot(jnp.concatenate(es[i * group:(i + 1) * group], axis=0), v2,
                               preferred_element_type=F32))
        for j, h in enumerate(heads):
            o = o4s[j // group][(j % group) * tq:(j % group + 1) * tq]
            o_ref[:, h * hd:(h + 1) * hd] = (o / dens[j]).astype(o_ref.dtype)


def _swa_prompt(qkv, sinks, n_kv, hd):
    l = qkv.shape[0]
    n_heads = sinks.shape[0]
    group = n_heads // n_kv
    dq = n_heads * hd
    dk = n_kv * hd
    nb = l // WINDOW
    kblk = dq // dk
    return pl.pallas_call(
        functools.partial(_swa_prompt_kernel, n_kv=n_kv, group=group, hd=hd),
        out_shape=jax.ShapeDtypeStruct((l, dq), BF16),
        grid_spec=pltpu.PrefetchScalarGridSpec(
            num_scalar_prefetch=1,
            grid=(nb,),
            in_specs=[pl.BlockSpec((WINDOW, dq), lambda i, s: (i, 0)),
                      pl.BlockSpec((WINDOW, dk), lambda i, s: (jnp.maximum(i - 1, 0), kblk)),
                      pl.BlockSpec((WINDOW, dk), lambda i, s: (i, kblk)),
                      pl.BlockSpec((WINDOW, dk), lambda i, s: (jnp.maximum(i - 1, 0), kblk + 1)),
                      pl.BlockSpec((WINDOW, dk), lambda i, s: (i, kblk + 1))],
            out_specs=pl.BlockSpec((WINDOW, dq), lambda i, s: (i, 0)),
        ),
        compiler_params=_cparams(("arbitrary",), V7X_VMEM_LIMIT),
        name="swa_prompt",
    )(sinks, qkv, qkv, qkv, qkv, qkv)


def _swa_sample_bias(n_heads, n_kv, seq, cr, pad_keys):
    group = n_heads // n_kv
    r = np.arange(seq * n_heads)
    t, h = r // n_heads, r % n_heads
    slope = 2.0 ** (-8.0 * (h + 1) / n_heads)
    c = np.arange(cr * n_kv)
    key, kvc = c // n_kv, c % n_kv
    dist = cr + t[:, None] - key[None, :]
    ok = (kvc[None, :] == (h // group)[:, None]) & (dist >= 0) & (dist < WINDOW)
    cached = np.where(ok, -slope[:, None] * dist, -np.inf)
    j = np.arange(pad_keys)
    tn, kvn = j // n_kv, j % n_kv
    distn = t[:, None] - tn[None, :]
    okn = (kvn[None, :] == (h // group)[:, None]) & (distn >= 0) & (tn[None, :] < seq)
    new = np.where(okn, -slope[:, None] * distn, -np.inf)
    return jnp.asarray(np.concatenate([cached, new], axis=1) * LOG2E, dtype=F32)


def _swa_sample_kernel(q_ref, kc_ref, vc_ref, kn_ref, vn_ref, bias_ref, sink_ref, o_ref, *, hd, pad_keys):
    bs = q_ref.shape[0]
    nnew = kn_ref.shape[1]
    qscale = hd ** -0.5 * LOG2E
    bias = bias_ref[...]
    sink2 = sink_ref[...] * LOG2E
    zpad = jnp.zeros((pad_keys - nnew, hd), BF16)
    nt = (((1,), (1,)), ((), ()))
    keys = [jnp.concatenate([kc_ref[b].astype(BF16), kn_ref[b].astype(BF16), zpad], axis=0) for b in range(bs)]
    sc = [lax.dot_general((q_ref[b] * qscale).astype(BF16), keys[b], nt, preferred_element_type=F32) + bias
          for b in range(bs)]
    mx = [jnp.maximum(jnp.max(s, axis=-1, keepdims=True), sink2) for s in sc]
    es = [jnp.exp2(s - m) for s, m in zip(sc, mx)]
    den = [jnp.sum(e, axis=-1, keepdims=True) + jnp.exp2(sink2 - m) for e, m in zip(es, mx)]
    vals = [jnp.concatenate([vc_ref[b].astype(BF16), vn_ref[b].astype(BF16), zpad], axis=0) for b in range(bs)]
    outs = [jnp.dot(e.astype(BF16), v, preferred_element_type=F32) for e, v in zip(es, vals)]
    for b in range(bs):
        o_ref[b] = (outs[b] / den[b]).astype(o_ref.dtype)


def _swa_sample(q, k_new, v_new, cache_k, cache_v, sinks, n_kv, hd, seq):
    b, cr = cache_k.shape[0], cache_k.shape[1]
    n_heads = sinks.shape[0]
    rows = seq * n_heads
    pad_keys = LANES
    assert seq * n_kv <= pad_keys
    bias = _swa_sample_bias(n_heads, n_kv, seq, cr, pad_keys)
    ncol = bias.shape[1]
    bs = _pick(b, (8, 4, 2, 1))
    blk = lambda r: pl.BlockSpec((bs, r, hd), lambda i: (i, 0, 0))
    o = pl.pallas_call(
        functools.partial(_swa_sample_kernel, hd=hd, pad_keys=pad_keys),
        out_shape=jax.ShapeDtypeStruct((b, rows, hd), BF16),
        grid=(b // bs,),
        in_specs=[blk(rows), blk(cr * n_kv), blk(cr * n_kv), blk(seq * n_kv), blk(seq * n_kv),
                  pl.BlockSpec((rows, ncol), lambda i: (0, 0)),
                  pl.BlockSpec((rows, 1), lambda i: (0, 0))],
        out_specs=blk(rows),
        compiler_params=_cparams(("arbitrary",), V7X_VMEM_LIMIT),
        name="swa_sample",
    )(q.reshape(b, rows, hd), cache_k.reshape(b, cr * n_kv, hd), cache_v.reshape(b, cr * n_kv, hd),
      k_new.reshape(b, seq * n_kv, hd), v_new.reshape(b, seq * n_kv, hd), bias,
      jnp.tile(sinks, seq).reshape(rows, 1))
    return o.reshape(b * seq, n_heads * hd)


def _lower_bound(p_ref):
    p0 = p_ref[0:1, :]
    p1 = p_ref[1:2, :]
    mx = jnp.maximum(p0, p1)
    e0 = jnp.exp(p0 - mx)
    e1 = jnp.exp(p1 - mx)
    return e1 / (e0 + e1)


def _hgrn_gates(fr, lb):
    e = jnp.exp(-jnp.abs(fr))
    log_sig = jnp.minimum(fr, 0.0) - jnp.log1p(e)
    logf = _logaddexp(jnp.log(lb), jnp.log1p(-lb) + log_sig)
    r = 1.0 / (1.0 + e)
    k = (1.0 - lb) * jnp.where(fr >= 0.0, e * r, r)
    return logf, k


def _group_norm_gate(o, gr, ng):
    o = o * lax.rsqrt(jnp.mean(o * o, axis=-1, keepdims=True) + EPS) * ng
    return o * _silu(gr)


SMALL = 8

def _level_tables(c):
    g = min(c, 128)
    t = np.arange(g)[:, None]
    s = np.arange(g)[None, :]
    hb = np.floor(np.log2(np.maximum(t ^ s, 1))).astype(np.int32)
    same = (t // SMALL) == (s // SMALL)
    near = np.where(same & (s < t), hb, np.where(s == t, int(math.log2(SMALL)), -1))
    i = np.arange(c // 2)[:, None]
    j = np.arange(c // 2)[None, :]
    far = np.where(i == j, -1, np.floor(np.log2(np.maximum(i ^ j, 1)))).astype(np.int32)
    return jnp.asarray(near, dtype=jnp.int32), jnp.asarray(far, dtype=jnp.int32)


def _group_ref_rows(p, m, row):
    c, w = p.shape
    if m == 1:
        return jnp.where((row & 1) == 1, pltpu.roll(p, 1, 0), p)
    if m == 2:
        r4 = row & 3
        return jnp.where(r4 == 0, pltpu.roll(p, c - 1, 0),
                         jnp.where(r4 == 1, p,
                                   jnp.where(r4 == 2, pltpu.roll(p, 1, 0), pltpu.roll(p, 2, 0))))
    p3 = p.reshape(c // 8, 8, w)
    return jnp.broadcast_to(p3[:, 3:4, :], (c // 8, 8, w)).reshape(c, w)


def _halves(x, m):
    c = x.shape[0]
    ev = [x[r:r + m] for r in range(0, c, 2 * m)]
    od = [x[r + m:r + 2 * m] for r in range(0, c, 2 * m)]
    cat = lambda xs: xs[0] if len(xs) == 1 else jnp.concatenate(xs, axis=0)
    return cat(ev), cat(od)


def _interleave(ev, od, m):
    parts = []
    for r in range(0, ev.shape[0], m):
        parts += [ev[r:r + m], od[r:r + m]]
    return jnp.concatenate(parts, axis=0)


def _hgrn_chunks(chunks, st, near, far):
    c, w = chunks[0][0].shape
    nlev = int(math.log2(c))
    nsmall = int(math.log2(SMALL))
    g = near.shape[0]
    row = lax.broadcasted_iota(jnp.int32, (c, w), 0)
    nt = (((1,), (1,)), ((), ()))
    dot_nt = lambda x, y: lax.dot_general(x, y, nt, preferred_element_type=F32)
    dot_nn = lambda x, y: jnp.dot(x, y, preferred_element_type=F32)

    near_ops, far_ops, tails = [], [], []
    for q, k, v, logf in chunks:
        p = logf
        qs, ks = [q.astype(BF16)], [k.astype(BF16)]
        for l in range(nsmall):
            gref = _group_ref_rows(p, 1 << l, row)
            odd = ((row >> l) & 1) == 1
            e = jnp.exp(jnp.where(odd, p, gref - p))
            qs.append((q * e).astype(BF16))
            ks.append((k * e).astype(BF16))
            p = p + jnp.where(odd, gref, 0.0)
        near_ops.append((qs, ks, v.astype(BF16)))
        lev = []
        for l in range(nsmall, nlev):
            m = 1 << l
            pe, po = _halves(p, m)
            tot = jnp.concatenate([jnp.broadcast_to(pe[r + m - 1:r + m], (m, w))
                                   for r in range(0, c // 2, m)], axis=0)
            q_od = _halves(q, m)[1]
            k_ev = _halves(k, m)[0]
            v_ev = _halves(v, m)[0].astype(BF16)
            lev.append(((q_od * jnp.exp(po)).astype(BF16), (k_ev * jnp.exp(tot - pe)).astype(BF16), v_ev))
            p = _interleave(pe, po + tot, m)
        far_ops.append(lev)
        b = p
        b_end = b[c - 1:c, :]
        tails.append(((q * jnp.exp(b)).astype(BF16), (k * jnp.exp(b_end - b)).astype(BF16),
                      jnp.exp(b_end)))

    near_sc = [[[dot_nt(qs[j][r:r + g], ks[j][r:r + g]) for j in range(nsmall + 1)]
                for r in range(0, c, g)] for qs, ks, _ in near_ops]
    far_sc = [[dot_nt(qt, kt) for qt, kt, _ in lev] for lev in far_ops]
    upds = [lax.dot_general(vb, k_end, (((0,), (0,)), ((), ())), preferred_element_type=F32)
            for (_, _, vb), (_, k_end, _) in zip(near_ops, tails)]

    near_a = []
    for sc_chunk in near_sc:
        tiles = []
        for sc in sc_chunk:
            a = jnp.where(near == nsmall, sc[0], 0.0)
            for l in range(nsmall):
                a = jnp.where(near == l, sc[l + 1], a)
            tiles.append(a.astype(BF16))
        near_a.append(tiles)
    far_a = [[(jnp.where(far < nsmall + j, a, 0.0) if (1 << (nsmall + j)) < c // 2 else a).astype(BF16)
              for j, a in enumerate(lev)] for lev in far_sc]
    states = [st]
    for (_, _, dec), upd in zip(tails, upds):
        states.append(states[-1] * dec + upd)

    near_o = [[dot_nn(a, vb[i * g:(i + 1) * g]) for i, a in enumerate(tiles)]
              for tiles, (_, _, vb) in zip(near_a, near_ops)]
    far_o = [[dot_nn(a, ops[2]) for a, ops in zip(avs, lev)] for avs, lev in zip(far_a, far_ops)]
    inter_o = [dot_nt(qe, s.astype(BF16)) for (qe, _, _), s in zip(tails, states[:-1])]

    outs = []
    for no, fo, io in zip(near_o, far_o, inter_o):
        o = (no[0] if len(no) == 1 else jnp.concatenate(no, axis=0)) + io
        for j, x in enumerate(fo):
            m = 1 << (nsmall + j)
            o_ev, o_od = _halves(o, m)
            o = _interleave(o_ev, o_od + x, m)
        outs.append(o)
    return outs, states[-1]


def _hgrn_prompt_kernel(a_ref, wq_ref, wf_ref, wi_ref, wg_ref, p_ref, ng_ref, near_ref, far_ref,
                        o_ref, s_ref, wbf, st, *, chunk):
    i = pl.program_id(1)
    dk = wq_ref.shape[1]

    @pl.when(i == 0)
    def _():
        wbf[:, 0 * dk:1 * dk] = wq_ref[...].astype(BF16)
        wbf[:, 1 * dk:2 * dk] = wf_ref[...].astype(BF16)
        wbf[:, 2 * dk:3 * dk] = wi_ref[...].astype(BF16)
        wbf[:, 3 * dk:4 * dk] = wg_ref[...].astype(BF16)
        st[...] = jnp.zeros_like(st)

    tm = a_ref.shape[0]
    projs = [jnp.dot(a_ref[c0:c0 + chunk, :], wbf[...], preferred_element_type=F32)
             for c0 in range(0, tm, chunk)]
    lb = _lower_bound(p_ref)
    ng = ng_ref[...]
    chunks = []
    for proj in projs:
        q = _silu(proj[:, 0 * dk:1 * dk])
        logf, k = _hgrn_gates(proj[:, 1 * dk:2 * dk], lb)
        chunks.append((q, k, proj[:, 2 * dk:3 * dk], logf))
    outs, s_t = _hgrn_chunks(chunks, st[...], near_ref[...], far_ref[...])
    for j, (o, proj) in enumerate(zip(outs, projs)):
        o_ref[j * chunk:(j + 1) * chunk, :] = _group_norm_gate(o, proj[:, 3 * dk:4 * dk], ng).astype(o_ref.dtype)
    st[...] = s_t
    s_ref[0, 0] = s_t.T


def _hgrn_prompt(a, w_in, lower_bounds, norm_g, n_heads, dk):
    m, d = a.shape
    chunk = min(HG_CHUNK, m)
    tm = _pick(m, (HG_ROWS, 512, 256, 128, 64, 32, 16, 8))
    tm = max(tm, chunk)
    near, far = _level_tables(chunk)
    w_spec = lambda blk: pl.BlockSpec((d, dk), lambda h, i: (0, blk * n_heads + h))
    o, s = pl.pallas_call(
        functools.partial(_hgrn_prompt_kernel, chunk=chunk),
        out_shape=(jax.ShapeDtypeStruct((m, n_heads * dk), BF16),
                   jax.ShapeDtypeStruct((1, n_heads, dk, dk), F32)),
        grid=(n_heads, m // tm),
        in_specs=[pl.BlockSpec((tm, d), lambda h, i: (i, 0)),
                  w_spec(0), w_spec(1), w_spec(2), w_spec(3),
                  pl.BlockSpec((lower_bounds.shape[0], dk), lambda h, i: (0, h)),
                  pl.BlockSpec((1, dk), lambda h, i: (0, h)),
                  pl.BlockSpec(near.shape, lambda h, i: (0, 0)),
                  pl.BlockSpec(far.shape, lambda h, i: (0, 0))],
        out_specs=(pl.BlockSpec((tm, dk), lambda h, i: (i, h)),
                   pl.BlockSpec((1, 1, dk, dk), lambda h, i: (0, h, 0, 0))),
        scratch_shapes=[pltpu.VMEM((d, 4 * dk), BF16), pltpu.VMEM((dk, dk), F32)],
        compiler_params=_cparams(("arbitrary", "arbitrary"), V7X_VMEM_LIMIT),
        name="hgrn_prompt",
    )(a, w_in, w_in, w_in, w_in, lower_bounds, norm_g.reshape(1, -1), near, far)
    return o, s


def _hgrn_sample_kernel(qr_ref, fr_ref, ir_ref, gr_ref, p_ref, ng_ref, s0_ref, o_ref, s_ref, *, seq):
    bs = s0_ref.shape[0]
    rows = bs * seq
    lb = _lower_bound(p_ref)
    q = _silu(qr_ref[...])
    logf, k = _hgrn_gates(fr_ref[...], lb)
    v = ir_ref[...]
    row = lax.broadcasted_iota(jnp.int32, q.shape, 0)
    t = row % seq
    b = logf
    d = 1
    while d < seq:
        b = b + jnp.where(t >= d, pltpu.roll(b, d, 0), 0.0)
        d *= 2
    o = jnp.sum(q * k, axis=-1, keepdims=True) * v
    for d in range(1, seq):
        ok = t >= d
        arg = jnp.where(ok, b - pltpu.roll(b, d, 0), 0.0)
        w = jnp.sum(q * pltpu.roll(k, d, 0) * jnp.exp(arg), axis=-1, keepdims=True)
        o = o + jnp.where(ok, w, 0.0) * pltpu.roll(v, d, 0)
    qe = (q * jnp.exp(b)).astype(BF16)
    b_end = b
    d = 1
    while d < seq:
        b_end = jnp.where(t < seq - d, pltpu.roll(b_end, rows - d, 0), b_end)
        d *= 2
    k_end = k * jnp.exp(b_end - b)
    k_end_t = k_end.T
    dec_t = jnp.exp(b_end).T
    lane = lax.broadcasted_iota(jnp.int32, k_end_t.shape, 1)
    vb = v.astype(BF16)
    o_inter = []
    for s in range(bs):
        s0 = s0_ref[s, 0]
        in_seq = (lane >= s * seq) & (lane < (s + 1) * seq)
        upd = jnp.dot(jnp.where(in_seq, k_end_t, 0.0).astype(BF16), vb, preferred_element_type=F32)
        s_ref[s, 0] = s0 * dec_t[:, s * seq:s * seq + 1] + upd
        if s % (8 // seq) == 0:
            r0 = s * seq
            parts = []
        oi = jnp.dot(qe[r0:r0 + 8, :], s0.astype(BF16), preferred_element_type=F32)
        parts.append(oi)
        if s % (8 // seq) == (8 // seq) - 1:
            r8 = lax.broadcasted_iota(jnp.int32, oi.shape, 0) // seq
            acc = parts[0]
            for j in range(1, len(parts)):
                acc = jnp.where(r8 == j, parts[j], acc)
            o_inter.append(acc)
    o = o + jnp.concatenate(o_inter, axis=0)
    o_ref[...] = _group_norm_gate(o, gr_ref[...], ng_ref[...]).astype(o_ref.dtype)


def _hgrn_sample(proj, lower_bounds, norm_g, state, seq):
    b, n_heads, dk, dv = state.shape
    m = proj.shape[0]
    bs = _pick(b, (64, 32, 16, 8, 4, 2))
    rows = bs * seq
    col = lambda blk: pl.BlockSpec((rows, dk), lambda h, i: (i, blk * n_heads + h))
    o, s = pl.pallas_call(
        functools.partial(_hgrn_sample_kernel, seq=seq),
        out_shape=(jax.ShapeDtypeStruct((m, n_heads * dv), BF16),
                   jax.ShapeDtypeStruct(state.shape, F32)),
        grid=(n_heads, b // bs),
        in_specs=[col(0), col(1), col(2), col(3),
                  pl.BlockSpec((lower_bounds.shape[0], dk), lambda h, i: (0, h)),
                  pl.BlockSpec((1, dv), lambda h, i: (0, h)),
                  pl.BlockSpec((bs, 1, dk, dv), lambda h, i: (i, h, 0, 0))],
        out_specs=(pl.BlockSpec((rows, dv), lambda h, i: (i, h)),
                   pl.BlockSpec((bs, 1, dk, dv), lambda h, i: (i, h, 0, 0))),
        compiler_params=_cparams(("arbitrary", "arbitrary"), V7X_VMEM_LIMIT),
        name="hgrn_sample",
    )(proj, proj, proj, proj, lower_bounds, norm_g.reshape(1, -1), state)
    return o, s


def kernel(x_prompt, x_sample, cache_swa_k, cache_swa_v, state_hgrn, state_ffn_conv, c_prompt, c_sample,
           norm1_g, norm2_g, w_ada, b_ada, attn_w_qkv, attn_w_o, attn_sinks,
           hgrn_w_in, hgrn_lower_bounds, hgrn_norm_g, hgrn_w_o,
           ffn_w_in, ffn_conv_w, ffn_conv_b, ffn_w_out, final_norm_g):
    bp, lp, d = x_prompt.shape
    bsmp, ls, _ = x_sample.shape
    assert bp == 1, "the prompt kernels carry one sequence"
    depth = w_ada.shape[0]
    n_kv, hd = cache_swa_k.shape[2], cache_swa_k.shape[3]
    n_heads = attn_sinks.shape[0]
    dq = n_heads * hd
    dkv = n_kv * hd
    hg_heads, hg_dk = state_hgrn.shape[1], state_hgrn.shape[2]
    keep = min(WINDOW, lp)

    ns = bsmp * ls
    assert ns % 8 == 0
    c_all = jnp.concatenate([jnp.repeat(c_sample, ls, axis=0), c_prompt, jnp.zeros((7, d), F32)], axis=0)
    mod = _ada(c_all, w_ada, b_ada)
    pk = dict(sample=False, n_sample=ns)
    sk = dict(sample=True, n_sample=ns)

    xp = x_prompt.reshape(lp, d)
    xs = x_sample.reshape(ns, d)
    conv_p, conv_s = [], []
    outs = {}
    w_out_bf = ffn_w_out.astype(BF16)
    hp = _modulate(xp, norm1_g, mod, 0, 0, 1, False, ns)
    hs = _modulate(xs, norm1_g, mod, 0, 0, 1, True, ns)
    for l in range(depth):
        if l % 2 == 0:
            qkv_p, qkv_s = _matmul2(hp, hs, attn_w_qkv, tm=1024, tn=1024, name="qkv")
            op = _swa_prompt(qkv_p, attn_sinks, n_kv, hd)
            outs['swa_k_prompt'] = qkv_p[lp - keep:, dq:dq + dkv].reshape(bp, keep, n_kv, hd)
            outs['swa_v_prompt'] = qkv_p[lp - keep:, dq + dkv:].reshape(bp, keep, n_kv, hd)
            outs['swa_k_sample'] = qkv_s[:, dq:dq + dkv].reshape(bsmp, ls, n_kv, hd)
            outs['swa_v_sample'] = qkv_s[:, dq + dkv:].reshape(bsmp, ls, n_kv, hd)
            os_ = _swa_sample(qkv_s[:, :dq], outs['swa_k_sample'], outs['swa_v_sample'],
                               cache_swa_k, cache_swa_v, attn_sinks, n_kv, hd, ls)
            w_o = attn_w_o
        else:
            op, sp = _hgrn_prompt(hp, hgrn_w_in, hgrn_lower_bounds, hgrn_norm_g, hg_heads, hg_dk)
            proj_s = _matmul(hs, hgrn_w_in, name="hgrn_in_sample")
            os_, ss = _hgrn_sample(proj_s, hgrn_lower_bounds, hgrn_norm_g, state_hgrn, ls)
            outs['hgrn_state_prompt'] = sp
            outs['hgrn_state_sample'] = ss
            w_o = hgrn_w_o
        xp, hp = _mm_rows(op, w_o, xp, mod, gate=(l, 2), norm_g=norm2_g, norm_layer=l, next_mod=(l, 3, 4),
                          tm=512, name="mix_out_prompt", **pk)
        xs, hs = _mm_rows(os_, w_o, xs, mod, gate=(l, 2), norm_g=norm2_g, norm_layer=l, next_mod=(l, 3, 4),
                          tm=512, name="mix_out_sample", **sk)
        actp, bufp = _ffn_in_prompt(hp, ffn_w_in, ffn_conv_w, ffn_conv_b, l)
        acts, bufs = _ffn_in_sample(hs, ffn_w_in, ffn_conv_w, ffn_conv_b, l, state_ffn_conv, ls)
        conv_p.append(bufp[None])
        conv_s.append(bufs)
        if l + 1 < depth:
            nxt = dict(norm_g=norm1_g, norm_layer=l + 1, next_mod=(l + 1, 0, 1))
        else:
            nxt = dict(norm_g=final_norm_g, norm_layer=0, next_mod=None, keep_x=False, out_dtype=F32)
        xp, hp = _mm_rows(actp, w_out_bf, xp, mod, gate=(l, 5), w_layer=l, tm=256, name="ffn_out_prompt",
                          **nxt, **pk)
        xs, hs = _mm_rows(acts, w_out_bf, xs, mod, gate=(l, 5), w_layer=l, tm=256, name="ffn_out_sample",
                          **nxt, **sk)
    y_prompt = hp.reshape(bp, lp, d)
    y_sample = hs.reshape(bsmp, ls, d)
    return (y_prompt, y_sample,
            outs['swa_k_prompt'], outs['swa_v_prompt'], outs['swa_k_sample'], outs['swa_v_sample'],
            outs['hgrn_state_prompt'], outs['hgrn_state_sample'],
            jnp.stack(conv_p), jnp.stack(conv_s))
```

```python
import functools
import math

import numpy as np
import jax
import jax.numpy as jnp
from jax import lax
from jax.experimental import pallas as pl
from jax.experimental.pallas import tpu as pltpu

F32 = jnp.float32
BF16 = jnp.bfloat16

EPS = 1e-6
LOG2E = math.log2(math.e)
WINDOW = 128
CONV_W = 3
LANES = 128
HG_CHUNK = 256
HG_ROWS = 1024
SWA_WAVE = 8
V7X_VMEM_LIMIT = 56 * 1024 * 1024


def _cparams(sem, vmem=None):
    return pltpu.CompilerParams(dimension_semantics=sem, vmem_limit_bytes=vmem)


def _pick(n, cands):
    for c in cands:
        if n % c == 0:
            return c
    return n


def _silu(x):
    return x * jax.nn.sigmoid(x)


def _logaddexp(a, b):
    return jnp.maximum(a, b) + jnp.log1p(jnp.exp(-jnp.abs(a - b)))


def _ada_kernel(c_ref, w_ref, b_ref, o_ref, act):
    @pl.when((pl.program_id(0) == 0) & (pl.program_id(1) == 0))
    def _():
        act[...] = _silu(c_ref[...]).astype(BF16)

    w = w_ref[0].astype(BF16)
    o_ref[0] = jnp.dot(act[...], w, preferred_element_type=F32) + b_ref[0]


def _ada(c_all, w_ada, b_ada):
    depth, d, n = w_ada.shape
    r = c_all.shape[0]
    tn = _pick(n, (1024, 512, 256, 128))
    return pl.pallas_call(
        _ada_kernel,
        out_shape=jax.ShapeDtypeStruct((depth, r, n), F32),
        grid=(depth, n // tn),
        in_specs=[pl.BlockSpec((r, d), lambda l, j: (0, 0)),
                  pl.BlockSpec((1, d, tn), lambda l, j: (l, 0, j)),
                  pl.BlockSpec((1, 1, tn), lambda l, j: (l, 0, j))],
        out_specs=pl.BlockSpec((1, r, tn), lambda l, j: (l, 0, j)),
        scratch_shapes=[pltpu.VMEM((r, d), BF16)],
        compiler_params=_cparams(("arbitrary", "arbitrary"), V7X_VMEM_LIMIT),
        name="ada",
    )(c_all, w_ada, b_ada.reshape(depth, 1, n))


def _mod_rows(sample, tm, n_sample):
    if sample:
        return tm, (lambda i: i)
    return 8, (lambda i: n_sample // 8)


def _modulate_kernel(x_ref, g_ref, sc_ref, sh_ref, o_ref, *, sample):
    x = x_ref[...]
    ms = jnp.mean(x * x, axis=-1, keepdims=True)
    y = x * lax.rsqrt(ms + EPS) * g_ref[...]
    sc = sc_ref[...] if sample else sc_ref[0:1, :]
    sh = sh_ref[...] if sample else sh_ref[0:1, :]
    o_ref[...] = (y * (1.0 + sc) + sh).astype(o_ref.dtype)


def _modulate(x, g_all, mod, l, shift_col, scale_col, sample, n_sample):
    m, d = x.shape
    tm = _pick(m, (512, 256, 128, 64, 32, 16, 8))
    rows, rmap = _mod_rows(sample, tm, n_sample)
    mspec = lambda col: pl.BlockSpec((None, rows, d), lambda i: (l, rmap(i), col))
    return pl.pallas_call(
        functools.partial(_modulate_kernel, sample=sample),
        out_shape=jax.ShapeDtypeStruct((m, d), BF16),
        grid=(m // tm,),
        in_specs=[pl.BlockSpec((tm, d), lambda i: (i, 0)),
                  pl.BlockSpec((None, 1, d), lambda i: (l, 0, 0)),
                  mspec(scale_col), mspec(shift_col)],
        out_specs=pl.BlockSpec((tm, d), lambda i: (i, 0)),
        compiler_params=_cparams(("arbitrary",), V7X_VMEM_LIMIT),
        name="modulate",
    )(x, g_all.reshape(g_all.shape[0], 1, d), mod, mod)


def _mm_kernel(a_ref, w_ref, o_ref, wbf):
    @pl.when(pl.program_id(1) == 0)
    def _():
        wbf[...] = w_ref[...].astype(BF16)

    o_ref[...] = jnp.dot(a_ref[...], wbf[...], preferred_element_type=F32)


def _mm2_kernel(a_ref, b_ref, w_ref, oa_ref, ob_ref, wbf, *, n_a):
    i = pl.program_id(1)

    @pl.when(i == 0)
    def _():
        wbf[...] = w_ref[...].astype(BF16)

    @pl.when(i < n_a)
    def _():
        oa_ref[...] = jnp.dot(a_ref[...], wbf[...], preferred_element_type=F32)

    @pl.when(i == n_a)
    def _():
        ob_ref[...] = jnp.dot(b_ref[...], wbf[...], preferred_element_type=F32)


def _matmul2(a, b, w, *, tm, tn, name):
    ma, k = a.shape
    mb = b.shape[0]
    n = w.shape[-1]
    tm, tn = min(tm, ma), min(tn, n)
    assert ma % tm == 0 and n % tn == 0
    n_a = ma // tm
    last = n_a - 1
    return pl.pallas_call(
        functools.partial(_mm2_kernel, n_a=n_a),
        out_shape=(jax.ShapeDtypeStruct((ma, n), F32), jax.ShapeDtypeStruct((mb, n), F32)),
        grid=(n // tn, n_a + 1),
        in_specs=[pl.BlockSpec((tm, k), lambda j, i: (jnp.minimum(i, last), 0)),
                  pl.BlockSpec((mb, k), lambda j, i: (0, 0)),
                  pl.BlockSpec((k, tn), lambda j, i: (0, j))],
        out_specs=(pl.BlockSpec((tm, tn), lambda j, i: (jnp.minimum(i, last), j)),
                   pl.BlockSpec((mb, tn), lambda j, i: (0, j))),
        scratch_shapes=[pltpu.VMEM((k, tn), BF16)],
        compiler_params=_cparams(("arbitrary", "arbitrary"), V7X_VMEM_LIMIT),
        name=name,
    )(a, b, w)


def _matmul(a, w, *, tm=None, tn=None, name="matmul"):
    m, k = a.shape
    n = w.shape[-1]
    tm = min(tm, m) if tm else _pick(m, (1024, 512, 256, 128, 64, 32, 16, 8))
    tn = min(tn, n) if tn else _pick(n, (512, 256, 128))
    assert m % tm == 0 and n % tn == 0
    return pl.pallas_call(
        _mm_kernel,
        out_shape=jax.ShapeDtypeStruct((m, n), F32),
        grid=(n // tn, m // tm),
        in_specs=[pl.BlockSpec((tm, k), lambda j, i: (i, 0)),
                  pl.BlockSpec((k, tn), lambda j, i: (0, j))],
        out_specs=pl.BlockSpec((tm, tn), lambda j, i: (i, j)),
        scratch_shapes=[pltpu.VMEM((k, tn), BF16)],
        compiler_params=_cparams(("arbitrary", "arbitrary"), V7X_VMEM_LIMIT),
        name=name,
    )(a, w)


def _mm_rows_kernel(*refs, cast_w, sample, modulate, keep_x):
    it = iter(refs)
    a_ref, w_ref, x_ref, gate_ref, g_ref = (next(it) for _ in range(5))
    sc_ref, sh_ref = (next(it), next(it)) if modulate else (None, None)
    xo_ref = next(it) if keep_x else None
    h_ref = next(it)
    if cast_w:
        wbf = next(it)

        @pl.when(pl.program_id(0) == 0)
        def _():
            wbf[...] = w_ref[...].astype(BF16)

        w = wbf[...]
    else:
        w = w_ref[...]
    rows = (lambda r: r[...]) if sample else (lambda r: r[0:1, :])
    x = x_ref[...] + rows(gate_ref) * jnp.dot(a_ref[...], w, preferred_element_type=F32)
    if keep_x:
        xo_ref[...] = x
    y = x * lax.rsqrt(jnp.mean(x * x, axis=-1, keepdims=True) + EPS) * g_ref[...]
    if modulate:
        y = y * (1.0 + rows(sc_ref)) + rows(sh_ref)
    h_ref[...] = y.astype(h_ref.dtype)


def _mm_rows(a, w, x, mod, *, gate, norm_g, norm_layer, next_mod=None, w_layer=None, sample, n_sample,
             tm, keep_x=True, out_dtype=None, name):
    m, k = a.shape
    n = w.shape[-1]
    tm = min(tm, m)
    assert m % tm == 0
    cast_w = w.dtype != BF16
    rows, rmap = _mod_rows(sample, tm, n_sample)
    once = pl.Buffered(1)
    if w_layer is None:
        w_spec = pl.BlockSpec((k, n), lambda i: (0, 0), pipeline_mode=once)
    else:
        w_spec = pl.BlockSpec((None, k, n), lambda i: (w_layer, 0, 0), pipeline_mode=once)
    mspec = lambda layer, col: pl.BlockSpec((None, rows, n), lambda i: (layer, rmap(i), col))
    in_specs = [pl.BlockSpec((tm, k), lambda i: (i, 0)), w_spec,
                pl.BlockSpec((tm, n), lambda i: (i, 0)),
                mspec(*gate),
                pl.BlockSpec((None, 1, n), lambda i: (norm_layer, 0, 0))]
    args = [a, w, x, mod, norm_g.reshape(-1, 1, n)]
    if next_mod is not None:
        in_specs += [mspec(next_mod[0], next_mod[2]), mspec(next_mod[0], next_mod[1])]
        args += [mod, mod]
    out_shape, out_specs = [], []
    if keep_x:
        out_shape.append(jax.ShapeDtypeStruct((m, n), F32))
        out_specs.append(pl.BlockSpec((tm, n), lambda i: (i, 0)))
    out_shape.append(jax.ShapeDtypeStruct((m, n), out_dtype or BF16))
    out_specs.append(pl.BlockSpec((tm, n), lambda i: (i, 0)))
    res = pl.pallas_call(
        functools.partial(_mm_rows_kernel, cast_w=cast_w, sample=sample, modulate=next_mod is not None,
                          keep_x=keep_x),
        out_shape=tuple(out_shape),
        grid=(m // tm,),
        in_specs=in_specs,
        out_specs=tuple(out_specs),
        scratch_shapes=[pltpu.VMEM((k, n), BF16)] if cast_w else [],
        compiler_params=_cparams(("arbitrary",), V7X_VMEM_LIMIT),
        name=name,
    )(*args)
    return res if keep_x else (None, res[0])


def _gelu_exact(a):
    return 0.5 * a * (1.0 + lax.erf(a * (1.0 / math.sqrt(2.0))))


def _ffn_in_prompt_kernel(a_ref, wu_ref, wg_ref, cw_ref, cb_ref, act_ref, tail_ref, wbf, carry):
    i = pl.program_id(1)
    tf = wu_ref.shape[1]

    @pl.when(i == 0)
    def _():
        wbf[:, :tf] = wu_ref[...].astype(BF16)
        wbf[:, tf:] = wg_ref[...].astype(BF16)
        carry[...] = jnp.zeros_like(carry)

    ug = jnp.dot(a_ref[...], wbf[...], preferred_element_type=F32)
    u = ug[:, :tf]
    g = ug[:, tf:]
    tm = u.shape[0]
    row = lax.broadcasted_iota(jnp.int32, u.shape, 0)
    c = carry[...]
    u1 = jnp.where(row == 0, c[7:8, :], pltpu.roll(u, 1, 0))
    u2 = jnp.where(row == 0, c[6:7, :], jnp.where(row == 1, c[7:8, :], pltpu.roll(u, 2, 0)))
    cw = cw_ref[...]
    conv = cw[0:1, :] * u2 + cw[1:2, :] * u1 + cw[2:3, :] * u + cb_ref[...]
    act_ref[...] = (_gelu_exact(conv) * g).astype(act_ref.dtype)
    carry[...] = u[tm - 8:, :]
    tail_ref[...] = u[tm - 8:, :]


def _ffn_in_sample_kernel(a_ref, wu_ref, wg_ref, cw_ref, cb_ref, buf_ref, act_ref, nbuf_ref,
                          us, gs, acts, *, seq):
    a = a_ref[...]
    b = buf_ref.shape[0]
    u_all = jnp.dot(a, wu_ref[...].astype(BF16), preferred_element_type=F32)
    g_all = jnp.dot(a, wg_ref[...].astype(BF16), preferred_element_type=F32)
    cw = cw_ref[...]
    cb = cb_ref[...]
    for c in range(us.shape[0]):
        cs = slice(c * LANES, (c + 1) * LANES)
        us[c] = u_all[:, cs]
        gs[c] = g_all[:, cs]
        prev2, prev1 = buf_ref[:, 0, cs], buf_ref[:, 1, cs]
        for t in range(seq):
            u = us[c, pl.ds(t, b, stride=seq), :]
            conv = cw[0:1, cs] * prev2 + cw[1:2, cs] * prev1 + cw[2:3, cs] * u + cb[:, cs]
            acts[c, pl.ds(t, b, stride=seq), :] = _gelu_exact(conv) * gs[c, pl.ds(t, b, stride=seq), :]
            prev2, prev1 = prev1, u
        act_ref[:, cs] = acts[c].astype(act_ref.dtype)
        nbuf_ref[:, 0, cs] = prev2
        nbuf_ref[:, 1, cs] = prev1


def _ffn_in_prompt(a, w_in, conv_w, conv_b, l):
    m, d = a.shape
    f = w_in.shape[-1] // 2
    tf = _pick(f, (512, 256, 128))
    tm = _pick(m, (1024, 512, 256, 128, 64, 32, 16, 8))
    nf = f // tf
    act, tail = pl.pallas_call(
        _ffn_in_prompt_kernel,
        out_shape=(jax.ShapeDtypeStruct((m, f), BF16), jax.ShapeDtypeStruct((8, f), F32)),
        grid=(nf, m // tm),
        in_specs=[pl.BlockSpec((tm, d), lambda j, i: (i, 0)),
                  pl.BlockSpec((None, d, tf), lambda j, i: (l, 0, j)),
                  pl.BlockSpec((None, d, tf), lambda j, i: (l, 0, nf + j)),
                  pl.BlockSpec((None, CONV_W, tf), lambda j, i: (l, 0, j)),
                  pl.BlockSpec((None, 1, tf), lambda j, i: (l, 0, j))],
        out_specs=(pl.BlockSpec((tm, tf), lambda j, i: (i, j)),
                   pl.BlockSpec((8, tf), lambda j, i: (0, j))),
        scratch_shapes=[pltpu.VMEM((d, 2 * tf), BF16), pltpu.VMEM((8, tf), F32)],
        compiler_params=_cparams(("arbitrary", "arbitrary"), V7X_VMEM_LIMIT),
        name="ffn_in_prompt",
    )(a, w_in, w_in, conv_w, conv_b.reshape(conv_b.shape[0], 1, f))
    return act, tail[6:8]


def _ffn_in_sample(a, w_in, conv_w, conv_b, l, state, seq):
    m, d = a.shape
    f = w_in.shape[-1] // 2
    b = state.shape[1]
    tf = _pick(f, (512, 256, 128))
    nf = f // tf
    return pl.pallas_call(
        functools.partial(_ffn_in_sample_kernel, seq=seq),
        out_shape=(jax.ShapeDtypeStruct((m, f), BF16), jax.ShapeDtypeStruct((b, CONV_W - 1, f), F32)),
        grid=(nf,),
        in_specs=[pl.BlockSpec((m, d), lambda j: (0, 0)),
                  pl.BlockSpec((None, d, tf), lambda j: (l, 0, j)),
                  pl.BlockSpec((None, d, tf), lambda j: (l, 0, nf + j)),
                  pl.BlockSpec((None, CONV_W, tf), lambda j: (l, 0, j)),
                  pl.BlockSpec((None, 1, tf), lambda j: (l, 0, j)),
                  pl.BlockSpec((None, b, CONV_W - 1, tf), lambda j: (l, 0, 0, j))],
        out_specs=(pl.BlockSpec((m, tf), lambda j: (0, j)),
                   pl.BlockSpec((b, CONV_W - 1, tf), lambda j: (0, 0, j))),
        scratch_shapes=[pltpu.VMEM((tf // LANES, m, LANES), F32)] * 3,
        compiler_params=_cparams(("arbitrary",), V7X_VMEM_LIMIT),
        name="ffn_in_sample",
    )(a, w_in, w_in, conv_w, conv_b.reshape(conv_b.shape[0], 1, f), state)


def _alibi_slope(h, n_heads):
    return 2.0 ** (-8.0 * (h + 1) / n_heads)


def _swa_prompt_kernel(sink_ref, q_ref, kp_ref, kc_ref, vp_ref, vc_ref, o_ref, *, n_kv, group, hd):
    i = pl.program_id(0)
    n_heads = n_kv * group
    tq = q_ref.shape[0]
    t = lax.broadcasted_iota(jnp.int32, (tq, 2 * tq), 0)
    s = lax.broadcasted_iota(jnp.int32, (tq, 2 * tq), 1)
    dist = t + tq - s
    allowed = (dist >= 0) & (dist < WINDOW) & ((s >= tq) | (i > 0))
    sidx = s[0:1, :].astype(F32)
    trow = (t[:, 0:1] + tq).astype(F32)
    qscale = hd ** -0.5 * LOG2E
    nt = (((1,), (1,)), ((), ()))
    def scores(kv):
        ks = slice(kv * hd, (kv + 1) * hd)
        k2 = jnp.concatenate([kp_ref[:, ks], kc_ref[:, ks]], axis=0).astype(BF16)
        q4 = jnp.concatenate([(q_ref[:, h * hd:(h + 1) * hd] * qscale).astype(BF16)
                              for h in range(kv * group, (kv + 1) * group)], axis=0)
        return lax.dot_general(q4, k2, nt, preferred_element_type=F32)

    for kv0 in range(0, n_kv, SWA_WAVE):
        kvs = list(range(kv0, min(kv0 + SWA_WAVE, n_kv)))
        sc4s = [scores(kv) for kv in kvs]
        heads = [kv * group + g for kv in kvs for g in range(group)]
        slopes = [_alibi_slope(h, n_heads) * LOG2E for h in heads]
        scs = [jnp.where(allowed, sc4s[j // group][(j % group) * tq:(j % group + 1) * tq] + slopes[j] * sidx,
                         -jnp.inf) for j in range(len(heads))]
        sinks = [sink_ref[h] * LOG2E + slopes[j] * trow for j, h in enumerate(heads)]
        mxs = [jnp.maximum(jnp.max(scs[j], axis=-1, keepdims=True), sinks[j]) for j in range(len(heads))]
        es = [jnp.exp2(scs[j] - mxs[j]) for j in range(len(heads))]
        dens = [jnp.sum(es[j], axis=-1, keepdims=True) + jnp.exp2(sinks[j] - mxs[j])
                for j in range(len(heads))]
        es = [e.astype(BF16) for e in es]
        o4s = []
        for i, kv in enumerate(kvs):
            ks = slice(kv * hd, (kv + 1) * hd)
            v2 = jnp.concatenate([vp_ref[:, ks], vc_ref[:, ks]], axis=0).astype(BF16)
            o4s.append(jnp.dot(jnp.concatenate(es[i * group:(i + 1) * group], axis=0), v2,
                               preferred_element_type=F32))
        for j, h in enumerate(heads):
            o = o4s[j // group][(j % group) * tq:(j % group + 1) * tq]
            o_ref[:, h * hd:(h + 1) * hd] = (o / dens[j]).astype(o_ref.dtype)


def _swa_prompt(qkv, sinks, n_kv, hd):
    l = qkv.shape[0]
    n_heads = sinks.shape[0]
    group = n_heads // n_kv
    dq = n_heads * hd
    dk = n_kv * hd
    nb = l // WINDOW
    kblk = dq // dk
    return pl.pallas_call(
        functools.partial(_swa_prompt_kernel, n_kv=n_kv, group=group, hd=hd),
        out_shape=jax.ShapeDtypeStruct((l, dq), BF16),
        grid_spec=pltpu.PrefetchScalarGridSpec(
            num_scalar_prefetch=1,
            grid=(nb,),
            in_specs=[pl.BlockSpec((WINDOW, dq), lambda i, s: (i, 0)),
                      pl.BlockSpec((WINDOW, dk), lambda i, s: (jnp.maximum(i - 1, 0), kblk)),
                      pl.BlockSpec((WINDOW, dk), lambda i, s: (i, kblk)),
                      pl.BlockSpec((WINDOW, dk), lambda i, s: (jnp.maximum(i - 1, 0), kblk + 1)),
                      pl.BlockSpec((WINDOW, dk), lambda i, s: (i, kblk + 1))],
            out_specs=pl.BlockSpec((WINDOW, dq), lambda i, s: (i, 0)),
        ),
        compiler_params=_cparams(("arbitrary",), V7X_VMEM_LIMIT),
        name="swa_prompt",
    )(sinks, qkv, qkv, qkv, qkv, qkv)


def _swa_sample_tables(n_heads, n_kv, hd, seq, cr):
    group = n_heads // n_kv
    r = np.arange(seq * n_heads)
    t, h = r // n_heads, r % n_heads
    slope = 2.0 ** (-8.0 * (h + 1) / n_heads)
    slot = (np.arange(n_kv * hd)[None, :] // hd == (h // group)[:, None]).astype(np.float32)
    dist = cr + t[:, None] - np.arange(cr)[None, :]
    bias_c = np.where((dist >= 0) & (dist < WINDOW), -slope[:, None] * dist, -np.inf) * LOG2E
    j = np.arange(LANES)
    distn = t[:, None] - j[None, :]
    bias_n = np.where((distn >= 0) & (j[None, :] < seq), -slope[:, None] * distn, -np.inf) * LOG2E
    return jnp.asarray(slot, F32), jnp.asarray(np.concatenate([bias_c, bias_n], axis=1), F32)


def _swa_sample_kernel(q_ref, kt_ref, vt_ref, kn_ref, vn_ref, slot_ref, bias_ref, sink_ref, o_ref, *, hd):
    bs = q_ref.shape[0]
    cr = kt_ref.shape[2]
    width = slot_ref.shape[1]
    qscale = hd ** -0.5 * LOG2E
    slot = slot_ref[...]
    bias = bias_ref[...]
    sink2 = sink_ref[...] * LOG2E
    nt = (((1,), (1,)), ((), ()))

    def slotted(q):
        return jnp.concatenate([q] * (width // hd), axis=1) * slot

    qs = [slotted(q_ref[b] * qscale) for b in range(bs)]
    qb = [q.astype(BF16) for q in qs]
    zrows = jnp.zeros((LANES - kn_ref.shape[1], width), F32)
    kn = [jnp.concatenate([kn_ref[b], zrows], axis=0).astype(BF16) for b in range(bs)]
    vn = [jnp.concatenate([vn_ref[b], zrows], axis=0).astype(BF16) for b in range(bs)]
    sc = [jnp.concatenate([jnp.dot(qb[b], kt_ref[b].astype(BF16), preferred_element_type=F32),
                           lax.dot_general(qb[b], kn[b], nt, preferred_element_type=F32)], axis=1) + bias
          for b in range(bs)]
    mx = [jnp.maximum(jnp.max(s, axis=-1, keepdims=True), sink2) for s in sc]
    es = [jnp.exp2(s - m) for s, m in zip(sc, mx)]
    den = [jnp.sum(e, axis=-1, keepdims=True) + jnp.exp2(sink2 - m) for e, m in zip(es, mx)]
    es = [e.astype(BF16) for e in es]
    o_w = [lax.dot_general(es[b][:, :cr], vt_ref[b].astype(BF16), nt, preferred_element_type=F32)
           + jnp.dot(es[b][:, cr:], vn[b], preferred_element_type=F32) for b in range(bs)]
    for b in range(bs):
        o = o_w[b] * slot
        o = functools.reduce(jnp.add, [o[:, c * hd:(c + 1) * hd] for c in range(width // hd)])
        o_ref[b] = (o / den[b]).astype(o_ref.dtype)


def _swa_sample(q, k_new, v_new, cache_k, cache_v, sinks, n_kv, hd, seq):
    b, cr = cache_k.shape[0], cache_k.shape[1]
    n_heads = sinks.shape[0]
    rows = seq * n_heads
    width = n_kv * hd
    slot, bias = _swa_sample_tables(n_heads, n_kv, hd, seq, cr)
    kt = cache_k.transpose(0, 2, 3, 1).reshape(b, width, cr)
    vt = cache_v.transpose(0, 2, 3, 1).reshape(b, width, cr)
    new_rows = -(-seq // 8) * 8
    pad_new = lambda x: jnp.pad(x.reshape(b, seq, width), ((0, 0), (0, new_rows - seq), (0, 0)))
    bs = _pick(b, (8, 4, 2, 1))
    const = lambda shape: pl.BlockSpec(shape, lambda i: (0, 0))
    o = pl.pallas_call(
        functools.partial(_swa_sample_kernel, hd=hd),
        out_shape=jax.ShapeDtypeStruct((b, rows, hd), BF16),
        grid=(b // bs,),
        in_specs=[pl.BlockSpec((bs, rows, hd), lambda i: (i, 0, 0)),
                  pl.BlockSpec((bs, width, cr), lambda i: (i, 0, 0)),
                  pl.BlockSpec((bs, width, cr), lambda i: (i, 0, 0)),
                  pl.BlockSpec((bs, new_rows, width), lambda i: (i, 0, 0)),
                  pl.BlockSpec((bs, new_rows, width), lambda i: (i, 0, 0)),
                  const(slot.shape), const(bias.shape), const((rows, 1))],
        out_specs=pl.BlockSpec((bs, rows, hd), lambda i: (i, 0, 0)),
        compiler_params=_cparams(("arbitrary",), V7X_VMEM_LIMIT),
        name="swa_sample",
    )(q.reshape(b, rows, hd), kt, vt, pad_new(k_new), pad_new(v_new),
      slot, bias, jnp.tile(sinks, seq).reshape(rows, 1))
    return o.reshape(b * seq, n_heads * hd)


def _lower_bound(p_ref):
    p0 = p_ref[0:1, :]
    p1 = p_ref[1:2, :]
    mx = jnp.maximum(p0, p1)
    e0 = jnp.exp(p0 - mx)
    e1 = jnp.exp(p1 - mx)
    return e1 / (e0 + e1)


def _hgrn_gates(fr, lb):
    e = jnp.exp(-jnp.abs(fr))
    log_sig = jnp.minimum(fr, 0.0) - jnp.log1p(e)
    logf = _logaddexp(jnp.log(lb), jnp.log1p(-lb) + log_sig)
    r = 1.0 / (1.0 + e)
    k = (1.0 - lb) * jnp.where(fr >= 0.0, e * r, r)
    return logf, k


def _group_norm_gate(o, gr, ng):
    o = o * lax.rsqrt(jnp.mean(o * o, axis=-1, keepdims=True) + EPS) * ng
    return o * _silu(gr)


SMALL = 8

def _level_tables(c):
    g = min(c, 128)
    t = np.arange(g)[:, None]
    s = np.arange(g)[None, :]
    hb = np.floor(np.log2(np.maximum(t ^ s, 1))).astype(np.int32)
    same = (t // SMALL) == (s // SMALL)
    near = np.where(same & (s < t), hb, np.where(s == t, int(math.log2(SMALL)), -1))
    i = np.arange(c // 2)[:, None]
    j = np.arange(c // 2)[None, :]
    far = np.where(i == j, -1, np.floor(np.log2(np.maximum(i ^ j, 1)))).astype(np.int32)
    return jnp.asarray(near, dtype=jnp.int32), jnp.asarray(far, dtype=jnp.int32)


def _group_ref_rows(p, m, row):
    c, w = p.shape
    if m == 1:
        return jnp.where((row & 1) == 1, pltpu.roll(p, 1, 0), p)
    if m == 2:
        r4 = row & 3
        return jnp.where(r4 == 0, pltpu.roll(p, c - 1, 0),
                         jnp.where(r4 == 1, p,
                                   jnp.where(r4 == 2, pltpu.roll(p, 1, 0), pltpu.roll(p, 2, 0))))
    p3 = p.reshape(c // 8, 8, w)
    return jnp.broadcast_to(p3[:, 3:4, :], (c // 8, 8, w)).reshape(c, w)


def _halves(x, m):
    c = x.shape[0]
    ev = [x[r:r + m] for r in range(0, c, 2 * m)]
    od = [x[r + m:r + 2 * m] for r in range(0, c, 2 * m)]
    cat = lambda xs: xs[0] if len(xs) == 1 else jnp.concatenate(xs, axis=0)
    return cat(ev), cat(od)


def _interleave(ev, od, m):
    parts = []
    for r in range(0, ev.shape[0], m):
        parts += [ev[r:r + m], od[r:r + m]]
    return jnp.concatenate(parts, axis=0)


def _hgrn_chunks(chunks, st, near, far):
    c, w = chunks[0][0].shape
    nlev = int(math.log2(c))
    nsmall = int(math.log2(SMALL))
    g = near.shape[0]
    row = lax.broadcasted_iota(jnp.int32, (c, w), 0)
    nt = (((1,), (1,)), ((), ()))
    dot_nt = lambda x, y: lax.dot_general(x, y, nt, preferred_element_type=F32)
    dot_nn = lambda x, y: jnp.dot(x, y, preferred_element_type=F32)

    near_ops, far_ops, tails = [], [], []
    for q, k, v, logf in chunks:
        p = logf
        qs, ks = [q.astype(BF16)], [k.astype(BF16)]
        for l in range(nsmall):
            gref = _group_ref_rows(p, 1 << l, row)
            odd = ((row >> l) & 1) == 1
            e = jnp.exp(jnp.where(odd, p, gref - p))
            qs.append((q * e).astype(BF16))
            ks.append((k * e).astype(BF16))
            p = p + jnp.where(odd, gref, 0.0)
        near_ops.append((qs, ks, v.astype(BF16)))
        lev = []
        for l in range(nsmall, nlev):
            m = 1 << l
            pe, po = _halves(p, m)
            tot = jnp.concatenate([jnp.broadcast_to(pe[r + m - 1:r + m], (m, w))
                                   for r in range(0, c // 2, m)], axis=0)
            q_od = _halves(q, m)[1]
            k_ev = _halves(k, m)[0]
            v_ev = _halves(v, m)[0].astype(BF16)
            lev.append(((q_od * jnp.exp(po)).astype(BF16), (k_ev * jnp.exp(tot - pe)).astype(BF16), v_ev))
            p = _interleave(pe, po + tot, m)
        far_ops.append(lev)
        b = p
        b_end = b[c - 1:c, :]
        tails.append(((q * jnp.exp(b)).astype(BF16), (k * jnp.exp(b_end - b)).astype(BF16),
                      jnp.exp(b_end)))

    near_sc = [[[dot_nt(qs[j][r:r + g], ks[j][r:r + g]) for j in range(nsmall + 1)]
                for r in range(0, c, g)] for qs, ks, _ in near_ops]
    far_sc = [[dot_nt(qt, kt) for qt, kt, _ in lev] for lev in far_ops]
    upds = [lax.dot_general(vb, k_end, (((0,), (0,)), ((), ())), preferred_element_type=F32)
            for (_, _, vb), (_, k_end, _) in zip(near_ops, tails)]

    near_a = []
    for sc_chunk in near_sc:
        tiles = []
        for sc in sc_chunk:
            a = jnp.where(near == nsmall, sc[0], 0.0)
            for l in range(nsmall):
                a = jnp.where(near == l, sc[l + 1], a)
            tiles.append(a.astype(BF16))
        near_a.append(tiles)
    far_a = [[(jnp.where(far < nsmall + j, a, 0.0) if (1 << (nsmall + j)) < c // 2 else a).astype(BF16)
              for j, a in enumerate(lev)] for lev in far_sc]
    states = [st]
    for (_, _, dec), upd in zip(tails, upds):
        states.append(states[-1] * dec + upd)

    near_o = [[dot_nn(a, vb[i * g:(i + 1) * g]) for i, a in enumerate(tiles)]
              for tiles, (_, _, vb) in zip(near_a, near_ops)]
    far_o = [[dot_nn(a, ops[2]) for a, ops in zip(avs, lev)] for avs, lev in zip(far_a, far_ops)]
    inter_o = [dot_nt(qe, s.astype(BF16)) for (qe, _, _), s in zip(tails, states[:-1])]

    outs = []
    for no, fo, io in zip(near_o, far_o, inter_o):
        o = (no[0] if len(no) == 1 else jnp.concatenate(no, axis=0)) + io
        for j, x in enumerate(fo):
            m = 1 << (nsmall + j)
            o_ev, o_od = _halves(o, m)
            o = _interleave(o_ev, o_od + x, m)
        outs.append(o)
    return outs, states[-1]


def _hgrn_prompt_kernel(a_ref, wq_ref, wf_ref, wi_ref, wg_ref, p_ref, ng_ref, near_ref, far_ref,
                        o_ref, s_ref, wbf, st, *, chunk):
    i = pl.program_id(1)
    dk = wq_ref.shape[1]

    @pl.when(i == 0)
    def _():
        wbf[:, 0 * dk:1 * dk] = wq_ref[...].astype(BF16)
        wbf[:, 1 * dk:2 * dk] = wf_ref[...].astype(BF16)
        wbf[:, 2 * dk:3 * dk] = wi_ref[...].astype(BF16)
        wbf[:, 3 * dk:4 * dk] = wg_ref[...].astype(BF16)
        st[...] = jnp.zeros_like(st)

    tm = a_ref.shape[0]
    projs = [jnp.dot(a_ref[c0:c0 + chunk, :], wbf[...], preferred_element_type=F32)
             for c0 in range(0, tm, chunk)]
    lb = _lower_bound(p_ref)
    ng = ng_ref[...]
    chunks = []
    for proj in projs:
        q = _silu(proj[:, 0 * dk:1 * dk])
        logf, k = _hgrn_gates(proj[:, 1 * dk:2 * dk], lb)
        chunks.append((q, k, proj[:, 2 * dk:3 * dk], logf))
    outs, s_t = _hgrn_chunks(chunks, st[...], near_ref[...], far_ref[...])
    for j, (o, proj) in enumerate(zip(outs, projs)):
        o_ref[j * chunk:(j + 1) * chunk, :] = _group_norm_gate(o, proj[:, 3 * dk:4 * dk], ng).astype(o_ref.dtype)
    st[...] = s_t
    s_ref[0, 0] = s_t.T


def _hgrn_prompt(a, w_in, lower_bounds, norm_g, n_heads, dk):
    m, d = a.shape
    chunk = min(HG_CHUNK, m)
    tm = _pick(m, (HG_ROWS, 512, 256, 128, 64, 32, 16, 8))
    tm = max(tm, chunk)
    near, far = _level_tables(chunk)
    w_spec = lambda blk: pl.BlockSpec((d, dk), lambda h, i: (0, blk * n_heads + h))
    o, s = pl.pallas_call(
        functools.partial(_hgrn_prompt_kernel, chunk=chunk),
        out_shape=(jax.ShapeDtypeStruct((m, n_heads * dk), BF16),
                   jax.ShapeDtypeStruct((1, n_heads, dk, dk), F32)),
        grid=(n_heads, m // tm),
        in_specs=[pl.BlockSpec((tm, d), lambda h, i: (i, 0)),
                  w_spec(0), w_spec(1), w_spec(2), w_spec(3),
                  pl.BlockSpec((lower_bounds.shape[0], dk), lambda h, i: (0, h)),
                  pl.BlockSpec((1, dk), lambda h, i: (0, h)),
                  pl.BlockSpec(near.shape, lambda h, i: (0, 0)),
                  pl.BlockSpec(far.shape, lambda h, i: (0, 0))],
        out_specs=(pl.BlockSpec((tm, dk), lambda h, i: (i, h)),
                   pl.BlockSpec((1, 1, dk, dk), lambda h, i: (0, h, 0, 0))),
        scratch_shapes=[pltpu.VMEM((d, 4 * dk), BF16), pltpu.VMEM((dk, dk), F32)],
        compiler_params=_cparams(("arbitrary", "arbitrary"), V7X_VMEM_LIMIT),
        name="hgrn_prompt",
    )(a, w_in, w_in, w_in, w_in, lower_bounds, norm_g.reshape(1, -1), near, far)
    return o, s


def _hgrn_sample_kernel(qr_ref, fr_ref, ir_ref, gr_ref, p_ref, ng_ref, s0_ref, o_ref, s_ref, *, seq):
    bs = s0_ref.shape[0]
    rows = bs * seq
    lb = _lower_bound(p_ref)
    q = _silu(qr_ref[...])
    logf, k = _hgrn_gates(fr_ref[...], lb)
    v = ir_ref[...]
    row = lax.broadcasted_iota(jnp.int32, q.shape, 0)
    t = row % seq
    b = logf
    d = 1
    while d < seq:
        b = b + jnp.where(t >= d, pltpu.roll(b, d, 0), 0.0)
        d *= 2
    o = jnp.sum(q * k, axis=-1, keepdims=True) * v
    for d in range(1, seq):
        ok = t >= d
        arg = jnp.where(ok, b - pltpu.roll(b, d, 0), 0.0)
        w = jnp.sum(q * pltpu.roll(k, d, 0) * jnp.exp(arg), axis=-1, keepdims=True)
        o = o + jnp.where(ok, w, 0.0) * pltpu.roll(v, d, 0)
    qe = (q * jnp.exp(b)).astype(BF16)
    b_end = b
    d = 1
    while d < seq:
        b_end = jnp.where(t < seq - d, pltpu.roll(b_end, rows - d, 0), b_end)
        d *= 2
    k_end = k * jnp.exp(b_end - b)
    k_end_t = k_end.T
    dec_t = jnp.exp(b_end).T
    lane = lax.broadcasted_iota(jnp.int32, k_end_t.shape, 1)
    vb = v.astype(BF16)
    o_inter = []
    for s in range(bs):
        s0 = s0_ref[s, 0]
        in_seq = (lane >= s * seq) & (lane < (s + 1) * seq)
        upd = jnp.dot(jnp.where(in_seq, k_end_t, 0.0).astype(BF16), vb, preferred_element_type=F32)
        s_ref[s, 0] = s0 * dec_t[:, s * seq:s * seq + 1] + upd
        if s % (8 // seq) == 0:
            r0 = s * seq
            parts = []
        oi = jnp.dot(qe[r0:r0 + 8, :], s0.astype(BF16), preferred_element_type=F32)
        parts.append(oi)
        if s % (8 // seq) == (8 // seq) - 1:
            r8 = lax.broadcasted_iota(jnp.int32, oi.shape, 0) // seq
            acc = parts[0]
            for j in range(1, len(parts)):
                acc = jnp.where(r8 == j, parts[j], acc)
            o_inter.append(acc)
    o = o + jnp.concatenate(o_inter, axis=0)
    o_ref[...] = _group_norm_gate(o, gr_ref[...], ng_ref[...]).astype(o_ref.dtype)


def _hgrn_sample(proj, lower_bounds, norm_g, state, seq):
    b, n_heads, dk, dv = state.shape
    m = proj.shape[0]
    bs = _pick(b, (64, 32, 16, 8, 4, 2))
    rows = bs * seq
    col = lambda blk: pl.BlockSpec((rows, dk), lambda h, i: (i, blk * n_heads + h))
    o, s = pl.pallas_call(
        functools.partial(_hgrn_sample_kernel, seq=seq),
        out_shape=(jax.ShapeDtypeStruct((m, n_heads * dv), BF16),
                   jax.ShapeDtypeStruct(state.shape, F32)),
        grid=(n_heads, b // bs),
        in_specs=[col(0), col(1), col(2), col(3),
                  pl.BlockSpec((lower_bounds.shape[0], dk), lambda h, i: (0, h)),
                  pl.BlockSpec((1, dv), lambda h, i: (0, h)),
                  pl.BlockSpec((bs, 1, dk, dv), lambda h, i: (i, h, 0, 0))],
        out_specs=(pl.BlockSpec((rows, dv), lambda h, i: (i, h)),
                   pl.BlockSpec((bs, 1, dk, dv), lambda h, i: (i, h, 0, 0))),
        compiler_params=_cparams(("arbitrary", "arbitrary"), V7X_VMEM_LIMIT),
        name="hgrn_sample",
    )(proj, proj, proj, proj, lower_bounds, norm_g.reshape(1, -1), state)
    return o, s


def kernel(x_prompt, x_sample, cache_swa_k, cache_swa_v, state_hgrn, state_ffn_conv, c_prompt, c_sample,
           norm1_g, norm2_g, w_ada, b_ada, attn_w_qkv, attn_w_o, attn_sinks,
           hgrn_w_in, hgrn_lower_bounds, hgrn_norm_g, hgrn_w_o,
           ffn_w_in, ffn_conv_w, ffn_conv_b, ffn_w_out, final_norm_g):
    bp, lp, d = x_prompt.shape
    bsmp, ls, _ = x_sample.shape
    assert bp == 1, "the prompt kernels carry one sequence"
    depth = w_ada.shape[0]
    n_kv, hd = cache_swa_k.shape[2], cache_swa_k.shape[3]
    n_heads = attn_sinks.shape[0]
    dq = n_heads * hd
    dkv = n_kv * hd
    hg_heads, hg_dk = state_hgrn.shape[1], state_hgrn.shape[2]
    keep = min(WINDOW, lp)

    ns = bsmp * ls
    assert ns % 8 == 0
    c_all = jnp.concatenate([jnp.repeat(c_sample, ls, axis=0), c_prompt, jnp.zeros((7, d), F32)], axis=0)
    mod = _ada(c_all, w_ada, b_ada)
    pk = dict(sample=False, n_sample=ns)
    sk = dict(sample=True, n_sample=ns)

    xp = x_prompt.reshape(lp, d)
    xs = x_sample.reshape(ns, d)
    conv_p, conv_s = [], []
    outs = {}
    w_out_bf = ffn_w_out.astype(BF16)
    hp = _modulate(xp, norm1_g, mod, 0, 0, 1, False, ns)
    hs = _modulate(xs, norm1_g, mod, 0, 0, 1, True, ns)
    for l in range(depth):
        if l % 2 == 0:
            qkv_p, qkv_s = _matmul2(hp, hs, attn_w_qkv, tm=1024, tn=1024, name="qkv")
            op = _swa_prompt(qkv_p, attn_sinks, n_kv, hd)
            outs['swa_k_prompt'] = qkv_p[lp - keep:, dq:dq + dkv].reshape(bp, keep, n_kv, hd)
            outs['swa_v_prompt'] = qkv_p[lp - keep:, dq + dkv:].reshape(bp, keep, n_kv, hd)
            outs['swa_k_sample'] = qkv_s[:, dq:dq + dkv].reshape(bsmp, ls, n_kv, hd)
            outs['swa_v_sample'] = qkv_s[:, dq + dkv:].reshape(bsmp, ls, n_kv, hd)
            os_ = _swa_sample(qkv_s[:, :dq], qkv_s[:, dq:dq + dkv], qkv_s[:, dq + dkv:],
                              cache_swa_k, cache_swa_v, attn_sinks, n_kv, hd, ls)
            w_o = attn_w_o
        else:
            op, sp = _hgrn_prompt(hp, hgrn_w_in, hgrn_lower_bounds, hgrn_norm_g, hg_heads, hg_dk)
            proj_s = _matmul(hs, hgrn_w_in, name="hgrn_in_sample")
            os_, ss = _hgrn_sample(proj_s, hgrn_lower_bounds, hgrn_norm_g, state_hgrn, ls)
            outs['hgrn_state_prompt'] = sp
            outs['hgrn_state_sample'] = ss
            w_o = hgrn_w_o
        xp, hp = _mm_rows(op, w_o, xp, mod, gate=(l, 2), norm_g=norm2_g, norm_layer=l, next_mod=(l, 3, 4),
                          tm=512, name="mix_out_prompt", **pk)
        xs, hs = _mm_rows(os_, w_o, xs, mod, gate=(l, 2), norm_g=norm2_g, norm_layer=l, next_mod=(l, 3, 4),
                          tm=512, name="mix_out_sample", **sk)
        actp, bufp = _ffn_in_prompt(hp, ffn_w_in, ffn_conv_w, ffn_conv_b, l)
        acts, bufs = _ffn_in_sample(hs, ffn_w_in, ffn_conv_w, ffn_conv_b, l, state_ffn_conv, ls)
        conv_p.append(bufp[None])
        conv_s.append(bufs)
        if l + 1 < depth:
            nxt = dict(norm_g=norm1_g, norm_layer=l + 1, next_mod=(l + 1, 0, 1))
        else:
            nxt = dict(norm_g=final_norm_g, norm_layer=0, next_mod=None, keep_x=False, out_dtype=F32)
        xp, hp = _mm_rows(actp, w_out_bf, xp, mod, gate=(l, 5), w_layer=l, tm=256, name="ffn_out_prompt",
                          **nxt, **pk)
        xs, hs = _mm_rows(acts, w_out_bf, xs, mod, gate=(l, 5), w_layer=l, tm=256, name="ffn_out_sample",
                          **nxt, **sk)
    y_prompt = hp.reshape(bp, lp, d)
    y_sample = hs.reshape(bsmp, ls, d)
    return (y_prompt, y_sample,
            outs['swa_k_prompt'], outs['swa_v_prompt'], outs['swa_k_sample'], outs['swa_v_sample'],
            outs['hgrn_state_prompt'], outs['hgrn_state_sample'],
            jnp.stack(conv_p), jnp.stack(conv_s))
```

```python
import functools
import math

import numpy as np
import jax
import jax.numpy as jnp
from jax import lax
from jax.experimental import pallas as pl
from jax.experimental.pallas import tpu as pltpu

F32 = jnp.float32
BF16 = jnp.bfloat16

EPS = 1e-6
LOG2E = math.log2(math.e)
WINDOW = 128
CONV_W = 3
LANES = 128
HG_CHUNK = 256
HG_ROWS = 1024
SWA_WAVE = 8
V7X_VMEM_LIMIT = 56 * 1024 * 1024


def _cparams(sem, vmem=None):
    return pltpu.CompilerParams(dimension_semantics=sem, vmem_limit_bytes=vmem)


def _pick(n, cands):
    for c in cands:
        if n % c == 0:
            return c
    return n


def _silu(x):
    return x * jax.nn.sigmoid(x)


def _logaddexp(a, b):
    return jnp.maximum(a, b) + jnp.log1p(jnp.exp(-jnp.abs(a - b)))


def _ada_kernel(c_ref, w_ref, b_ref, o_ref, act):
    @pl.when((pl.program_id(0) == 0) & (pl.program_id(1) == 0))
    def _():
        act[...] = _silu(c_ref[...]).astype(BF16)

    w = w_ref[0].astype(BF16)
    o_ref[0] = jnp.dot(act[...], w, preferred_element_type=F32) + b_ref[0]


def _ada(c_all, w_ada, b_ada):
    depth, d, n = w_ada.shape
    r = c_all.shape[0]
    tn = _pick(n, (1024, 512, 256, 128))
    return pl.pallas_call(
        _ada_kernel,
        out_shape=jax.ShapeDtypeStruct((depth, r, n), F32),
        grid=(depth, n // tn),
        in_specs=[pl.BlockSpec((r, d), lambda l, j: (0, 0)),
                  pl.BlockSpec((1, d, tn), lambda l, j: (l, 0, j)),
                  pl.BlockSpec((1, 1, tn), lambda l, j: (l, 0, j))],
        out_specs=pl.BlockSpec((1, r, tn), lambda l, j: (l, 0, j)),
        scratch_shapes=[pltpu.VMEM((r, d), BF16)],
        compiler_params=_cparams(("arbitrary", "arbitrary"), V7X_VMEM_LIMIT),
        name="ada",
    )(c_all, w_ada, b_ada.reshape(depth, 1, n))


def _mod_rows(sample, tm, n_sample):
    if sample:
        return tm, (lambda i: i)
    return 8, (lambda i: n_sample // 8)


def _modulate_kernel(x_ref, g_ref, sc_ref, sh_ref, o_ref, *, sample):
    x = x_ref[...]
    ms = jnp.mean(x * x, axis=-1, keepdims=True)
    y = x * lax.rsqrt(ms + EPS) * g_ref[...]
    sc = sc_ref[...] if sample else sc_ref[0:1, :]
    sh = sh_ref[...] if sample else sh_ref[0:1, :]
    o_ref[...] = (y * (1.0 + sc) + sh).astype(o_ref.dtype)


def _modulate(x, g_all, mod, l, shift_col, scale_col, sample, n_sample):
    m, d = x.shape
    tm = _pick(m, (512, 256, 128, 64, 32, 16, 8))
    rows, rmap = _mod_rows(sample, tm, n_sample)
    mspec = lambda col: pl.BlockSpec((None, rows, d), lambda i: (l, rmap(i), col))
    return pl.pallas_call(
        functools.partial(_modulate_kernel, sample=sample),
        out_shape=jax.ShapeDtypeStruct((m, d), BF16),
        grid=(m // tm,),
        in_specs=[pl.BlockSpec((tm, d), lambda i: (i, 0)),
                  pl.BlockSpec((None, 1, d), lambda i: (l, 0, 0)),
                  mspec(scale_col), mspec(shift_col)],
        out_specs=pl.BlockSpec((tm, d), lambda i: (i, 0)),
        compiler_params=_cparams(("arbitrary",), V7X_VMEM_LIMIT),
        name="modulate",
    )(x, g_all.reshape(g_all.shape[0], 1, d), mod, mod)


def _mm_kernel(a_ref, w_ref, o_ref, wbf):
    @pl.when(pl.program_id(1) == 0)
    def _():
        wbf[...] = w_ref[...].astype(BF16)

    o_ref[...] = jnp.dot(a_ref[...], wbf[...], preferred_element_type=F32)


def _mm_mod_kernel(x_ref, g_ref, sc_ref, sh_ref, w_ref, o_ref, wbf):
    @pl.when(pl.program_id(1) == 0)
    def _():
        wbf[...] = w_ref[...].astype(BF16)

    x = x_ref[...]
    y = x * lax.rsqrt(jnp.mean(x * x, axis=-1, keepdims=True) + EPS) * g_ref[...]
    h = (y * (1.0 + sc_ref[0:1, :]) + sh_ref[0:1, :]).astype(BF16)
    o_ref[...] = jnp.dot(h, wbf[...], preferred_element_type=F32)


def _matmul_mod(x, g_all, mod, l, shift_col, scale_col, w, n_sample, *, tm, tn, name):
    m, d = x.shape
    n = w.shape[-1]
    tm, tn = min(tm, m), min(tn, n)
    assert m % tm == 0 and n % tn == 0
    rows, rmap = _mod_rows(False, tm, n_sample)
    mspec = lambda col: pl.BlockSpec((None, rows, d), lambda j, i: (l, rmap(i), col))
    return pl.pallas_call(
        _mm_mod_kernel,
        out_shape=jax.ShapeDtypeStruct((m, n), F32),
        grid=(n // tn, m // tm),
        in_specs=[pl.BlockSpec((tm, d), lambda j, i: (i, 0)),
                  pl.BlockSpec((None, 1, d), lambda j, i: (l, 0, 0)),
                  mspec(scale_col), mspec(shift_col),
                  pl.BlockSpec((d, tn), lambda j, i: (0, j))],
        out_specs=pl.BlockSpec((tm, tn), lambda j, i: (i, j)),
        scratch_shapes=[pltpu.VMEM((d, tn), BF16)],
        compiler_params=_cparams(("arbitrary", "arbitrary"), V7X_VMEM_LIMIT),
        name=name,
    )(x, g_all.reshape(g_all.shape[0], 1, d), mod, mod, w)


def _matmul(a, w, *, tm=None, tn=None, name="matmul"):
    m, k = a.shape
    n = w.shape[-1]
    tm = min(tm, m) if tm else _pick(m, (1024, 512, 256, 128, 64, 32, 16, 8))
    tn = min(tn, n) if tn else _pick(n, (512, 256, 128))
    assert m % tm == 0 and n % tn == 0
    return pl.pallas_call(
        _mm_kernel,
        out_shape=jax.ShapeDtypeStruct((m, n), F32),
        grid=(n // tn, m // tm),
        in_specs=[pl.BlockSpec((tm, k), lambda j, i: (i, 0)),
                  pl.BlockSpec((k, tn), lambda j, i: (0, j))],
        out_specs=pl.BlockSpec((tm, tn), lambda j, i: (i, j)),
        scratch_shapes=[pltpu.VMEM((k, tn), BF16)],
        compiler_params=_cparams(("arbitrary", "arbitrary"), V7X_VMEM_LIMIT),
        name=name,
    )(a, w)


def _mm_rows_kernel(*refs, cast_w, sample, modulate, keep_x):
    it = iter(refs)
    a_ref, w_ref, x_ref, gate_ref, g_ref = (next(it) for _ in range(5))
    sc_ref, sh_ref = (next(it), next(it)) if modulate else (None, None)
    xo_ref = next(it) if keep_x else None
    h_ref = next(it)
    if cast_w:
        wbf = next(it)

        @pl.when(pl.program_id(0) == 0)
        def _():
            wbf[...] = w_ref[...].astype(BF16)

        w = wbf[...]
    else:
        w = w_ref[...]
    rows = (lambda r: r[...]) if sample else (lambda r: r[0:1, :])
    x = x_ref[...] + rows(gate_ref) * jnp.dot(a_ref[...], w, preferred_element_type=F32)
    if keep_x:
        xo_ref[...] = x
    y = x * lax.rsqrt(jnp.mean(x * x, axis=-1, keepdims=True) + EPS) * g_ref[...]
    if modulate:
        y = y * (1.0 + rows(sc_ref)) + rows(sh_ref)
    h_ref[...] = y.astype(h_ref.dtype)


def _mm_rows(a, w, x, mod, *, gate, norm_g, norm_layer, next_mod=None, w_layer=None, sample, n_sample,
             tm, keep_x=True, out_dtype=None, name):
    m, k = a.shape
    n = w.shape[-1]
    tm = min(tm, m)
    assert m % tm == 0
    cast_w = w.dtype != BF16
    rows, rmap = _mod_rows(sample, tm, n_sample)
    once = pl.Buffered(1)
    if w_layer is None:
        w_spec = pl.BlockSpec((k, n), lambda i: (0, 0), pipeline_mode=once)
    else:
        w_spec = pl.BlockSpec((None, k, n), lambda i: (w_layer, 0, 0), pipeline_mode=once)
    mspec = lambda layer, col: pl.BlockSpec((None, rows, n), lambda i: (layer, rmap(i), col))
    in_specs = [pl.BlockSpec((tm, k), lambda i: (i, 0)), w_spec,
                pl.BlockSpec((tm, n), lambda i: (i, 0)),
                mspec(*gate),
                pl.BlockSpec((None, 1, n), lambda i: (norm_layer, 0, 0))]
    args = [a, w, x, mod, norm_g.reshape(-1, 1, n)]
    if next_mod is not None:
        in_specs += [mspec(next_mod[0], next_mod[2]), mspec(next_mod[0], next_mod[1])]
        args += [mod, mod]
    out_shape, out_specs = [], []
    if keep_x:
        out_shape.append(jax.ShapeDtypeStruct((m, n), F32))
        out_specs.append(pl.BlockSpec((tm, n), lambda i: (i, 0)))
    out_shape.append(jax.ShapeDtypeStruct((m, n), out_dtype or BF16))
    out_specs.append(pl.BlockSpec((tm, n), lambda i: (i, 0)))
    res = pl.pallas_call(
        functools.partial(_mm_rows_kernel, cast_w=cast_w, sample=sample, modulate=next_mod is not None,
                          keep_x=keep_x),
        out_shape=tuple(out_shape),
        grid=(m // tm,),
        in_specs=in_specs,
        out_specs=tuple(out_specs),
        scratch_shapes=[pltpu.VMEM((k, n), BF16)] if cast_w else [],
        compiler_params=_cparams(("arbitrary",), V7X_VMEM_LIMIT),
        name=name,
    )(*args)
    return res if keep_x else (None, res[0])


def _gelu_exact(a):
    return 0.5 * a * (1.0 + lax.erf(a * (1.0 / math.sqrt(2.0))))


def _ffn_in_prompt_kernel(a_ref, wu_ref, wg_ref, cw_ref, cb_ref, act_ref, tail_ref, wbf, carry):
    i = pl.program_id(1)
    tf = wu_ref.shape[1]

    @pl.when(i == 0)
    def _():
        wbf[:, :tf] = wu_ref[...].astype(BF16)
        wbf[:, tf:] = wg_ref[...].astype(BF16)
        carry[...] = jnp.zeros_like(carry)

    ug = jnp.dot(a_ref[...], wbf[...], preferred_element_type=F32)
    u = ug[:, :tf]
    g = ug[:, tf:]
    tm = u.shape[0]
    row = lax.broadcasted_iota(jnp.int32, u.shape, 0)
    c = carry[...]
    u1 = jnp.where(row == 0, c[7:8, :], pltpu.roll(u, 1, 0))
    u2 = jnp.where(row == 0, c[6:7, :], jnp.where(row == 1, c[7:8, :], pltpu.roll(u, 2, 0)))
    cw = cw_ref[...]
    conv = cw[0:1, :] * u2 + cw[1:2, :] * u1 + cw[2:3, :] * u + cb_ref[...]
    act_ref[...] = (_gelu_exact(conv) * g).astype(act_ref.dtype)
    carry[...] = u[tm - 8:, :]
    tail_ref[...] = u[tm - 8:, :]


def _ffn_in_sample_kernel(a_ref, wu_ref, wg_ref, cw_ref, cb_ref, buf_ref, act_ref, nbuf_ref,
                          us, gs, acts, *, seq):
    a = a_ref[...]
    b = buf_ref.shape[0]
    u_all = jnp.dot(a, wu_ref[...].astype(BF16), preferred_element_type=F32)
    g_all = jnp.dot(a, wg_ref[...].astype(BF16), preferred_element_type=F32)
    cw = cw_ref[...]
    cb = cb_ref[...]
    for c in range(us.shape[0]):
        cs = slice(c * LANES, (c + 1) * LANES)
        us[c] = u_all[:, cs]
        gs[c] = g_all[:, cs]
        prev2, prev1 = buf_ref[:, 0, cs], buf_ref[:, 1, cs]
        for t in range(seq):
            u = us[c, pl.ds(t, b, stride=seq), :]
            conv = cw[0:1, cs] * prev2 + cw[1:2, cs] * prev1 + cw[2:3, cs] * u + cb[:, cs]
            acts[c, pl.ds(t, b, stride=seq), :] = _gelu_exact(conv) * gs[c, pl.ds(t, b, stride=seq), :]
            prev2, prev1 = prev1, u
        act_ref[:, cs] = acts[c].astype(act_ref.dtype)
        nbuf_ref[:, 0, cs] = prev2
        nbuf_ref[:, 1, cs] = prev1


def _ffn_in_prompt(a, w_in, conv_w, conv_b, l):
    m, d = a.shape
    f = w_in.shape[-1] // 2
    tf = _pick(f, (512, 256, 128))
    tm = _pick(m, (1024, 512, 256, 128, 64, 32, 16, 8))
    nf = f // tf
    act, tail = pl.pallas_call(
        _ffn_in_prompt_kernel,
        out_shape=(jax.ShapeDtypeStruct((m, f), BF16), jax.ShapeDtypeStruct((8, f), F32)),
        grid=(nf, m // tm),
        in_specs=[pl.BlockSpec((tm, d), lambda j, i: (i, 0)),
                  pl.BlockSpec((None, d, tf), lambda j, i: (l, 0, j)),
                  pl.BlockSpec((None, d, tf), lambda j, i: (l, 0, nf + j)),
                  pl.BlockSpec((None, CONV_W, tf), lambda j, i: (l, 0, j)),
                  pl.BlockSpec((None, 1, tf), lambda j, i: (l, 0, j))],
        out_specs=(pl.BlockSpec((tm, tf), lambda j, i: (i, j)),
                   pl.BlockSpec((8, tf), lambda j, i: (0, j))),
        scratch_shapes=[pltpu.VMEM((d, 2 * tf), BF16), pltpu.VMEM((8, tf), F32)],
        compiler_params=_cparams(("arbitrary", "arbitrary"), V7X_VMEM_LIMIT),
        name="ffn_in_prompt",
    )(a, w_in, w_in, conv_w, conv_b.reshape(conv_b.shape[0], 1, f))
    return act, tail[6:8]


def _ffn_in_sample(a, w_in, conv_w, conv_b, l, state, seq):
    m, d = a.shape
    f = w_in.shape[-1] // 2
    b = state.shape[1]
    tf = _pick(f, (512, 256, 128))
    nf = f // tf
    return pl.pallas_call(
        functools.partial(_ffn_in_sample_kernel, seq=seq),
        out_shape=(jax.ShapeDtypeStruct((m, f), BF16), jax.ShapeDtypeStruct((b, CONV_W - 1, f), F32)),
        grid=(nf,),
        in_specs=[pl.BlockSpec((m, d), lambda j: (0, 0)),
                  pl.BlockSpec((None, d, tf), lambda j: (l, 0, j)),
                  pl.BlockSpec((None, d, tf), lambda j: (l, 0, nf + j)),
                  pl.BlockSpec((None, CONV_W, tf), lambda j: (l, 0, j)),
                  pl.BlockSpec((None, 1, tf), lambda j: (l, 0, j)),
                  pl.BlockSpec((None, b, CONV_W - 1, tf), lambda j: (l, 0, 0, j))],
        out_specs=(pl.BlockSpec((m, tf), lambda j: (0, j)),
                   pl.BlockSpec((b, CONV_W - 1, tf), lambda j: (0, 0, j))),
        scratch_shapes=[pltpu.VMEM((tf // LANES, m, LANES), F32)] * 3,
        compiler_params=_cparams(("arbitrary",), V7X_VMEM_LIMIT),
        name="ffn_in_sample",
    )(a, w_in, w_in, conv_w, conv_b.reshape(conv_b.shape[0], 1, f), state)


def _alibi_slope(h, n_heads):
    return 2.0 ** (-8.0 * (h + 1) / n_heads)


def _swa_prompt_bias(n_heads, tq):
    t = np.arange(tq)[:, None]
    s = np.arange(2 * tq)[None, :]
    dist = t + tq - s
    ok = (dist >= 0) & (dist < WINDOW)
    slope = (2.0 ** (-8.0 * (np.arange(n_heads) + 1) / n_heads) * LOG2E)[:, None, None]
    table = np.stack([np.where(ok & (s >= tq), slope * s, -np.inf), np.where(ok, slope * s, -np.inf)])
    return jnp.asarray(table, F32)


def _swa_prompt_kernel(sink_ref, q_ref, kp_ref, kc_ref, vp_ref, vc_ref, bias_ref, o_ref, *, n_kv, group, hd):
    n_heads = n_kv * group
    tq = q_ref.shape[0]
    trow = (lax.broadcasted_iota(jnp.int32, (tq, 1), 0) + tq).astype(F32)
    qscale = hd ** -0.5 * LOG2E
    nt = (((1,), (1,)), ((), ()))
    def scores(kv):
        ks = slice(kv * hd, (kv + 1) * hd)
        k2 = jnp.concatenate([kp_ref[:, ks], kc_ref[:, ks]], axis=0).astype(BF16)
        q4 = jnp.concatenate([(q_ref[:, h * hd:(h + 1) * hd] * qscale).astype(BF16)
                              for h in range(kv * group, (kv + 1) * group)], axis=0)
        return lax.dot_general(q4, k2, nt, preferred_element_type=F32)

    for kv0 in range(0, n_kv, SWA_WAVE):
        kvs = list(range(kv0, min(kv0 + SWA_WAVE, n_kv)))
        sc4s = [scores(kv) for kv in kvs]
        heads = [kv * group + g for kv in kvs for g in range(group)]
        slopes = [_alibi_slope(h, n_heads) * LOG2E for h in heads]
        scs = [sc4s[j // group][(j % group) * tq:(j % group + 1) * tq] + bias_ref[h]
               for j, h in enumerate(heads)]
        sinks = [sink_ref[h] * LOG2E + slopes[j] * trow for j, h in enumerate(heads)]
        mxs = [jnp.maximum(jnp.max(scs[j], axis=-1, keepdims=True), sinks[j]) for j in range(len(heads))]
        es = [jnp.exp2(scs[j] - mxs[j]) for j in range(len(heads))]
        dens = [jnp.sum(es[j], axis=-1, keepdims=True) + jnp.exp2(sinks[j] - mxs[j])
                for j in range(len(heads))]
        es = [e.astype(BF16) for e in es]
        o4s = []
        for i, kv in enumerate(kvs):
            ks = slice(kv * hd, (kv + 1) * hd)
            v2 = jnp.concatenate([vp_ref[:, ks], vc_ref[:, ks]], axis=0).astype(BF16)
            o4s.append(jnp.dot(jnp.concatenate(es[i * group:(i + 1) * group], axis=0), v2,
                               preferred_element_type=F32))
        for j, h in enumerate(heads):
            o = o4s[j // group][(j % group) * tq:(j % group + 1) * tq]
            o_ref[:, h * hd:(h + 1) * hd] = (o / dens[j]).astype(o_ref.dtype)


def _swa_prompt(qkv, sinks, n_kv, hd):
    l = qkv.shape[0]
    n_heads = sinks.shape[0]
    group = n_heads // n_kv
    dq = n_heads * hd
    dk = n_kv * hd
    nb = l // WINDOW
    kblk = dq // dk
    bias = _swa_prompt_bias(n_heads, WINDOW)
    return pl.pallas_call(
        functools.partial(_swa_prompt_kernel, n_kv=n_kv, group=group, hd=hd),
        out_shape=jax.ShapeDtypeStruct((l, dq), BF16),
        grid_spec=pltpu.PrefetchScalarGridSpec(
            num_scalar_prefetch=1,
            grid=(nb,),
            in_specs=[pl.BlockSpec((WINDOW, dq), lambda i, s: (i, 0)),
                      pl.BlockSpec((WINDOW, dk), lambda i, s: (jnp.maximum(i - 1, 0), kblk)),
                      pl.BlockSpec((WINDOW, dk), lambda i, s: (i, kblk)),
                      pl.BlockSpec((WINDOW, dk), lambda i, s: (jnp.maximum(i - 1, 0), kblk + 1)),
                      pl.BlockSpec((WINDOW, dk), lambda i, s: (i, kblk + 1)),
                      pl.BlockSpec((None,) + bias.shape[1:], lambda i, s: (jnp.minimum(i, 1), 0, 0, 0))],
            out_specs=pl.BlockSpec((WINDOW, dq), lambda i, s: (i, 0)),
        ),
        compiler_params=_cparams(("arbitrary",), V7X_VMEM_LIMIT),
        name="swa_prompt",
    )(sinks, qkv, qkv, qkv, qkv, qkv, bias)


def _swa_sample_tables(n_heads, n_kv, hd, seq, cr):
    group = n_heads // n_kv
    r = np.arange(seq * n_heads)
    t, h = r // n_heads, r % n_heads
    slope = 2.0 ** (-8.0 * (h + 1) / n_heads)
    slot = (np.arange(n_kv * hd)[None, :] // hd == (h // group)[:, None]).astype(np.float32)
    dist = cr + t[:, None] - np.arange(cr)[None, :]
    bias_c = np.where((dist >= 0) & (dist < WINDOW), -slope[:, None] * dist, -np.inf) * LOG2E
    j = np.arange(LANES)
    distn = t[:, None] - j[None, :]
    bias_n = np.where((distn >= 0) & (j[None, :] < seq), -slope[:, None] * distn, -np.inf) * LOG2E
    return jnp.asarray(slot, F32), jnp.asarray(np.concatenate([bias_c, bias_n], axis=1), F32)


def _swa_sample_kernel(q_ref, kt_ref, vt_ref, kn_ref, vn_ref, slot_ref, bias_ref, sink_ref, o_ref, *, hd):
    bs = q_ref.shape[0]
    cr = kt_ref.shape[2]
    width = slot_ref.shape[1]
    qscale = hd ** -0.5 * LOG2E
    slot = slot_ref[...]
    bias = bias_ref[...]
    sink2 = sink_ref[...] * LOG2E
    nt = (((1,), (1,)), ((), ()))

    def slotted(q):
        return jnp.concatenate([q] * (width // hd), axis=1) * slot

    qs = [slotted(q_ref[b] * qscale) for b in range(bs)]
    qb = [q.astype(BF16) for q in qs]
    zrows = jnp.zeros((LANES - kn_ref.shape[1], width), F32)
    kn = [jnp.concatenate([kn_ref[b], zrows], axis=0).astype(BF16) for b in range(bs)]
    vn = [jnp.concatenate([vn_ref[b], zrows], axis=0).astype(BF16) for b in range(bs)]
    sc = [jnp.concatenate([jnp.dot(qb[b], kt_ref[b].astype(BF16), preferred_element_type=F32),
                           lax.dot_general(qb[b], kn[b], nt, preferred_element_type=F32)], axis=1) + bias
          for b in range(bs)]
    mx = [jnp.maximum(jnp.max(s, axis=-1, keepdims=True), sink2) for s in sc]
    es = [jnp.exp2(s - m) for s, m in zip(sc, mx)]
    den = [jnp.sum(e, axis=-1, keepdims=True) + jnp.exp2(sink2 - m) for e, m in zip(es, mx)]
    es = [e.astype(BF16) for e in es]
    o_w = [lax.dot_general(es[b][:, :cr], vt_ref[b].astype(BF16), nt, preferred_element_type=F32)
           + jnp.dot(es[b][:, cr:], vn[b], preferred_element_type=F32) for b in range(bs)]
    for b in range(bs):
        o = o_w[b] * slot
        o = functools.reduce(jnp.add, [o[:, c * hd:(c + 1) * hd] for c in range(width // hd)])
        o_ref[b] = (o / den[b]).astype(o_ref.dtype)


def _swa_sample(q, k_new, v_new, cache_k, cache_v, sinks, n_kv, hd, seq):
    b, cr = cache_k.shape[0], cache_k.shape[1]
    n_heads = sinks.shape[0]
    rows = seq * n_heads
    width = n_kv * hd
    slot, bias = _swa_sample_tables(n_heads, n_kv, hd, seq, cr)
    kt = cache_k.transpose(0, 2, 3, 1).reshape(b, width, cr)
    vt = cache_v.transpose(0, 2, 3, 1).reshape(b, width, cr)
    new_rows = -(-seq // 8) * 8
    pad_new = lambda x: jnp.pad(x.reshape(b, seq, width), ((0, 0), (0, new_rows - seq), (0, 0)))
    bs = _pick(b, (8, 4, 2, 1))
    const = lambda shape: pl.BlockSpec(shape, lambda i: (0, 0))
    o = pl.pallas_call(
        functools.partial(_swa_sample_kernel, hd=hd),
        out_shape=jax.ShapeDtypeStruct((b, rows, hd), BF16),
        grid=(b // bs,),
        in_specs=[pl.BlockSpec((bs, rows, hd), lambda i: (i, 0, 0)),
                  pl.BlockSpec((bs, width, cr), lambda i: (i, 0, 0)),
                  pl.BlockSpec((bs, width, cr), lambda i: (i, 0, 0)),
                  pl.BlockSpec((bs, new_rows, width), lambda i: (i, 0, 0)),
                  pl.BlockSpec((bs, new_rows, width), lambda i: (i, 0, 0)),
                  const(slot.shape), const(bias.shape), const((rows, 1))],
        out_specs=pl.BlockSpec((bs, rows, hd), lambda i: (i, 0, 0)),
        compiler_params=_cparams(("arbitrary",), V7X_VMEM_LIMIT),
        name="swa_sample",
    )(q.reshape(b, rows, hd), kt, vt, pad_new(k_new), pad_new(v_new),
      slot, bias, jnp.tile(sinks, seq).reshape(rows, 1))
    return o.reshape(b * seq, n_heads * hd)


def _lower_bound(p_ref):
    p0 = p_ref[0:1, :]
    p1 = p_ref[1:2, :]
    mx = jnp.maximum(p0, p1)
    e0 = jnp.exp(p0 - mx)
    e1 = jnp.exp(p1 - mx)
    return e1 / (e0 + e1)


def _hgrn_gates(fr, lb):
    e = jnp.exp(-jnp.abs(fr))
    log_sig = jnp.minimum(fr, 0.0) - jnp.log1p(e)
    logf = _logaddexp(jnp.log(lb), jnp.log1p(-lb) + log_sig)
    r = 1.0 / (1.0 + e)
    k = (1.0 - lb) * jnp.where(fr >= 0.0, e * r, r)
    return logf, k


def _group_norm_gate(o, gr, ng):
    o = o * lax.rsqrt(jnp.mean(o * o, axis=-1, keepdims=True) + EPS) * ng
    return o * _silu(gr)


SMALL = 8

def _level_tables(c):
    g = min(c, 128)
    t = np.arange(g)[:, None]
    s = np.arange(g)[None, :]
    hb = np.floor(np.log2(np.maximum(t ^ s, 1))).astype(np.int32)
    same = (t // SMALL) == (s // SMALL)
    near = np.where(same & (s < t), hb, np.where(s == t, int(math.log2(SMALL)), -1))
    i = np.arange(c // 2)[:, None]
    j = np.arange(c // 2)[None, :]
    far = np.where(i == j, -1, np.floor(np.log2(np.maximum(i ^ j, 1)))).astype(np.int32)
    return jnp.asarray(near, dtype=jnp.int32), jnp.asarray(far, dtype=jnp.int32)


def _group_ref_rows(p, m, row):
    c, w = p.shape
    if m == 1:
        return jnp.where((row & 1) == 1, pltpu.roll(p, 1, 0), p)
    if m == 2:
        r4 = row & 3
        return jnp.where(r4 == 0, pltpu.roll(p, c - 1, 0),
                         jnp.where(r4 == 1, p,
                                   jnp.where(r4 == 2, pltpu.roll(p, 1, 0), pltpu.roll(p, 2, 0))))
    p3 = p.reshape(c // 8, 8, w)
    return jnp.broadcast_to(p3[:, 3:4, :], (c // 8, 8, w)).reshape(c, w)


def _halves(x, m):
    c = x.shape[0]
    ev = [x[r:r + m] for r in range(0, c, 2 * m)]
    od = [x[r + m:r + 2 * m] for r in range(0, c, 2 * m)]
    cat = lambda xs: xs[0] if len(xs) == 1 else jnp.concatenate(xs, axis=0)
    return cat(ev), cat(od)


def _interleave(ev, od, m):
    parts = []
    for r in range(0, ev.shape[0], m):
        parts += [ev[r:r + m], od[r:r + m]]
    return jnp.concatenate(parts, axis=0)


def _hgrn_chunks(chunks, st, near, far):
    c, w = chunks[0][0].shape
    nlev = int(math.log2(c))
    nsmall = int(math.log2(SMALL))
    g = near.shape[0]
    row = lax.broadcasted_iota(jnp.int32, (c, w), 0)
    nt = (((1,), (1,)), ((), ()))
    dot_nt = lambda x, y: lax.dot_general(x, y, nt, preferred_element_type=F32)
    dot_nn = lambda x, y: jnp.dot(x, y, preferred_element_type=F32)

    near_ops, far_ops, tails = [], [], []
    for q, k, v, logf in chunks:
        p = logf
        qs, ks = [q.astype(BF16)], [k.astype(BF16)]
        for l in range(nsmall):
            gref = _group_ref_rows(p, 1 << l, row)
            odd = ((row >> l) & 1) == 1
            e = jnp.exp(jnp.where(odd, p, gref - p))
            qs.append((q * e).astype(BF16))
            ks.append((k * e).astype(BF16))
            p = p + jnp.where(odd, gref, 0.0)
        near_ops.append((qs, ks, v.astype(BF16)))
        lev = []
        for l in range(nsmall, nlev):
            m = 1 << l
            pe, po = _halves(p, m)
            tot = jnp.concatenate([jnp.broadcast_to(pe[r + m - 1:r + m], (m, w))
                                   for r in range(0, c // 2, m)], axis=0)
            q_od = _halves(q, m)[1]
            k_ev = _halves(k, m)[0]
            v_ev = _halves(v, m)[0].astype(BF16)
            lev.append(((q_od * jnp.exp(po)).astype(BF16), (k_ev * jnp.exp(tot - pe)).astype(BF16), v_ev))
            p = _interleave(pe, po + tot, m)
        far_ops.append(lev)
        b = p
        b_end = b[c - 1:c, :]
        tails.append(((q * jnp.exp(b)).astype(BF16), (k * jnp.exp(b_end - b)).astype(BF16),
                      jnp.exp(b_end)))

    near_sc = [[[dot_nt(qs[j][r:r + g], ks[j][r:r + g]) for j in range(nsmall + 1)]
                for r in range(0, c, g)] for qs, ks, _ in near_ops]
    far_sc = [[dot_nt(qt, kt) for qt, kt, _ in lev] for lev in far_ops]
    upds = [lax.dot_general(vb, k_end, (((0,), (0,)), ((), ())), preferred_element_type=F32)
            for (_, _, vb), (_, k_end, _) in zip(near_ops, tails)]

    near_a = []
    for sc_chunk in near_sc:
        tiles = []
        for sc in sc_chunk:
            a = jnp.where(near == nsmall, sc[0], 0.0)
            for l in range(nsmall):
                a = jnp.where(near == l, sc[l + 1], a)
            tiles.append(a.astype(BF16))
        near_a.append(tiles)
    far_a = [[(jnp.where(far < nsmall + j, a, 0.0) if (1 << (nsmall + j)) < c // 2 else a).astype(BF16)
              for j, a in enumerate(lev)] for lev in far_sc]
    states = [st]
    for (_, _, dec), upd in zip(tails, upds):
        states.append(states[-1] * dec + upd)

    near_o = [[dot_nn(a, vb[i * g:(i + 1) * g]) for i, a in enumerate(tiles)]
              for tiles, (_, _, vb) in zip(near_a, near_ops)]
    far_o = [[dot_nn(a, ops[2]) for a, ops in zip(avs, lev)] for avs, lev in zip(far_a, far_ops)]
    inter_o = [dot_nt(qe, s.astype(BF16)) for (qe, _, _), s in zip(tails, states[:-1])]

    outs = []
    for no, fo, io in zip(near_o, far_o, inter_o):
        o = (no[0] if len(no) == 1 else jnp.concatenate(no, axis=0)) + io
        for j, x in enumerate(fo):
            m = 1 << (nsmall + j)
            o_ev, o_od = _halves(o, m)
            o = _interleave(o_ev, o_od + x, m)
        outs.append(o)
    return outs, states[-1]


def _hgrn_prompt_kernel(a_ref, wq_ref, wf_ref, wi_ref, wg_ref, p_ref, ng_ref, near_ref, far_ref,
                        o_ref, s_ref, wbf, st, *, chunk):
    i = pl.program_id(1)
    dk = wq_ref.shape[1]

    @pl.when(i == 0)
    def _():
        wbf[:, 0 * dk:1 * dk] = wq_ref[...].astype(BF16)
        wbf[:, 1 * dk:2 * dk] = wf_ref[...].astype(BF16)
        wbf[:, 2 * dk:3 * dk] = wi_ref[...].astype(BF16)
        wbf[:, 3 * dk:4 * dk] = wg_ref[...].astype(BF16)
        st[...] = jnp.zeros_like(st)

    tm = a_ref.shape[0]
    projs = [jnp.dot(a_ref[c0:c0 + chunk, :], wbf[...], preferred_element_type=F32)
             for c0 in range(0, tm, chunk)]
    lb = _lower_bound(p_ref)
    ng = ng_ref[...]
    chunks = []
    for proj in projs:
        q = _silu(proj[:, 0 * dk:1 * dk])
        logf, k = _hgrn_gates(proj[:, 1 * dk:2 * dk], lb)
        chunks.append((q, k, proj[:, 2 * dk:3 * dk], logf))
    outs, s_t = _hgrn_chunks(chunks, st[...], near_ref[...], far_ref[...])
    for j, (o, proj) in enumerate(zip(outs, projs)):
        o_ref[j * chunk:(j + 1) * chunk, :] = _group_norm_gate(o, proj[:, 3 * dk:4 * dk], ng).astype(o_ref.dtype)
    st[...] = s_t
    s_ref[0, 0] = s_t.T


def _hgrn_prompt(a, w_in, lower_bounds, norm_g, n_heads, dk):
    m, d = a.shape
    chunk = min(HG_CHUNK, m)
    tm = _pick(m, (HG_ROWS, 512, 256, 128, 64, 32, 16, 8))
    tm = max(tm, chunk)
    near, far = _level_tables(chunk)
    w_spec = lambda blk: pl.BlockSpec((d, dk), lambda h, i: (0, blk * n_heads + h))
    o, s = pl.pallas_call(
        functools.partial(_hgrn_prompt_kernel, chunk=chunk),
        out_shape=(jax.ShapeDtypeStruct((m, n_heads * dk), BF16),
                   jax.ShapeDtypeStruct((1, n_heads, dk, dk), F32)),
        grid=(n_heads, m // tm),
        in_specs=[pl.BlockSpec((tm, d), lambda h, i: (i, 0)),
                  w_spec(0), w_spec(1), w_spec(2), w_spec(3),
                  pl.BlockSpec((lower_bounds.shape[0], dk), lambda h, i: (0, h)),
                  pl.BlockSpec((1, dk), lambda h, i: (0, h)),
                  pl.BlockSpec(near.shape, lambda h, i: (0, 0)),
                  pl.BlockSpec(far.shape, lambda h, i: (0, 0))],
        out_specs=(pl.BlockSpec((tm, dk), lambda h, i: (i, h)),
                   pl.BlockSpec((1, 1, dk, dk), lambda h, i: (0, h, 0, 0))),
        scratch_shapes=[pltpu.VMEM((d, 4 * dk), BF16), pltpu.VMEM((dk, dk), F32)],
        compiler_params=_cparams(("arbitrary", "arbitrary"), V7X_VMEM_LIMIT),
        name="hgrn_prompt",
    )(a, w_in, w_in, w_in, w_in, lower_bounds, norm_g.reshape(1, -1), near, far)
    return o, s


def _hgrn_sample_kernel(qr_ref, fr_ref, ir_ref, gr_ref, p_ref, ng_ref, s0_ref, o_ref, s_ref, *, seq):
    bs = s0_ref.shape[0]
    rows = bs * seq
    lb = _lower_bound(p_ref)
    q = _silu(qr_ref[...])
    logf, k = _hgrn_gates(fr_ref[...], lb)
    v = ir_ref[...]
    row = lax.broadcasted_iota(jnp.int32, q.shape, 0)
    t = row % seq
    b = logf
    d = 1
    while d < seq:
        b = b + jnp.where(t >= d, pltpu.roll(b, d, 0), 0.0)
        d *= 2
    o = jnp.sum(q * k, axis=-1, keepdims=True) * v
    for d in range(1, seq):
        ok = t >= d
        arg = jnp.where(ok, b - pltpu.roll(b, d, 0), 0.0)
        w = jnp.sum(q * pltpu.roll(k, d, 0) * jnp.exp(arg), axis=-1, keepdims=True)
        o = o + jnp.where(ok, w, 0.0) * pltpu.roll(v, d, 0)
    qe = (q * jnp.exp(b)).astype(BF16)
    b_end = b
    d = 1
    while d < seq:
        b_end = jnp.where(t < seq - d, pltpu.roll(b_end, rows - d, 0), b_end)
        d *= 2
    k_end = k * jnp.exp(b_end - b)
    k_end_t = k_end.T
    dec_t = jnp.exp(b_end).T
    lane = lax.broadcasted_iota(jnp.int32, k_end_t.shape, 1)
    vb = v.astype(BF16)
    o_inter = []
    for s in range(bs):
        s0 = s0_ref[s, 0]
        in_seq = (lane >= s * seq) & (lane < (s + 1) * seq)
        upd = jnp.dot(jnp.where(in_seq, k_end_t, 0.0).astype(BF16), vb, preferred_element_type=F32)
        s_ref[s, 0] = s0 * dec_t[:, s * seq:s * seq + 1] + upd
        if s % (8 // seq) == 0:
            r0 = s * seq
            parts = []
        oi = jnp.dot(qe[r0:r0 + 8, :], s0.astype(BF16), preferred_element_type=F32)
        parts.append(oi)
        if s % (8 // seq) == (8 // seq) - 1:
            r8 = lax.broadcasted_iota(jnp.int32, oi.shape, 0) // seq
            acc = parts[0]
            for j in range(1, len(parts)):
                acc = jnp.where(r8 == j, parts[j], acc)
            o_inter.append(acc)
    o = o + jnp.concatenate(o_inter, axis=0)
    o_ref[...] = _group_norm_gate(o, gr_ref[...], ng_ref[...]).astype(o_ref.dtype)


def _hgrn_sample(proj, lower_bounds, norm_g, state, seq):
    b, n_heads, dk, dv = state.shape
    m = proj.shape[0]
    bs = _pick(b, (64, 32, 16, 8, 4, 2))
    rows = bs * seq
    col = lambda blk: pl.BlockSpec((rows, dk), lambda h, i: (i, blk * n_heads + h))
    o, s = pl.pallas_call(
        functools.partial(_hgrn_sample_kernel, seq=seq),
        out_shape=(jax.ShapeDtypeStruct((m, n_heads * dv), BF16),
                   jax.ShapeDtypeStruct(state.shape, F32)),
        grid=(n_heads, b // bs),
        in_specs=[col(0), col(1), col(2), col(3),
                  pl.BlockSpec((lower_bounds.shape[0], dk), lambda h, i: (0, h)),
                  pl.BlockSpec((1, dv), lambda h, i: (0, h)),
                  pl.BlockSpec((bs, 1, dk, dv), lambda h, i: (i, h, 0, 0))],
        out_specs=(pl.BlockSpec((rows, dv), lambda h, i: (i, h)),
                   pl.BlockSpec((bs, 1, dk, dv), lambda h, i: (i, h, 0, 0))),
        compiler_params=_cparams(("arbitrary", "arbitrary"), V7X_VMEM_LIMIT),
        name="hgrn_sample",
    )(proj, proj, proj, proj, lower_bounds, norm_g.reshape(1, -1), state)
    return o, s


def kernel(x_prompt, x_sample, cache_swa_k, cache_swa_v, state_hgrn, state_ffn_conv, c_prompt, c_sample,
           norm1_g, norm2_g, w_ada, b_ada, attn_w_qkv, attn_w_o, attn_sinks,
           hgrn_w_in, hgrn_lower_bounds, hgrn_norm_g, hgrn_w_o,
           ffn_w_in, ffn_conv_w, ffn_conv_b, ffn_w_out, final_norm_g):
    bp, lp, d = x_prompt.shape
    bsmp, ls, _ = x_sample.shape
    assert bp == 1, "the prompt kernels carry one sequence"
    depth = w_ada.shape[0]
    n_kv, hd = cache_swa_k.shape[2], cache_swa_k.shape[3]
    n_heads = attn_sinks.shape[0]
    dq = n_heads * hd
    dkv = n_kv * hd
    hg_heads, hg_dk = state_hgrn.shape[1], state_hgrn.shape[2]
    keep = min(WINDOW, lp)

    ns = bsmp * ls
    assert ns % 8 == 0
    c_all = jnp.concatenate([jnp.repeat(c_sample, ls, axis=0), c_prompt, jnp.zeros((7, d), F32)], axis=0)
    mod = _ada(c_all, w_ada, b_ada)
    pk = dict(sample=False, n_sample=ns)
    sk = dict(sample=True, n_sample=ns)

    xp = x_prompt.reshape(lp, d)
    xs = x_sample.reshape(ns, d)
    conv_p, conv_s = [], []
    outs = {}
    w_out_bf = ffn_w_out.astype(BF16)
    hp = None
    hs = _modulate(xs, norm1_g, mod, 0, 0, 1, True, ns)
    for l in range(depth):
        if l % 2 == 0:
            if hp is None:
                qkv_p = _matmul_mod(xp, norm1_g, mod, l, 0, 1, attn_w_qkv, ns, tm=1024, tn=1024,
                                    name="qkv_prompt")
            else:
                qkv_p = _matmul(hp, attn_w_qkv, tn=1024, name="qkv_prompt")
            qkv_s = _matmul(hs, attn_w_qkv, name="qkv_sample")
            op = _swa_prompt(qkv_p, attn_sinks, n_kv, hd)
            outs['swa_k_prompt'] = qkv_p[lp - keep:, dq:dq + dkv].reshape(bp, keep, n_kv, hd)
            outs['swa_v_prompt'] = qkv_p[lp - keep:, dq + dkv:].reshape(bp, keep, n_kv, hd)
            outs['swa_k_sample'] = qkv_s[:, dq:dq + dkv].reshape(bsmp, ls, n_kv, hd)
            outs['swa_v_sample'] = qkv_s[:, dq + dkv:].reshape(bsmp, ls, n_kv, hd)
            os_ = _swa_sample(qkv_s[:, :dq], qkv_s[:, dq:dq + dkv], qkv_s[:, dq + dkv:],
                              cache_swa_k, cache_swa_v, attn_sinks, n_kv, hd, ls)
            w_o = attn_w_o
        else:
            if hp is None:
                hp = _modulate(xp, norm1_g, mod, l, 0, 1, False, ns)
            op, sp = _hgrn_prompt(hp, hgrn_w_in, hgrn_lower_bounds, hgrn_norm_g, hg_heads, hg_dk)
            proj_s = _matmul(hs, hgrn_w_in, name="hgrn_in_sample")
            os_, ss = _hgrn_sample(proj_s, hgrn_lower_bounds, hgrn_norm_g, state_hgrn, ls)
            outs['hgrn_state_prompt'] = sp
            outs['hgrn_state_sample'] = ss
            w_o = hgrn_w_o
        xp, hp = _mm_rows(op, w_o, xp, mod, gate=(l, 2), norm_g=norm2_g, norm_layer=l, next_mod=(l, 3, 4),
                          tm=512, name="mix_out_prompt", **pk)
        xs, hs = _mm_rows(os_, w_o, xs, mod, gate=(l, 2), norm_g=norm2_g, norm_layer=l, next_mod=(l, 3, 4),
                          tm=512, name="mix_out_sample", **sk)
        actp, bufp = _ffn_in_prompt(hp, ffn_w_in, ffn_conv_w, ffn_conv_b, l)
        acts, bufs = _ffn_in_sample(hs, ffn_w_in, ffn_conv_w, ffn_conv_b, l, state_ffn_conv, ls)
        conv_p.append(bufp[None])
        conv_s.append(bufs)
        if l + 1 < depth:
            nxt = dict(norm_g=norm1_g, norm_layer=l + 1, next_mod=(l + 1, 0, 1))
        else:
            nxt = dict(norm_g=final_norm_g, norm_layer=0, next_mod=None, keep_x=False, out_dtype=F32)
        xp, hp = _mm_rows(actp, w_out_bf, xp, mod, gate=(l, 5), w_layer=l, tm=256, name="ffn_out_prompt",
                          **nxt, **pk)
        xs, hs = _mm_rows(acts, w_out_bf, xs, mod, gate=(l, 5), w_layer=l, tm=256, name="ffn_out_sample",
                          **nxt, **sk)
    y_prompt = hp.reshape(bp, lp, d)
    y_sample = hs.reshape(bsmp, ls, d)
    return (y_prompt, y_sample,
            outs['swa_k_prompt'], outs['swa_v_prompt'], outs['swa_k_sample'], outs['swa_v_sample'],
            outs['hgrn_state_prompt'], outs['hgrn_state_sample'],
            jnp.stack(conv_p), jnp.stack(conv_s))
```

```python
import functools
import math

import numpy as np
import jax
import jax.numpy as jnp
from jax import lax
from jax.experimental import pallas as pl
from jax.experimental.pallas import tpu as pltpu

F32 = jnp.float32
BF16 = jnp.bfloat16

EPS = 1e-6
LOG2E = math.log2(math.e)
WINDOW = 128
CONV_W = 3
LANES = 128
HG_CHUNK = 256
HG_ROWS = 2048
SWA_WAVE = 8
V7X_VMEM_LIMIT = 56 * 1024 * 1024


def _cparams(sem, vmem=None):
    return pltpu.CompilerParams(dimension_semantics=sem, vmem_limit_bytes=vmem)


def _pick(n, cands):
    for c in cands:
        if n % c == 0:
            return c
    return n


def _silu(x):
    return x * jax.nn.sigmoid(x)


def _logaddexp(a, b):
    return jnp.maximum(a, b) + jnp.log1p(jnp.exp(-jnp.abs(a - b)))


def _ada_kernel(c_ref, w_ref, b_ref, o_ref, act, slab, *, n_seq, seq):
    @pl.when((pl.program_id(0) == 0) & (pl.program_id(1) == 0))
    def _():
        act[...] = _silu(c_ref[...]).astype(BF16)

    w = w_ref[0].astype(BF16)
    res = jnp.dot(act[...], w, preferred_element_type=F32) + b_ref[0]
    for c in range(slab.shape[0]):
        cs = slice(c * LANES, (c + 1) * LANES)
        for t in range(seq):
            slab[c, pl.ds(t, n_seq, stride=seq), :] = res[:n_seq, cs]
        slab[c, n_seq * seq:, :] = res[n_seq:, cs]
        o_ref[0, :, cs] = slab[c]


def _ada(c_seq, w_ada, b_ada, seq):
    depth, d, n = w_ada.shape
    r = c_seq.shape[0]
    n_seq = r - 8
    rows = n_seq * seq + 8
    tn = _pick(n, (1024, 512, 256, 128))
    return pl.pallas_call(
        functools.partial(_ada_kernel, n_seq=n_seq, seq=seq),
        out_shape=jax.ShapeDtypeStruct((depth, rows, n), F32),
        grid=(depth, n // tn),
        in_specs=[pl.BlockSpec((r, d), lambda l, j: (0, 0)),
                  pl.BlockSpec((1, d, tn), lambda l, j: (l, 0, j)),
                  pl.BlockSpec((1, 1, tn), lambda l, j: (l, 0, j))],
        out_specs=pl.BlockSpec((1, rows, tn), lambda l, j: (l, 0, j)),
        scratch_shapes=[pltpu.VMEM((r, d), BF16), pltpu.VMEM((tn // LANES, rows, LANES), F32)],
        compiler_params=_cparams(("arbitrary", "arbitrary"), V7X_VMEM_LIMIT),
        name="ada",
    )(c_seq, w_ada, b_ada.reshape(depth, 1, n))


def _mod_rows(sample, tm, n_sample):
    if sample:
        return tm, (lambda i: i)
    return 8, (lambda i: n_sample // 8)


def _modulate_kernel(x_ref, g_ref, sc_ref, sh_ref, o_ref, *, sample):
    x = x_ref[...]
    ms = jnp.mean(x * x, axis=-1, keepdims=True)
    y = x * lax.rsqrt(ms + EPS) * g_ref[...]
    sc = sc_ref[...] if sample else sc_ref[0:1, :]
    sh = sh_ref[...] if sample else sh_ref[0:1, :]
    o_ref[...] = (y * (1.0 + sc) + sh).astype(o_ref.dtype)


def _modulate(x, g_all, mod, l, shift_col, scale_col, sample, n_sample):
    m, d = x.shape
    tm = _pick(m, (512, 256, 128, 64, 32, 16, 8))
    rows, rmap = _mod_rows(sample, tm, n_sample)
    mspec = lambda col: pl.BlockSpec((None, rows, d), lambda i: (l, rmap(i), col))
    return pl.pallas_call(
        functools.partial(_modulate_kernel, sample=sample),
        out_shape=jax.ShapeDtypeStruct((m, d), BF16),
        grid=(m // tm,),
        in_specs=[pl.BlockSpec((tm, d), lambda i: (i, 0)),
                  pl.BlockSpec((None, 1, d), lambda i: (l, 0, 0)),
                  mspec(scale_col), mspec(shift_col)],
        out_specs=pl.BlockSpec((tm, d), lambda i: (i, 0)),
        compiler_params=_cparams(("arbitrary",), V7X_VMEM_LIMIT),
        name="modulate",
    )(x, g_all.reshape(g_all.shape[0], 1, d), mod, mod)


def _mm_kernel(a_ref, w_ref, o_ref, wbf):
    @pl.when(pl.program_id(1) == 0)
    def _():
        wbf[...] = w_ref[...].astype(BF16)

    o_ref[...] = jnp.dot(a_ref[...], wbf[...], preferred_element_type=F32)


def _mm_mod_kernel(x_ref, g_ref, sc_ref, sh_ref, w_ref, o_ref, wbf):
    @pl.when(pl.program_id(1) == 0)
    def _():
        wbf[...] = w_ref[...].astype(BF16)

    x = x_ref[...]
    y = x * lax.rsqrt(jnp.mean(x * x, axis=-1, keepdims=True) + EPS) * g_ref[...]
    h = (y * (1.0 + sc_ref[0:1, :]) + sh_ref[0:1, :]).astype(BF16)
    o_ref[...] = jnp.dot(h, wbf[...], preferred_element_type=F32)


def _matmul_mod(x, g_all, mod, l, shift_col, scale_col, w, n_sample, *, tm, tn, name):
    m, d = x.shape
    n = w.shape[-1]
    tm, tn = min(tm, m), min(tn, n)
    assert m % tm == 0 and n % tn == 0
    rows, rmap = _mod_rows(False, tm, n_sample)
    mspec = lambda col: pl.BlockSpec((None, rows, d), lambda j, i: (l, rmap(i), col))
    return pl.pallas_call(
        _mm_mod_kernel,
        out_shape=jax.ShapeDtypeStruct((m, n), F32),
        grid=(n // tn, m // tm),
        in_specs=[pl.BlockSpec((tm, d), lambda j, i: (i, 0)),
                  pl.BlockSpec((None, 1, d), lambda j, i: (l, 0, 0)),
                  mspec(scale_col), mspec(shift_col),
                  pl.BlockSpec((d, tn), lambda j, i: (0, j))],
        out_specs=pl.BlockSpec((tm, tn), lambda j, i: (i, j)),
        scratch_shapes=[pltpu.VMEM((d, tn), BF16)],
        compiler_params=_cparams(("arbitrary", "arbitrary"), V7X_VMEM_LIMIT),
        name=name,
    )(x, g_all.reshape(g_all.shape[0], 1, d), mod, mod, w)


def _matmul(a, w, *, tm=None, tn=None, name="matmul"):
    m, k = a.shape
    n = w.shape[-1]
    tm = min(tm, m) if tm else _pick(m, (1024, 512, 256, 128, 64, 32, 16, 8))
    tn = min(tn, n) if tn else _pick(n, (512, 256, 128))
    assert m % tm == 0 and n % tn == 0
    return pl.pallas_call(
        _mm_kernel,
        out_shape=jax.ShapeDtypeStruct((m, n), F32),
        grid=(n // tn, m // tm),
        in_specs=[pl.BlockSpec((tm, k), lambda j, i: (i, 0)),
                  pl.BlockSpec((k, tn), lambda j, i: (0, j))],
        out_specs=pl.BlockSpec((tm, tn), lambda j, i: (i, j)),
        scratch_shapes=[pltpu.VMEM((k, tn), BF16)],
        compiler_params=_cparams(("arbitrary", "arbitrary"), V7X_VMEM_LIMIT),
        name=name,
    )(a, w)


def _mm_rows_kernel(*refs, cast_w, sample, modulate, keep_x):
    it = iter(refs)
    a_ref, w_ref, x_ref, gate_ref, g_ref = (next(it) for _ in range(5))
    sc_ref, sh_ref = (next(it), next(it)) if modulate else (None, None)
    xo_ref = next(it) if keep_x else None
    h_ref = next(it)
    if cast_w:
        wbf = next(it)

        @pl.when(pl.program_id(0) == 0)
        def _():
            wbf[...] = w_ref[...].astype(BF16)

        w = wbf[...]
    else:
        w = w_ref[...]
    rows = (lambda r: r[...]) if sample else (lambda r: r[0:1, :])
    x = x_ref[...] + rows(gate_ref) * jnp.dot(a_ref[...], w, preferred_element_type=F32)
    if keep_x:
        xo_ref[...] = x
    y = x * lax.rsqrt(jnp.mean(x * x, axis=-1, keepdims=True) + EPS) * g_ref[...]
    if modulate:
        y = y * (1.0 + rows(sc_ref)) + rows(sh_ref)
    h_ref[...] = y.astype(h_ref.dtype)


def _mm_rows(a, w, x, mod, *, gate, norm_g, norm_layer, next_mod=None, w_layer=None, sample, n_sample,
             tm, keep_x=True, out_dtype=None, name):
    m, k = a.shape
    n = w.shape[-1]
    tm = min(tm, m)
    assert m % tm == 0
    cast_w = w.dtype != BF16
    rows, rmap = _mod_rows(sample, tm, n_sample)
    once = pl.Buffered(1)
    if w_layer is None:
        w_spec = pl.BlockSpec((k, n), lambda i: (0, 0), pipeline_mode=once)
    else:
        w_spec = pl.BlockSpec((None, k, n), lambda i: (w_layer, 0, 0), pipeline_mode=once)
    mspec = lambda layer, col: pl.BlockSpec((None, rows, n), lambda i: (layer, rmap(i), col))
    in_specs = [pl.BlockSpec((tm, k), lambda i: (i, 0)), w_spec,
                pl.BlockSpec((tm, n), lambda i: (i, 0)),
                mspec(*gate),
                pl.BlockSpec((None, 1, n), lambda i: (norm_layer, 0, 0))]
    args = [a, w, x, mod, norm_g.reshape(-1, 1, n)]
    if next_mod is not None:
        in_specs += [mspec(next_mod[0], next_mod[2]), mspec(next_mod[0], next_mod[1])]
        args += [mod, mod]
    out_shape, out_specs = [], []
    if keep_x:
        out_shape.append(jax.ShapeDtypeStruct((m, n), F32))
        out_specs.append(pl.BlockSpec((tm, n), lambda i: (i, 0)))
    out_shape.append(jax.ShapeDtypeStruct((m, n), out_dtype or BF16))
    out_specs.append(pl.BlockSpec((tm, n), lambda i: (i, 0)))
    res = pl.pallas_call(
        functools.partial(_mm_rows_kernel, cast_w=cast_w, sample=sample, modulate=next_mod is not None,
                          keep_x=keep_x),
        out_shape=tuple(out_shape),
        grid=(m // tm,),
        in_specs=in_specs,
        out_specs=tuple(out_specs),
        scratch_shapes=[pltpu.VMEM((k, n), BF16)] if cast_w else [],
        compiler_params=_cparams(("arbitrary",), V7X_VMEM_LIMIT),
        name=name,
    )(*args)
    return res if keep_x else (None, res[0])


def _gelu_exact(a):
    return 0.5 * a * (1.0 + lax.erf(a * (1.0 / math.sqrt(2.0))))


def _ffn_in_prompt_kernel(a_ref, wu_ref, wg_ref, cw_ref, cb_ref, act_ref, tail_ref, wbf, carry):
    i = pl.program_id(1)
    tf = wu_ref.shape[1]

    @pl.when(i == 0)
    def _():
        wbf[:, :tf] = wu_ref[...].astype(BF16)
        wbf[:, tf:] = wg_ref[...].astype(BF16)
        carry[...] = jnp.zeros_like(carry)

    ug = jnp.dot(a_ref[...], wbf[...], preferred_element_type=F32)
    u = ug[:, :tf]
    g = ug[:, tf:]
    tm = u.shape[0]
    row = lax.broadcasted_iota(jnp.int32, u.shape, 0)
    c = carry[...]
    u1 = jnp.where(row == 0, c[7:8, :], pltpu.roll(u, 1, 0))
    u2 = jnp.where(row == 0, c[6:7, :], jnp.where(row == 1, c[7:8, :], pltpu.roll(u, 2, 0)))
    cw = cw_ref[...]
    conv = cw[0:1, :] * u2 + cw[1:2, :] * u1 + cw[2:3, :] * u + cb_ref[...]
    act_ref[...] = (_gelu_exact(conv) * g).astype(act_ref.dtype)
    carry[...] = u[tm - 8:, :]
    tail_ref[...] = u[tm - 8:, :]


def _ffn_in_sample_kernel(a_ref, wu_ref, wg_ref, cw_ref, cb_ref, buf_ref, act_ref, nbuf_ref,
                          us, gs, acts, *, seq):
    a = a_ref[...]
    b = buf_ref.shape[0]
    u_all = jnp.dot(a, wu_ref[...].astype(BF16), preferred_element_type=F32)
    g_all = jnp.dot(a, wg_ref[...].astype(BF16), preferred_element_type=F32)
    cw = cw_ref[...]
    cb = cb_ref[...]
    for c in range(us.shape[0]):
        cs = slice(c * LANES, (c + 1) * LANES)
        us[c] = u_all[:, cs]
        gs[c] = g_all[:, cs]
        prev2, prev1 = buf_ref[:, 0, cs], buf_ref[:, 1, cs]
        for t in range(seq):
            u = us[c, pl.ds(t, b, stride=seq), :]
            conv = cw[0:1, cs] * prev2 + cw[1:2, cs] * prev1 + cw[2:3, cs] * u + cb[:, cs]
            acts[c, pl.ds(t, b, stride=seq), :] = _gelu_exact(conv) * gs[c, pl.ds(t, b, stride=seq), :]
            prev2, prev1 = prev1, u
        act_ref[:, cs] = acts[c].astype(act_ref.dtype)
        nbuf_ref[:, 0, cs] = prev2
        nbuf_ref[:, 1, cs] = prev1


def _ffn_in_prompt(a, w_in, conv_w, conv_b, l):
    m, d = a.shape
    f = w_in.shape[-1] // 2
    tf = _pick(f, (512, 256, 128))
    tm = _pick(m, (1024, 512, 256, 128, 64, 32, 16, 8))
    nf = f // tf
    act, tail = pl.pallas_call(
        _ffn_in_prompt_kernel,
        out_shape=(jax.ShapeDtypeStruct((m, f), BF16), jax.ShapeDtypeStruct((8, f), F32)),
        grid=(nf, m // tm),
        in_specs=[pl.BlockSpec((tm, d), lambda j, i: (i, 0)),
                  pl.BlockSpec((None, d, tf), lambda j, i: (l, 0, j)),
                  pl.BlockSpec((None, d, tf), lambda j, i: (l, 0, nf + j)),
                  pl.BlockSpec((None, CONV_W, tf), lambda j, i: (l, 0, j)),
                  pl.BlockSpec((None, 1, tf), lambda j, i: (l, 0, j))],
        out_specs=(pl.BlockSpec((tm, tf), lambda j, i: (i, j)),
                   pl.BlockSpec((8, tf), lambda j, i: (0, j))),
        scratch_shapes=[pltpu.VMEM((d, 2 * tf), BF16), pltpu.VMEM((8, tf), F32)],
        compiler_params=_cparams(("arbitrary", "arbitrary"), V7X_VMEM_LIMIT),
        name="ffn_in_prompt",
    )(a, w_in, w_in, conv_w, conv_b.reshape(conv_b.shape[0], 1, f))
    return act, tail[6:8]


def _ffn_in_sample(a, w_in, conv_w, conv_b, l, state, seq):
    m, d = a.shape
    f = w_in.shape[-1] // 2
    b = state.shape[1]
    tf = _pick(f, (512, 256, 128))
    nf = f // tf
    return pl.pallas_call(
        functools.partial(_ffn_in_sample_kernel, seq=seq),
        out_shape=(jax.ShapeDtypeStruct((m, f), BF16), jax.ShapeDtypeStruct((b, CONV_W - 1, f), F32)),
        grid=(nf,),
        in_specs=[pl.BlockSpec((m, d), lambda j: (0, 0)),
                  pl.BlockSpec((None, d, tf), lambda j: (l, 0, j)),
                  pl.BlockSpec((None, d, tf), lambda j: (l, 0, nf + j)),
                  pl.BlockSpec((None, CONV_W, tf), lambda j: (l, 0, j)),
                  pl.BlockSpec((None, 1, tf), lambda j: (l, 0, j)),
                  pl.BlockSpec((None, b, CONV_W - 1, tf), lambda j: (l, 0, 0, j))],
        out_specs=(pl.BlockSpec((m, tf), lambda j: (0, j)),
                   pl.BlockSpec((b, CONV_W - 1, tf), lambda j: (0, 0, j))),
        scratch_shapes=[pltpu.VMEM((tf // LANES, m, LANES), F32)] * 3,
        compiler_params=_cparams(("arbitrary",), V7X_VMEM_LIMIT),
        name="ffn_in_sample",
    )(a, w_in, w_in, conv_w, conv_b.reshape(conv_b.shape[0], 1, f), state)


def _alibi_slope(h, n_heads):
    return 2.0 ** (-8.0 * (h + 1) / n_heads)


def _swa_prompt_bias(n_heads, tq):
    t = np.arange(tq)[:, None]
    s = np.arange(2 * tq)[None, :]
    dist = t + tq - s
    ok = (dist >= 0) & (dist < WINDOW)
    slope = (2.0 ** (-8.0 * (np.arange(n_heads) + 1) / n_heads) * LOG2E)[:, None, None]
    table = np.stack([np.where(ok & (s >= tq), slope * s, -np.inf), np.where(ok, slope * s, -np.inf)])
    return jnp.asarray(table, F32)


def _swa_prompt_kernel(sink_ref, q_ref, kp_ref, kc_ref, vp_ref, vc_ref, bias_ref, o_ref, *, n_kv, group, hd):
    n_heads = n_kv * group
    tq = q_ref.shape[0]
    trow = (lax.broadcasted_iota(jnp.int32, (tq, 1), 0) + tq).astype(F32)
    qscale = hd ** -0.5 * LOG2E
    nt = (((1,), (1,)), ((), ()))
    def scores(kv):
        ks = slice(kv * hd, (kv + 1) * hd)
        k2 = jnp.concatenate([kp_ref[:, ks], kc_ref[:, ks]], axis=0).astype(BF16)
        q4 = jnp.concatenate([(q_ref[:, h * hd:(h + 1) * hd] * qscale).astype(BF16)
                              for h in range(kv * group, (kv + 1) * group)], axis=0)
        return lax.dot_general(q4, k2, nt, preferred_element_type=F32)

    for kv0 in range(0, n_kv, SWA_WAVE):
        kvs = list(range(kv0, min(kv0 + SWA_WAVE, n_kv)))
        sc4s = [scores(kv) for kv in kvs]
        heads = [kv * group + g for kv in kvs for g in range(group)]
        slopes = [_alibi_slope(h, n_heads) * LOG2E for h in heads]
        scs = [sc4s[j // group][(j % group) * tq:(j % group + 1) * tq] + bias_ref[h]
               for j, h in enumerate(heads)]
        sinks = [sink_ref[h] * LOG2E + slopes[j] * trow for j, h in enumerate(heads)]
        mxs = [jnp.maximum(jnp.max(scs[j], axis=-1, keepdims=True), sinks[j]) for j in range(len(heads))]
        es = [jnp.exp2(scs[j] - mxs[j]) for j in range(len(heads))]
        dens = [jnp.sum(es[j], axis=-1, keepdims=True) + jnp.exp2(sinks[j] - mxs[j])
                for j in range(len(heads))]
        es = [e.astype(BF16) for e in es]
        o4s = []
        for i, kv in enumerate(kvs):
            ks = slice(kv * hd, (kv + 1) * hd)
            v2 = jnp.concatenate([vp_ref[:, ks], vc_ref[:, ks]], axis=0).astype(BF16)
            o4s.append(jnp.dot(jnp.concatenate(es[i * group:(i + 1) * group], axis=0), v2,
                               preferred_element_type=F32))
        for j, h in enumerate(heads):
            o = o4s[j // group][(j % group) * tq:(j % group + 1) * tq]
            o_ref[:, h * hd:(h + 1) * hd] = (o / dens[j]).astype(o_ref.dtype)


def _swa_prompt(qkv, sinks, n_kv, hd):
    l = qkv.shape[0]
    n_heads = sinks.shape[0]
    group = n_heads // n_kv
    dq = n_heads * hd
    dk = n_kv * hd
    nb = l // WINDOW
    kblk = dq // dk
    bias = _swa_prompt_bias(n_heads, WINDOW)
    return pl.pallas_call(
        functools.partial(_swa_prompt_kernel, n_kv=n_kv, group=group, hd=hd),
        out_shape=jax.ShapeDtypeStruct((l, dq), BF16),
        grid_spec=pltpu.PrefetchScalarGridSpec(
            num_scalar_prefetch=1,
            grid=(nb,),
            in_specs=[pl.BlockSpec((WINDOW, dq), lambda i, s: (i, 0)),
                      pl.BlockSpec((WINDOW, dk), lambda i, s: (jnp.maximum(i - 1, 0), kblk)),
                      pl.BlockSpec((WINDOW, dk), lambda i, s: (i, kblk)),
                      pl.BlockSpec((WINDOW, dk), lambda i, s: (jnp.maximum(i - 1, 0), kblk + 1)),
                      pl.BlockSpec((WINDOW, dk), lambda i, s: (i, kblk + 1)),
                      pl.BlockSpec((None,) + bias.shape[1:], lambda i, s: (jnp.minimum(i, 1), 0, 0, 0))],
            out_specs=pl.BlockSpec((WINDOW, dq), lambda i, s: (i, 0)),
        ),
        compiler_params=_cparams(("arbitrary",), V7X_VMEM_LIMIT),
        name="swa_prompt",
    )(sinks, qkv, qkv, qkv, qkv, qkv, bias)


def _swa_sample_tables(n_heads, n_kv, hd, seq, cr):
    group = n_heads // n_kv
    r = np.arange(seq * n_heads)
    t, h = r // n_heads, r % n_heads
    slope = 2.0 ** (-8.0 * (h + 1) / n_heads)
    slot = (np.arange(n_kv * hd)[None, :] // hd == (h // group)[:, None]).astype(np.float32)
    dist = cr + t[:, None] - np.arange(cr)[None, :]
    bias_c = np.where((dist >= 0) & (dist < WINDOW), -slope[:, None] * dist, -np.inf) * LOG2E
    j = np.arange(LANES)
    distn = t[:, None] - j[None, :]
    bias_n = np.where((distn >= 0) & (j[None, :] < seq), -slope[:, None] * distn, -np.inf) * LOG2E
    return jnp.asarray(slot, F32), jnp.asarray(np.concatenate([bias_c, bias_n], axis=1), F32)


def _swa_sample_kernel(q_ref, kt_ref, vt_ref, kn_ref, vn_ref, slot_ref, bias_ref, sink_ref, o_ref, *, hd):
    bs = q_ref.shape[0]
    cr = kt_ref.shape[2]
    width = slot_ref.shape[1]
    qscale = hd ** -0.5 * LOG2E
    slot = slot_ref[...]
    bias = bias_ref[...]
    sink2 = sink_ref[...] * LOG2E
    nt = (((1,), (1,)), ((), ()))

    def slotted(q):
        return jnp.concatenate([q] * (width // hd), axis=1) * slot

    qs = [slotted(q_ref[b] * qscale) for b in range(bs)]
    qb = [q.astype(BF16) for q in qs]
    zrows = jnp.zeros((LANES - kn_ref.shape[1], width), F32)
    kn = [jnp.concatenate([kn_ref[b], zrows], axis=0).astype(BF16) for b in range(bs)]
    vn = [jnp.concatenate([vn_ref[b], zrows], axis=0).astype(BF16) for b in range(bs)]
    sc = [jnp.concatenate([jnp.dot(qb[b], kt_ref[b].astype(BF16), preferred_element_type=F32),
                           lax.dot_general(qb[b], kn[b], nt, preferred_element_type=F32)], axis=1) + bias
          for b in range(bs)]
    mx = [jnp.maximum(jnp.max(s, axis=-1, keepdims=True), sink2) for s in sc]
    es = [jnp.exp2(s - m) for s, m in zip(sc, mx)]
    den = [jnp.sum(e, axis=-1, keepdims=True) + jnp.exp2(sink2 - m) for e, m in zip(es, mx)]
    es = [e.astype(BF16) for e in es]
    o_w = [lax.dot_general(es[b][:, :cr], vt_ref[b].astype(BF16), nt, preferred_element_type=F32)
           + jnp.dot(es[b][:, cr:], vn[b], preferred_element_type=F32) for b in range(bs)]
    for b in range(bs):
        o = o_w[b] * slot
        o = functools.reduce(jnp.add, [o[:, c * hd:(c + 1) * hd] for c in range(width // hd)])
        o_ref[b] = (o / den[b]).astype(o_ref.dtype)


def _swa_sample(q, k_new, v_new, cache_k, cache_v, sinks, n_kv, hd, seq):
    b, cr = cache_k.shape[0], cache_k.shape[1]
    n_heads = sinks.shape[0]
    rows = seq * n_heads
    width = n_kv * hd
    slot, bias = _swa_sample_tables(n_heads, n_kv, hd, seq, cr)
    kt = cache_k.transpose(0, 2, 3, 1).reshape(b, width, cr)
    vt = cache_v.transpose(0, 2, 3, 1).reshape(b, width, cr)
    new_rows = -(-seq // 8) * 8
    pad_new = lambda x: jnp.pad(x.reshape(b, seq, width), ((0, 0), (0, new_rows - seq), (0, 0)))
    bs = _pick(b, (8, 4, 2, 1))
    const = lambda shape: pl.BlockSpec(shape, lambda i: (0, 0))
    o = pl.pallas_call(
        functools.partial(_swa_sample_kernel, hd=hd),
        out_shape=jax.ShapeDtypeStruct((b, rows, hd), BF16),
        grid=(b // bs,),
        in_specs=[pl.BlockSpec((bs, rows, hd), lambda i: (i, 0, 0)),
                  pl.BlockSpec((bs, width, cr), lambda i: (i, 0, 0)),
                  pl.BlockSpec((bs, width, cr), lambda i: (i, 0, 0)),
                  pl.BlockSpec((bs, new_rows, width), lambda i: (i, 0, 0)),
                  pl.BlockSpec((bs, new_rows, width), lambda i: (i, 0, 0)),
                  const(slot.shape), const(bias.shape), const((rows, 1))],
        out_specs=pl.BlockSpec((bs, rows, hd), lambda i: (i, 0, 0)),
        compiler_params=_cparams(("arbitrary",), V7X_VMEM_LIMIT),
        name="swa_sample",
    )(q.reshape(b, rows, hd), kt, vt, pad_new(k_new), pad_new(v_new),
      slot, bias, jnp.tile(sinks, seq).reshape(rows, 1))
    return o.reshape(b * seq, n_heads * hd)


def _lower_bound(p_ref):
    p0 = p_ref[0:1, :]
    p1 = p_ref[1:2, :]
    mx = jnp.maximum(p0, p1)
    e0 = jnp.exp(p0 - mx)
    e1 = jnp.exp(p1 - mx)
    return e1 / (e0 + e1)


def _hgrn_gates(fr, lb):
    e = jnp.exp(-jnp.abs(fr))
    log_sig = jnp.minimum(fr, 0.0) - jnp.log1p(e)
    logf = _logaddexp(jnp.log(lb), jnp.log1p(-lb) + log_sig)
    r = 1.0 / (1.0 + e)
    k = (1.0 - lb) * jnp.where(fr >= 0.0, e * r, r)
    return logf, k


def _group_norm_gate(o, gr, ng):
    o = o * lax.rsqrt(jnp.mean(o * o, axis=-1, keepdims=True) + EPS) * ng
    return o * _silu(gr)


SMALL = 8

def _level_tables(c):
    g = min(c, 128)
    t = np.arange(g)[:, None]
    s = np.arange(g)[None, :]
    hb = np.floor(np.log2(np.maximum(t ^ s, 1))).astype(np.int32)
    same = (t // SMALL) == (s // SMALL)
    near = np.where(same & (s < t), hb, np.where(s == t, int(math.log2(SMALL)), -1))
    i = np.arange(c // 2)[:, None]
    j = np.arange(c // 2)[None, :]
    far = np.where(i == j, -1, np.floor(np.log2(np.maximum(i ^ j, 1)))).astype(np.int32)
    return jnp.asarray(near, dtype=jnp.int32), jnp.asarray(far, dtype=jnp.int32)


def _group_ref_rows(p, m, row):
    c, w = p.shape
    if m == 1:
        return jnp.where((row & 1) == 1, pltpu.roll(p, 1, 0), p)
    if m == 2:
        r4 = row & 3
        return jnp.where(r4 == 0, pltpu.roll(p, c - 1, 0),
                         jnp.where(r4 == 1, p,
                                   jnp.where(r4 == 2, pltpu.roll(p, 1, 0), pltpu.roll(p, 2, 0))))
    p3 = p.reshape(c // 8, 8, w)
    return jnp.broadcast_to(p3[:, 3:4, :], (c // 8, 8, w)).reshape(c, w)


def _halves(x, m):
    c = x.shape[0]
    ev = [x[r:r + m] for r in range(0, c, 2 * m)]
    od = [x[r + m:r + 2 * m] for r in range(0, c, 2 * m)]
    cat = lambda xs: xs[0] if len(xs) == 1 else jnp.concatenate(xs, axis=0)
    return cat(ev), cat(od)


def _interleave(ev, od, m):
    parts = []
    for r in range(0, ev.shape[0], m):
        parts += [ev[r:r + m], od[r:r + m]]
    return jnp.concatenate(parts, axis=0)


def _hgrn_chunks(chunks, st, near, far):
    c, w = chunks[0][0].shape
    nlev = int(math.log2(c))
    nsmall = int(math.log2(SMALL))
    g = near.shape[0]
    row = lax.broadcasted_iota(jnp.int32, (c, w), 0)
    nt = (((1,), (1,)), ((), ()))
    dot_nt = lambda x, y: lax.dot_general(x, y, nt, preferred_element_type=F32)
    dot_nn = lambda x, y: jnp.dot(x, y, preferred_element_type=F32)

    near_ops, far_ops, tails = [], [], []
    for q, k, v, logf in chunks:
        p = logf
        qs, ks = [q.astype(BF16)], [k.astype(BF16)]
        for l in range(nsmall):
            gref = _group_ref_rows(p, 1 << l, row)
            odd = ((row >> l) & 1) == 1
            e = jnp.exp(jnp.where(odd, p, gref - p))
            qs.append((q * e).astype(BF16))
            ks.append((k * e).astype(BF16))
            p = p + jnp.where(odd, gref, 0.0)
        near_ops.append((qs, ks, v.astype(BF16)))
        lev = []
        for l in range(nsmall, nlev):
            m = 1 << l
            pe, po = _halves(p, m)
            tot = jnp.concatenate([jnp.broadcast_to(pe[r + m - 1:r + m], (m, w))
                                   for r in range(0, c // 2, m)], axis=0)
            q_od = _halves(q, m)[1]
            k_ev = _halves(k, m)[0]
            v_ev = _halves(v, m)[0].astype(BF16)
            lev.append(((q_od * jnp.exp(po)).astype(BF16), (k_ev * jnp.exp(tot - pe)).astype(BF16), v_ev))
            p = _interleave(pe, po + tot, m)
        far_ops.append(lev)
        b = p
        b_end = b[c - 1:c, :]
        tails.append(((q * jnp.exp(b)).astype(BF16), (k * jnp.exp(b_end - b)).astype(BF16),
                      jnp.exp(b_end)))

    near_sc = [[[dot_nt(qs[j][r:r + g], ks[j][r:r + g]) for j in range(nsmall + 1)]
                for r in range(0, c, g)] for qs, ks, _ in near_ops]
    far_sc = [[dot_nt(qt, kt) for qt, kt, _ in lev] for lev in far_ops]
    upds = [lax.dot_general(vb, k_end, (((0,), (0,)), ((), ())), preferred_element_type=F32)
            for (_, _, vb), (_, k_end, _) in zip(near_ops, tails)]

    near_a = []
    for sc_chunk in near_sc:
        tiles = []
        for sc in sc_chunk:
            a = jnp.where(near == nsmall, sc[0], 0.0)
            for l in range(nsmall):
                a = jnp.where(near == l, sc[l + 1], a)
            tiles.append(a.astype(BF16))
        near_a.append(tiles)
    far_a = [[(jnp.where(far < nsmall + j, a, 0.0) if (1 << (nsmall + j)) < c // 2 else a).astype(BF16)
              for j, a in enumerate(lev)] for lev in far_sc]
    states = [st]
    for (_, _, dec), upd in zip(tails, upds):
        states.append(states[-1] * dec + upd)

    near_o = [[dot_nn(a, vb[i * g:(i + 1) * g]) for i, a in enumerate(tiles)]
              for tiles, (_, _, vb) in zip(near_a, near_ops)]
    far_o = [[dot_nn(a, ops[2]) for a, ops in zip(avs, lev)] for avs, lev in zip(far_a, far_ops)]
    inter_o = [dot_nt(qe, s.astype(BF16)) for (qe, _, _), s in zip(tails, states[:-1])]

    outs = []
    for no, fo, io in zip(near_o, far_o, inter_o):
        o = (no[0] if len(no) == 1 else jnp.concatenate(no, axis=0)) + io
        for j, x in enumerate(fo):
            m = 1 << (nsmall + j)
            o_ev, o_od = _halves(o, m)
            o = _interleave(o_ev, o_od + x, m)
        outs.append(o)
    return outs, states[-1]


def _hgrn_prompt_kernel(a_ref, wq_ref, wf_ref, wi_ref, wg_ref, p_ref, ng_ref, near_ref, far_ref,
                        o_ref, s_ref, wbf, st, *, chunk):
    i = pl.program_id(1)
    dk = wq_ref.shape[1]

    @pl.when(i == 0)
    def _():
        wbf[:, 0 * dk:1 * dk] = wq_ref[...].astype(BF16)
        wbf[:, 1 * dk:2 * dk] = wf_ref[...].astype(BF16)
        wbf[:, 2 * dk:3 * dk] = wi_ref[...].astype(BF16)
        wbf[:, 3 * dk:4 * dk] = wg_ref[...].astype(BF16)
        st[...] = jnp.zeros_like(st)

    tm = a_ref.shape[0]
    projs = [jnp.dot(a_ref[c0:c0 + chunk, :], wbf[...], preferred_element_type=F32)
             for c0 in range(0, tm, chunk)]
    lb = _lower_bound(p_ref)
    ng = ng_ref[...]
    chunks = []
    for proj in projs:
        q = _silu(proj[:, 0 * dk:1 * dk])
        logf, k = _hgrn_gates(proj[:, 1 * dk:2 * dk], lb)
        chunks.append((q, k, proj[:, 2 * dk:3 * dk], logf))
    outs, s_t = _hgrn_chunks(chunks, st[...], near_ref[...], far_ref[...])
    for j, (o, proj) in enumerate(zip(outs, projs)):
        o_ref[j * chunk:(j + 1) * chunk, :] = _group_norm_gate(o, proj[:, 3 * dk:4 * dk], ng).astype(o_ref.dtype)
    st[...] = s_t
    s_ref[0, 0] = s_t.T


def _hgrn_prompt(a, w_in, lower_bounds, norm_g, n_heads, dk):
    m, d = a.shape
    chunk = min(HG_CHUNK, m)
    tm = _pick(m, (HG_ROWS, 512, 256, 128, 64, 32, 16, 8))
    tm = max(tm, chunk)
    near, far = _level_tables(chunk)
    w_spec = lambda blk: pl.BlockSpec((d, dk), lambda h, i: (0, blk * n_heads + h))
    o, s = pl.pallas_call(
        functools.partial(_hgrn_prompt_kernel, chunk=chunk),
        out_shape=(jax.ShapeDtypeStruct((m, n_heads * dk), BF16),
                   jax.ShapeDtypeStruct((1, n_heads, dk, dk), F32)),
        grid=(n_heads, m // tm),
        in_specs=[pl.BlockSpec((tm, d), lambda h, i: (i, 0)),
                  w_spec(0), w_spec(1), w_spec(2), w_spec(3),
                  pl.BlockSpec((lower_bounds.shape[0], dk), lambda h, i: (0, h)),
                  pl.BlockSpec((1, dk), lambda h, i: (0, h)),
                  pl.BlockSpec(near.shape, lambda h, i: (0, 0)),
                  pl.BlockSpec(far.shape, lambda h, i: (0, 0))],
        out_specs=(pl.BlockSpec((tm, dk), lambda h, i: (i, h)),
                   pl.BlockSpec((1, 1, dk, dk), lambda h, i: (0, h, 0, 0))),
        scratch_shapes=[pltpu.VMEM((d, 4 * dk), BF16), pltpu.VMEM((dk, dk), F32)],
        compiler_params=_cparams(("arbitrary", "arbitrary"), V7X_VMEM_LIMIT),
        name="hgrn_prompt",
    )(a, w_in, w_in, w_in, w_in, lower_bounds, norm_g.reshape(1, -1), near, far)
    return o, s


def _hgrn_sample_kernel(qr_ref, fr_ref, ir_ref, gr_ref, p_ref, ng_ref, s0_ref, o_ref, s_ref, *, seq):
    bs = s0_ref.shape[0]
    rows = bs * seq
    lb = _lower_bound(p_ref)
    q = _silu(qr_ref[...])
    logf, k = _hgrn_gates(fr_ref[...], lb)
    v = ir_ref[...]
    row = lax.broadcasted_iota(jnp.int32, q.shape, 0)
    t = row % seq
    b = logf
    d = 1
    while d < seq:
        b = b + jnp.where(t >= d, pltpu.roll(b, d, 0), 0.0)
        d *= 2
    o = jnp.sum(q * k, axis=-1, keepdims=True) * v
    for d in range(1, seq):
        ok = t >= d
        arg = jnp.where(ok, b - pltpu.roll(b, d, 0), 0.0)
        w = jnp.sum(q * pltpu.roll(k, d, 0) * jnp.exp(arg), axis=-1, keepdims=True)
        o = o + jnp.where(ok, w, 0.0) * pltpu.roll(v, d, 0)
    qe = (q * jnp.exp(b)).astype(BF16)
    b_end = b
    d = 1
    while d < seq:
        b_end = jnp.where(t < seq - d, pltpu.roll(b_end, rows - d, 0), b_end)
        d *= 2
    k_end = k * jnp.exp(b_end - b)
    k_end_t = k_end.T
    dec_t = jnp.exp(b_end).T
    lane = lax.broadcasted_iota(jnp.int32, k_end_t.shape, 1)
    vb = v.astype(BF16)
    o_inter = []
    for s in range(bs):
        s0 = s0_ref[s, 0]
        in_seq = (lane >= s * seq) & (lane < (s + 1) * seq)
        upd = jnp.dot(jnp.where(in_seq, k_end_t, 0.0).astype(BF16), vb, preferred_element_type=F32)
        s_ref[s, 0] = s0 * dec_t[:, s * seq:s * seq + 1] + upd
        if s % (8 // seq) == 0:
            r0 = s * seq
            parts = []
        oi = jnp.dot(qe[r0:r0 + 8, :], s0.astype(BF16), preferred_element_type=F32)
        parts.append(oi)
        if s % (8 // seq) == (8 // seq) - 1:
            r8 = lax.broadcasted_iota(jnp.int32, oi.shape, 0) // seq
            acc = parts[0]
            for j in range(1, len(parts)):
                acc = jnp.where(r8 == j, parts[j], acc)
            o_inter.append(acc)
    o = o + jnp.concatenate(o_inter, axis=0)
    o_ref[...] = _group_norm_gate(o, gr_ref[...], ng_ref[...]).astype(o_ref.dtype)


def _hgrn_sample(proj, lower_bounds, norm_g, state, seq):
    b, n_heads, dk, dv = state.shape
    m = proj.shape[0]
    bs = _pick(b, (64, 32, 16, 8, 4, 2))
    rows = bs * seq
    col = lambda blk: pl.BlockSpec((rows, dk), lambda h, i: (i, blk * n_heads + h))
    o, s = pl.pallas_call(
        functools.partial(_hgrn_sample_kernel, seq=seq),
        out_shape=(jax.ShapeDtypeStruct((m, n_heads * dv), BF16),
                   jax.ShapeDtypeStruct(state.shape, F32)),
        grid=(n_heads, b // bs),
        in_specs=[col(0), col(1), col(2), col(3),
                  pl.BlockSpec((lower_bounds.shape[0], dk), lambda h, i: (0, h)),
                  pl.BlockSpec((1, dv), lambda h, i: (0, h)),
                  pl.BlockSpec((bs, 1, dk, dv), lambda h, i: (i, h, 0, 0))],
        out_specs=(pl.BlockSpec((rows, dv), lambda h, i: (i, h)),
                   pl.BlockSpec((bs, 1, dk, dv), lambda h, i: (i, h, 0, 0))),
        compiler_params=_cparams(("arbitrary", "arbitrary"), V7X_VMEM_LIMIT),
        name="hgrn_sample",
    )(proj, proj, proj, proj, lower_bounds, norm_g.reshape(1, -1), state)
    return o, s


def kernel(x_prompt, x_sample, cache_swa_k, cache_swa_v, state_hgrn, state_ffn_conv, c_prompt, c_sample,
           norm1_g, norm2_g, w_ada, b_ada, attn_w_qkv, attn_w_o, attn_sinks,
           hgrn_w_in, hgrn_lower_bounds, hgrn_norm_g, hgrn_w_o,
           ffn_w_in, ffn_conv_w, ffn_conv_b, ffn_w_out, final_norm_g):
    bp, lp, d = x_prompt.shape
    bsmp, ls, _ = x_sample.shape
    assert bp == 1, "the prompt kernels carry one sequence"
    depth = w_ada.shape[0]
    n_kv, hd = cache_swa_k.shape[2], cache_swa_k.shape[3]
    n_heads = attn_sinks.shape[0]
    dq = n_heads * hd
    dkv = n_kv * hd
    hg_heads, hg_dk = state_hgrn.shape[1], state_hgrn.shape[2]
    keep = min(WINDOW, lp)

    ns = bsmp * ls
    assert ns % 8 == 0
    assert bsmp % 8 == 0
    c_seq = jnp.concatenate([c_sample, c_prompt, jnp.zeros((7, d), F32)], axis=0)
    mod = _ada(c_seq, w_ada, b_ada, ls)
    pk = dict(sample=False, n_sample=ns)
    sk = dict(sample=True, n_sample=ns)

    xp = x_prompt.reshape(lp, d)
    xs = x_sample.reshape(ns, d)
    conv_p, conv_s = [], []
    outs = {}
    w_out_bf = ffn_w_out.astype(BF16)
    hp = None
    hs = _modulate(xs, norm1_g, mod, 0, 0, 1, True, ns)
    for l in range(depth):
        if l % 2 == 0:
            if hp is None:
                qkv_p = _matmul_mod(xp, norm1_g, mod, l, 0, 1, attn_w_qkv, ns, tm=1024, tn=1024,
                                    name="qkv_prompt")
            else:
                qkv_p = _matmul(hp, attn_w_qkv, tn=1024, name="qkv_prompt")
            qkv_s = _matmul(hs, attn_w_qkv, name="qkv_sample")
            op = _swa_prompt(qkv_p, attn_sinks, n_kv, hd)
            outs['swa_k_prompt'] = qkv_p[lp - keep:, dq:dq + dkv].reshape(bp, keep, n_kv, hd)
            outs['swa_v_prompt'] = qkv_p[lp - keep:, dq + dkv:].reshape(bp, keep, n_kv, hd)
            outs['swa_k_sample'] = qkv_s[:, dq:dq + dkv].reshape(bsmp, ls, n_kv, hd)
            outs['swa_v_sample'] = qkv_s[:, dq + dkv:].reshape(bsmp, ls, n_kv, hd)
            os_ = _swa_sample(qkv_s[:, :dq], qkv_s[:, dq:dq + dkv], qkv_s[:, dq + dkv:],
                              cache_swa_k, cache_swa_v, attn_sinks, n_kv, hd, ls)
            w_o = attn_w_o
        else:
            if hp is None:
                hp = _modulate(xp, norm1_g, mod, l, 0, 1, False, ns)
            op, sp = _hgrn_prompt(hp, hgrn_w_in, hgrn_lower_bounds, hgrn_norm_g, hg_heads, hg_dk)
            proj_s = _matmul(hs, hgrn_w_in, name="hgrn_in_sample")
            os_, ss = _hgrn_sample(proj_s, hgrn_lower_bounds, hgrn_norm_g, state_hgrn, ls)
            outs['hgrn_state_prompt'] = sp
            outs['hgrn_state_sample'] = ss
            w_o = hgrn_w_o
        xp, hp = _mm_rows(op, w_o, xp, mod, gate=(l, 2), norm_g=norm2_g, norm_layer=l, next_mod=(l, 3, 4),
                          tm=512, name="mix_out_prompt", **pk)
        xs, hs = _mm_rows(os_, w_o, xs, mod, gate=(l, 2), norm_g=norm2_g, norm_layer=l, next_mod=(l, 3, 4),
                          tm=512, name="mix_out_sample", **sk)
        actp, bufp = _ffn_in_prompt(hp, ffn_w_in, ffn_conv_w, ffn_conv_b, l)
        acts, bufs = _ffn_in_sample(hs, ffn_w_in, ffn_conv_w, ffn_conv_b, l, state_ffn_conv, ls)
        conv_p.append(bufp[None])
        conv_s.append(bufs)
        if l + 1 < depth:
            nxt = dict(norm_g=norm1_g, norm_layer=l + 1, next_mod=(l + 1, 0, 1))
        else:
            nxt = dict(norm_g=final_norm_g, norm_layer=0, next_mod=None, keep_x=False, out_dtype=F32)
        xp, hp = _mm_rows(actp, w_out_bf, xp, mod, gate=(l, 5), w_layer=l, tm=256, name="ffn_out_prompt",
                          **nxt, **pk)
        xs, hs = _mm_rows(acts, w_out_bf, xs, mod, gate=(l, 5), w_layer=l, tm=256, name="ffn_out_sample",
                          **nxt, **sk)
    y_prompt = hp.reshape(bp, lp, d)
    y_sample = hs.reshape(bsmp, ls, d)
    return (y_prompt, y_sample,
            outs['swa_k_prompt'], outs['swa_v_prompt'], outs['swa_k_sample'], outs['swa_v_sample'],
            outs['hgrn_state_prompt'], outs['hgrn_state_sample'],
            jnp.stack(conv_p), jnp.stack(conv_s))
```

```python
import functools
import math

import numpy as np
import jax
import jax.numpy as jnp
from jax import lax
from jax.experimental import pallas as pl
from jax.experimental.pallas import tpu as pltpu

F32 = jnp.float32
BF16 = jnp.bfloat16

EPS = 1e-6
LOG2E = math.log2(math.e)
WINDOW = 128
CONV_W = 3
LANES = 128
HG_CHUNK = 256
HG_ROWS = 2048
SWA_WAVE = 8
V7X_VMEM_LIMIT = 56 * 1024 * 1024


def _cparams(sem, vmem=None):
    return pltpu.CompilerParams(dimension_semantics=sem, vmem_limit_bytes=vmem)


def _pick(n, cands):
    for c in cands:
        if n % c == 0:
            return c
    return n


def _silu(x):
    return x * jax.nn.sigmoid(x)


def _logaddexp(a, b):
    return jnp.maximum(a, b) + jnp.log1p(jnp.exp(-jnp.abs(a - b)))


def _ada_kernel(c_ref, w_ref, b_ref, o_ref, act, slab, *, n_seq, seq):
    @pl.when((pl.program_id(0) == 0) & (pl.program_id(1) == 0))
    def _():
        act[...] = _silu(c_ref[...]).astype(BF16)

    w = w_ref[0].astype(BF16)
    res = jnp.dot(act[...], w, preferred_element_type=F32) + b_ref[0]
    for c in range(slab.shape[0]):
        cs = slice(c * LANES, (c + 1) * LANES)
        for t in range(seq):
            slab[c, pl.ds(t, n_seq, stride=seq), :] = res[:n_seq, cs]
        slab[c, n_seq * seq:, :] = res[n_seq:, cs]
        o_ref[0, :, cs] = slab[c]


def _ada(c_seq, w_ada, b_ada, seq):
    depth, d, n = w_ada.shape
    r = c_seq.shape[0]
    n_seq = r - 8
    rows = n_seq * seq + 8
    tn = _pick(n, (2048, 1024, 512, 256, 128))
    return pl.pallas_call(
        functools.partial(_ada_kernel, n_seq=n_seq, seq=seq),
        out_shape=jax.ShapeDtypeStruct((depth, rows, n), F32),
        grid=(depth, n // tn),
        in_specs=[pl.BlockSpec((r, d), lambda l, j: (0, 0)),
                  pl.BlockSpec((1, d, tn), lambda l, j: (l, 0, j)),
                  pl.BlockSpec((1, 1, tn), lambda l, j: (l, 0, j))],
        out_specs=pl.BlockSpec((1, rows, tn), lambda l, j: (l, 0, j)),
        scratch_shapes=[pltpu.VMEM((r, d), BF16), pltpu.VMEM((tn // LANES, rows, LANES), F32)],
        compiler_params=_cparams(("arbitrary", "arbitrary"), V7X_VMEM_LIMIT),
        name="ada",
    )(c_seq, w_ada, b_ada.reshape(depth, 1, n))


def _mod_rows(sample, tm, n_sample):
    if sample:
        return tm, (lambda i: i)
    return 8, (lambda i: n_sample // 8)


def _modulate_kernel(x_ref, g_ref, sc_ref, sh_ref, o_ref, *, sample):
    x = x_ref[...]
    ms = jnp.mean(x * x, axis=-1, keepdims=True)
    y = x * lax.rsqrt(ms + EPS) * g_ref[...]
    sc = sc_ref[...] if sample else sc_ref[0:1, :]
    sh = sh_ref[...] if sample else sh_ref[0:1, :]
    o_ref[...] = (y * (1.0 + sc) + sh).astype(o_ref.dtype)


def _modulate(x, g_all, mod, l, shift_col, scale_col, sample, n_sample):
    m, d = x.shape
    tm = _pick(m, (512, 256, 128, 64, 32, 16, 8))
    rows, rmap = _mod_rows(sample, tm, n_sample)
    mspec = lambda col: pl.BlockSpec((None, rows, d), lambda i: (l, rmap(i), col))
    return pl.pallas_call(
        functools.partial(_modulate_kernel, sample=sample),
        out_shape=jax.ShapeDtypeStruct((m, d), BF16),
        grid=(m // tm,),
        in_specs=[pl.BlockSpec((tm, d), lambda i: (i, 0)),
                  pl.BlockSpec((None, 1, d), lambda i: (l, 0, 0)),
                  mspec(scale_col), mspec(shift_col)],
        out_specs=pl.BlockSpec((tm, d), lambda i: (i, 0)),
        compiler_params=_cparams(("arbitrary",), V7X_VMEM_LIMIT),
        name="modulate",
    )(x, g_all.reshape(g_all.shape[0], 1, d), mod, mod)


def _mm_kernel(a_ref, w_ref, o_ref, wbf):
    @pl.when(pl.program_id(1) == 0)
    def _():
        wbf[...] = w_ref[...].astype(BF16)

    o_ref[...] = jnp.dot(a_ref[...], wbf[...], preferred_element_type=F32)


def _mm_mod_kernel(x_ref, g_ref, sc_ref, sh_ref, w_ref, o_ref, wbf, *, sample):
    @pl.when(pl.program_id(1) == 0)
    def _():
        wbf[...] = w_ref[...].astype(BF16)

    rows = (lambda r: r[...]) if sample else (lambda r: r[0:1, :])
    x = x_ref[...]
    y = x * lax.rsqrt(jnp.mean(x * x, axis=-1, keepdims=True) + EPS) * g_ref[...]
    h = (y * (1.0 + rows(sc_ref)) + rows(sh_ref)).astype(BF16)
    o_ref[...] = jnp.dot(h, wbf[...], preferred_element_type=F32)


def _matmul_mod(x, g_all, mod, l, shift_col, scale_col, w, sample, n_sample, *, tm, tn, name):
    m, d = x.shape
    n = w.shape[-1]
    tm = min(tm, m)
    tn = tn if n % tn == 0 else _pick(n, (512, 256, 128))
    assert m % tm == 0 and n % tn == 0
    rows, rmap = _mod_rows(sample, tm, n_sample)
    mspec = lambda col: pl.BlockSpec((None, rows, d), lambda j, i: (l, rmap(i), col))
    return pl.pallas_call(
        functools.partial(_mm_mod_kernel, sample=sample),
        out_shape=jax.ShapeDtypeStruct((m, n), F32),
        grid=(n // tn, m // tm),
        in_specs=[pl.BlockSpec((tm, d), lambda j, i: (i, 0)),
                  pl.BlockSpec((None, 1, d), lambda j, i: (l, 0, 0)),
                  mspec(scale_col), mspec(shift_col),
                  pl.BlockSpec((d, tn), lambda j, i: (0, j))],
        out_specs=pl.BlockSpec((tm, tn), lambda j, i: (i, j)),
        scratch_shapes=[pltpu.VMEM((d, tn), BF16)],
        compiler_params=_cparams(("arbitrary", "arbitrary"), V7X_VMEM_LIMIT),
        name=name,
    )(x, g_all.reshape(g_all.shape[0], 1, d), mod, mod, w)


def _matmul(a, w, *, tm=None, tn=None, name="matmul"):
    m, k = a.shape
    n = w.shape[-1]
    tm = min(tm, m) if tm else _pick(m, (1024, 512, 256, 128, 64, 32, 16, 8))
    tn = min(tn, n) if tn else _pick(n, (512, 256, 128))
    assert m % tm == 0 and n % tn == 0
    return pl.pallas_call(
        _mm_kernel,
        out_shape=jax.ShapeDtypeStruct((m, n), F32),
        grid=(n // tn, m // tm),
        in_specs=[pl.BlockSpec((tm, k), lambda j, i: (i, 0)),
                  pl.BlockSpec((k, tn), lambda j, i: (0, j))],
        out_specs=pl.BlockSpec((tm, tn), lambda j, i: (i, j)),
        scratch_shapes=[pltpu.VMEM((k, tn), BF16)],
        compiler_params=_cparams(("arbitrary", "arbitrary"), V7X_VMEM_LIMIT),
        name=name,
    )(a, w)


def _mm_rows_kernel(*refs, cast_w, sample, modulate, keep_x):
    it = iter(refs)
    a_ref, w_ref, x_ref, gate_ref, g_ref = (next(it) for _ in range(5))
    sc_ref, sh_ref = (next(it), next(it)) if modulate else (None, None)
    xo_ref = next(it) if keep_x else None
    h_ref = next(it)
    if cast_w:
        wbf = next(it)

        @pl.when(pl.program_id(0) == 0)
        def _():
            wbf[...] = w_ref[...].astype(BF16)

        w = wbf[...]
    else:
        w = w_ref[...]
    rows = (lambda r: r[...]) if sample else (lambda r: r[0:1, :])
    x = x_ref[...] + rows(gate_ref) * jnp.dot(a_ref[...], w, preferred_element_type=F32)
    if keep_x:
        xo_ref[...] = x
    y = x * lax.rsqrt(jnp.mean(x * x, axis=-1, keepdims=True) + EPS) * g_ref[...]
    if modulate:
        y = y * (1.0 + rows(sc_ref)) + rows(sh_ref)
    h_ref[...] = y.astype(h_ref.dtype)


def _mm_rows(a, w, x, mod, *, gate, norm_g, norm_layer, next_mod=None, w_layer=None, sample, n_sample,
             tm, keep_x=True, out_dtype=None, name):
    m, k = a.shape
    n = w.shape[-1]
    tm = min(tm, m)
    assert m % tm == 0
    cast_w = w.dtype != BF16
    rows, rmap = _mod_rows(sample, tm, n_sample)
    once = pl.Buffered(1)
    if w_layer is None:
        w_spec = pl.BlockSpec((k, n), lambda i: (0, 0), pipeline_mode=once)
    else:
        w_spec = pl.BlockSpec((None, k, n), lambda i: (w_layer, 0, 0), pipeline_mode=once)
    mspec = lambda layer, col: pl.BlockSpec((None, rows, n), lambda i: (layer, rmap(i), col))
    in_specs = [pl.BlockSpec((tm, k), lambda i: (i, 0)), w_spec,
                pl.BlockSpec((tm, n), lambda i: (i, 0)),
                mspec(*gate),
                pl.BlockSpec((None, 1, n), lambda i: (norm_layer, 0, 0))]
    args = [a, w, x, mod, norm_g.reshape(-1, 1, n)]
    if next_mod is not None:
        in_specs += [mspec(next_mod[0], next_mod[2]), mspec(next_mod[0], next_mod[1])]
        args += [mod, mod]
    out_shape, out_specs = [], []
    if keep_x:
        out_shape.append(jax.ShapeDtypeStruct((m, n), F32))
        out_specs.append(pl.BlockSpec((tm, n), lambda i: (i, 0)))
    out_shape.append(jax.ShapeDtypeStruct((m, n), out_dtype or BF16))
    out_specs.append(pl.BlockSpec((tm, n), lambda i: (i, 0)))
    res = pl.pallas_call(
        functools.partial(_mm_rows_kernel, cast_w=cast_w, sample=sample, modulate=next_mod is not None,
                          keep_x=keep_x),
        out_shape=tuple(out_shape),
        grid=(m // tm,),
        in_specs=in_specs,
        out_specs=tuple(out_specs),
        scratch_shapes=[pltpu.VMEM((k, n), BF16)] if cast_w else [],
        compiler_params=_cparams(("arbitrary",), V7X_VMEM_LIMIT),
        name=name,
    )(*args)
    return res if keep_x else (None, res[0])


def _gelu_exact(a):
    return 0.5 * a * (1.0 + lax.erf(a * (1.0 / math.sqrt(2.0))))


def _ffn_in_prompt_kernel(a_ref, wu_ref, wg_ref, cw_ref, cb_ref, act_ref, tail_ref, wbf, carry):
    i = pl.program_id(1)
    tf = wu_ref.shape[1]

    @pl.when(i == 0)
    def _():
        wbf[:, :tf] = wu_ref[...].astype(BF16)
        wbf[:, tf:] = wg_ref[...].astype(BF16)
        carry[...] = jnp.zeros_like(carry)

    ug = jnp.dot(a_ref[...], wbf[...], preferred_element_type=F32)
    u = ug[:, :tf]
    g = ug[:, tf:]
    tm = u.shape[0]
    row = lax.broadcasted_iota(jnp.int32, u.shape, 0)
    c = carry[...]
    u1 = jnp.where(row == 0, c[7:8, :], pltpu.roll(u, 1, 0))
    u2 = jnp.where(row == 0, c[6:7, :], jnp.where(row == 1, c[7:8, :], pltpu.roll(u, 2, 0)))
    cw = cw_ref[...]
    conv = cw[0:1, :] * u2 + cw[1:2, :] * u1 + cw[2:3, :] * u + cb_ref[...]
    act_ref[...] = (_gelu_exact(conv) * g).astype(act_ref.dtype)
    carry[...] = u[tm - 8:, :]
    tail_ref[...] = u[tm - 8:, :]


def _ffn_in_sample_kernel(a_ref, wu_ref, wg_ref, cw_ref, cb_ref, buf_ref, act_ref, nbuf_ref,
                          us, gs, acts, *, seq):
    a = a_ref[...]
    b = buf_ref.shape[0]
    u_all = jnp.dot(a, wu_ref[...].astype(BF16), preferred_element_type=F32)
    g_all = jnp.dot(a, wg_ref[...].astype(BF16), preferred_element_type=F32)
    cw = cw_ref[...]
    cb = cb_ref[...]
    for c in range(us.shape[0]):
        cs = slice(c * LANES, (c + 1) * LANES)
        us[c] = u_all[:, cs]
        gs[c] = g_all[:, cs]
        prev2, prev1 = buf_ref[:, 0, cs], buf_ref[:, 1, cs]
        for t in range(seq):
            u = us[c, pl.ds(t, b, stride=seq), :]
            conv = cw[0:1, cs] * prev2 + cw[1:2, cs] * prev1 + cw[2:3, cs] * u + cb[:, cs]
            acts[c, pl.ds(t, b, stride=seq), :] = _gelu_exact(conv) * gs[c, pl.ds(t, b, stride=seq), :]
            prev2, prev1 = prev1, u
        act_ref[:, cs] = acts[c].astype(act_ref.dtype)
        nbuf_ref[:, 0, cs] = prev2
        nbuf_ref[:, 1, cs] = prev1


def _ffn_in_prompt(a, w_in, conv_w, conv_b, l):
    m, d = a.shape
    f = w_in.shape[-1] // 2
    tf = _pick(f, (512, 256, 128))
    tm = _pick(m, (1024, 512, 256, 128, 64, 32, 16, 8))
    nf = f // tf
    act, tail = pl.pallas_call(
        _ffn_in_prompt_kernel,
        out_shape=(jax.ShapeDtypeStruct((m, f), BF16), jax.ShapeDtypeStruct((8, f), F32)),
        grid=(nf, m // tm),
        in_specs=[pl.BlockSpec((tm, d), lambda j, i: (i, 0)),
                  pl.BlockSpec((None, d, tf), lambda j, i: (l, 0, j)),
                  pl.BlockSpec((None, d, tf), lambda j, i: (l, 0, nf + j)),
                  pl.BlockSpec((None, CONV_W, tf), lambda j, i: (l, 0, j)),
                  pl.BlockSpec((None, 1, tf), lambda j, i: (l, 0, j))],
        out_specs=(pl.BlockSpec((tm, tf), lambda j, i: (i, j)),
                   pl.BlockSpec((8, tf), lambda j, i: (0, j))),
        scratch_shapes=[pltpu.VMEM((d, 2 * tf), BF16), pltpu.VMEM((8, tf), F32)],
        compiler_params=_cparams(("arbitrary", "arbitrary"), V7X_VMEM_LIMIT),
        name="ffn_in_prompt",
    )(a, w_in, w_in, conv_w, conv_b.reshape(conv_b.shape[0], 1, f))
    return act, tail[6:8]


def _ffn_in_sample(a, w_in, conv_w, conv_b, l, state, seq):
    m, d = a.shape
    f = w_in.shape[-1] // 2
    b = state.shape[1]
    tf = _pick(f, (512, 256, 128))
    nf = f // tf
    return pl.pallas_call(
        functools.partial(_ffn_in_sample_kernel, seq=seq),
        out_shape=(jax.ShapeDtypeStruct((m, f), BF16), jax.ShapeDtypeStruct((b, CONV_W - 1, f), F32)),
        grid=(nf,),
        in_specs=[pl.BlockSpec((m, d), lambda j: (0, 0)),
                  pl.BlockSpec((None, d, tf), lambda j: (l, 0, j)),
                  pl.BlockSpec((None, d, tf), lambda j: (l, 0, nf + j)),
                  pl.BlockSpec((None, CONV_W, tf), lambda j: (l, 0, j)),
                  pl.BlockSpec((None, 1, tf), lambda j: (l, 0, j)),
                  pl.BlockSpec((None, b, CONV_W - 1, tf), lambda j: (l, 0, 0, j))],
        out_specs=(pl.BlockSpec((m, tf), lambda j: (0, j)),
                   pl.BlockSpec((b, CONV_W - 1, tf), lambda j: (0, 0, j))),
        scratch_shapes=[pltpu.VMEM((tf // LANES, m, LANES), F32)] * 3,
        compiler_params=_cparams(("arbitrary",), V7X_VMEM_LIMIT),
        name="ffn_in_sample",
    )(a, w_in, w_in, conv_w, conv_b.reshape(conv_b.shape[0], 1, f), state)


def _alibi_slope(h, n_heads):
    return 2.0 ** (-8.0 * (h + 1) / n_heads)


def _swa_prompt_bias(n_heads, tq):
    t = np.arange(tq)[:, None]
    s = np.arange(2 * tq)[None, :]
    dist = t + tq - s
    ok = (dist >= 0) & (dist < WINDOW)
    slope = (2.0 ** (-8.0 * (np.arange(n_heads) + 1) / n_heads) * LOG2E)[:, None, None]
    table = np.stack([np.where(ok & (s >= tq), slope * s, -np.inf), np.where(ok, slope * s, -np.inf)])
    return jnp.asarray(table, F32)


def _swa_prompt_kernel(sink_ref, q_ref, kp_ref, kc_ref, vp_ref, vc_ref, bias_ref, o_ref, *, n_kv, group, hd):
    n_heads = n_kv * group
    tq = q_ref.shape[0]
    trow = (lax.broadcasted_iota(jnp.int32, (tq, 1), 0) + tq).astype(F32)
    qscale = hd ** -0.5 * LOG2E
    nt = (((1,), (1,)), ((), ()))
    def scores(kv):
        ks = slice(kv * hd, (kv + 1) * hd)
        k2 = jnp.concatenate([kp_ref[:, ks], kc_ref[:, ks]], axis=0).astype(BF16)
        q4 = jnp.concatenate([(q_ref[:, h * hd:(h + 1) * hd] * qscale).astype(BF16)
                              for h in range(kv * group, (kv + 1) * group)], axis=0)
        return lax.dot_general(q4, k2, nt, preferred_element_type=F32)

    for kv0 in range(0, n_kv, SWA_WAVE):
        kvs = list(range(kv0, min(kv0 + SWA_WAVE, n_kv)))
        sc4s = [scores(kv) for kv in kvs]
        heads = [kv * group + g for kv in kvs for g in range(group)]
        slopes = [_alibi_slope(h, n_heads) * LOG2E for h in heads]
        scs = [sc4s[j // group][(j % group) * tq:(j % group + 1) * tq] + bias_ref[h]
               for j, h in enumerate(heads)]
        sinks = [sink_ref[h] * LOG2E + slopes[j] * trow for j, h in enumerate(heads)]
        mxs = [jnp.maximum(jnp.max(scs[j], axis=-1, keepdims=True), sinks[j]) for j in range(len(heads))]
        es = [jnp.exp2(scs[j] - mxs[j]) for j in range(len(heads))]
        dens = [jnp.sum(es[j], axis=-1, keepdims=True) + jnp.exp2(sinks[j] - mxs[j])
                for j in range(len(heads))]
        es = [e.astype(BF16) for e in es]
        o4s = []
        for i, kv in enumerate(kvs):
            ks = slice(kv * hd, (kv + 1) * hd)
            v2 = jnp.concatenate([vp_ref[:, ks], vc_ref[:, ks]], axis=0).astype(BF16)
            o4s.append(jnp.dot(jnp.concatenate(es[i * group:(i + 1) * group], axis=0), v2,
                               preferred_element_type=F32))
        for j, h in enumerate(heads):
            o = o4s[j // group][(j % group) * tq:(j % group + 1) * tq]
            o_ref[:, h * hd:(h + 1) * hd] = (o / dens[j]).astype(o_ref.dtype)


def _swa_prompt(qkv, sinks, n_kv, hd):
    l = qkv.shape[0]
    n_heads = sinks.shape[0]
    group = n_heads // n_kv
    dq = n_heads * hd
    dk = n_kv * hd
    nb = l // WINDOW
    kblk = dq // dk
    bias = _swa_prompt_bias(n_heads, WINDOW)
    return pl.pallas_call(
        functools.partial(_swa_prompt_kernel, n_kv=n_kv, group=group, hd=hd),
        out_shape=jax.ShapeDtypeStruct((l, dq), BF16),
        grid_spec=pltpu.PrefetchScalarGridSpec(
            num_scalar_prefetch=1,
            grid=(nb,),
            in_specs=[pl.BlockSpec((WINDOW, dq), lambda i, s: (i, 0)),
                      pl.BlockSpec((WINDOW, dk), lambda i, s: (jnp.maximum(i - 1, 0), kblk)),
                      pl.BlockSpec((WINDOW, dk), lambda i, s: (i, kblk)),
                      pl.BlockSpec((WINDOW, dk), lambda i, s: (jnp.maximum(i - 1, 0), kblk + 1)),
                      pl.BlockSpec((WINDOW, dk), lambda i, s: (i, kblk + 1)),
                      pl.BlockSpec((None,) + bias.shape[1:], lambda i, s: (jnp.minimum(i, 1), 0, 0, 0))],
            out_specs=pl.BlockSpec((WINDOW, dq), lambda i, s: (i, 0)),
        ),
        compiler_params=_cparams(("arbitrary",), V7X_VMEM_LIMIT),
        name="swa_prompt",
    )(sinks, qkv, qkv, qkv, qkv, qkv, bias)


def _swa_sample_tables(n_heads, n_kv, hd, seq, cr):
    group = n_heads // n_kv
    r = np.arange(seq * n_heads)
    t, h = r // n_heads, r % n_heads
    slope = 2.0 ** (-8.0 * (h + 1) / n_heads)
    slot = (np.arange(n_kv * hd)[None, :] // hd == (h // group)[:, None]).astype(np.float32)
    dist = cr + t[:, None] - np.arange(cr)[None, :]
    bias_c = np.where((dist >= 0) & (dist < WINDOW), -slope[:, None] * dist, -np.inf) * LOG2E
    j = np.arange(LANES)
    distn = t[:, None] - j[None, :]
    bias_n = np.where((distn >= 0) & (j[None, :] < seq), -slope[:, None] * distn, -np.inf) * LOG2E
    return jnp.asarray(slot, F32), jnp.asarray(np.concatenate([bias_c, bias_n], axis=1), F32)


def _swa_sample_kernel(q_ref, kt_ref, vt_ref, kn_ref, vn_ref, slot_ref, bias_ref, sink_ref, o_ref, *, hd):
    bs = q_ref.shape[0]
    cr = kt_ref.shape[2]
    width = slot_ref.shape[1]
    qscale = hd ** -0.5 * LOG2E
    slot = slot_ref[...]
    bias = bias_ref[...]
    sink2 = sink_ref[...] * LOG2E
    nt = (((1,), (1,)), ((), ()))

    def slotted(q):
        return jnp.concatenate([q] * (width // hd), axis=1) * slot

    qs = [slotted(q_ref[b] * qscale) for b in range(bs)]
    qb = [q.astype(BF16) for q in qs]
    zrows = jnp.zeros((LANES - kn_ref.shape[1], width), F32)
    kn = [jnp.concatenate([kn_ref[b], zrows], axis=0).astype(BF16) for b in range(bs)]
    vn = [jnp.concatenate([vn_ref[b], zrows], axis=0).astype(BF16) for b in range(bs)]
    sc = [jnp.concatenate([jnp.dot(qb[b], kt_ref[b].astype(BF16), preferred_element_type=F32),
                           lax.dot_general(qb[b], kn[b], nt, preferred_element_type=F32)], axis=1) + bias
          for b in range(bs)]
    mx = [jnp.maximum(jnp.max(s, axis=-1, keepdims=True), sink2) for s in sc]
    es = [jnp.exp2(s - m) for s, m in zip(sc, mx)]
    den = [jnp.sum(e, axis=-1, keepdims=True) + jnp.exp2(sink2 - m) for e, m in zip(es, mx)]
    es = [e.astype(BF16) for e in es]
    o_w = [lax.dot_general(es[b][:, :cr], vt_ref[b].astype(BF16), nt, preferred_element_type=F32)
           + jnp.dot(es[b][:, cr:], vn[b], preferred_element_type=F32) for b in range(bs)]
    for b in range(bs):
        o = o_w[b] * slot
        o = functools.reduce(jnp.add, [o[:, c * hd:(c + 1) * hd] for c in range(width // hd)])
        o_ref[b] = (o / den[b]).astype(o_ref.dtype)


def _swa_sample(q, k_new, v_new, cache_k, cache_v, sinks, n_kv, hd, seq):
    b, cr = cache_k.shape[0], cache_k.shape[1]
    n_heads = sinks.shape[0]
    rows = seq * n_heads
    width = n_kv * hd
    slot, bias = _swa_sample_tables(n_heads, n_kv, hd, seq, cr)
    kt = cache_k.transpose(0, 2, 3, 1).reshape(b, width, cr)
    vt = cache_v.transpose(0, 2, 3, 1).reshape(b, width, cr)
    new_rows = -(-seq // 8) * 8
    pad_new = lambda x: jnp.pad(x.reshape(b, seq, width), ((0, 0), (0, new_rows - seq), (0, 0)))
    bs = _pick(b, (8, 4, 2, 1))
    const = lambda shape: pl.BlockSpec(shape, lambda i: (0, 0))
    o = pl.pallas_call(
        functools.partial(_swa_sample_kernel, hd=hd),
        out_shape=jax.ShapeDtypeStruct((b, rows, hd), BF16),
        grid=(b // bs,),
        in_specs=[pl.BlockSpec((bs, rows, hd), lambda i: (i, 0, 0)),
                  pl.BlockSpec((bs, width, cr), lambda i: (i, 0, 0)),
                  pl.BlockSpec((bs, width, cr), lambda i: (i, 0, 0)),
                  pl.BlockSpec((bs, new_rows, width), lambda i: (i, 0, 0)),
                  pl.BlockSpec((bs, new_rows, width), lambda i: (i, 0, 0)),
                  const(slot.shape), const(bias.shape), const((rows, 1))],
        out_specs=pl.BlockSpec((bs, rows, hd), lambda i: (i, 0, 0)),
        compiler_params=_cparams(("arbitrary",), V7X_VMEM_LIMIT),
        name="swa_sample",
    )(q.reshape(b, rows, hd), kt, vt, pad_new(k_new), pad_new(v_new),
      slot, bias, jnp.tile(sinks, seq).reshape(rows, 1))
    return o.reshape(b * seq, n_heads * hd)


def _lower_bound(p_ref):
    p0 = p_ref[0:1, :]
    p1 = p_ref[1:2, :]
    mx = jnp.maximum(p0, p1)
    e0 = jnp.exp(p0 - mx)
    e1 = jnp.exp(p1 - mx)
    return e1 / (e0 + e1)


def _hgrn_gates(fr, lb):
    e = jnp.exp(-jnp.abs(fr))
    log_sig = jnp.minimum(fr, 0.0) - jnp.log1p(e)
    logf = _logaddexp(jnp.log(lb), jnp.log1p(-lb) + log_sig)
    r = 1.0 / (1.0 + e)
    k = (1.0 - lb) * jnp.where(fr >= 0.0, e * r, r)
    return logf, k


def _group_norm_gate(o, gr, ng):
    o = o * lax.rsqrt(jnp.mean(o * o, axis=-1, keepdims=True) + EPS) * ng
    return o * _silu(gr)


SMALL = 8

def _level_tables(c):
    g = min(c, 128)
    t = np.arange(g)[:, None]
    s = np.arange(g)[None, :]
    hb = np.floor(np.log2(np.maximum(t ^ s, 1))).astype(np.int32)
    same = (t // SMALL) == (s // SMALL)
    near = np.where(same & (s < t), hb, np.where(s == t, int(math.log2(SMALL)), -1))
    i = np.arange(c // 2)[:, None]
    j = np.arange(c // 2)[None, :]
    far = np.where(i == j, -1, np.floor(np.log2(np.maximum(i ^ j, 1)))).astype(np.int32)
    return jnp.asarray(near, dtype=jnp.int32), jnp.asarray(far, dtype=jnp.int32)


def _group_ref_rows(p, m, row):
    c, w = p.shape
    if m == 1:
        return jnp.where((row & 1) == 1, pltpu.roll(p, 1, 0), p)
    if m == 2:
        r4 = row & 3
        return jnp.where(r4 == 0, pltpu.roll(p, c - 1, 0),
                         jnp.where(r4 == 1, p,
                                   jnp.where(r4 == 2, pltpu.roll(p, 1, 0), pltpu.roll(p, 2, 0))))
    p3 = p.reshape(c // 8, 8, w)
    return jnp.broadcast_to(p3[:, 3:4, :], (c // 8, 8, w)).reshape(c, w)


def _halves(x, m):
    c = x.shape[0]
    ev = [x[r:r + m] for r in range(0, c, 2 * m)]
    od = [x[r + m:r + 2 * m] for r in range(0, c, 2 * m)]
    cat = lambda xs: xs[0] if len(xs) == 1 else jnp.concatenate(xs, axis=0)
    return cat(ev), cat(od)


def _interleave(ev, od, m):
    parts = []
    for r in range(0, ev.shape[0], m):
        parts += [ev[r:r + m], od[r:r + m]]
    return jnp.concatenate(parts, axis=0)


def _hgrn_chunks(chunks, st, near, far):
    c, w = chunks[0][0].shape
    nlev = int(math.log2(c))
    nsmall = int(math.log2(SMALL))
    g = near.shape[0]
    row = lax.broadcasted_iota(jnp.int32, (c, w), 0)
    nt = (((1,), (1,)), ((), ()))
    dot_nt = lambda x, y: lax.dot_general(x, y, nt, preferred_element_type=F32)
    dot_nn = lambda x, y: jnp.dot(x, y, preferred_element_type=F32)

    near_ops, far_ops, tails = [], [], []
    for q, k, v, logf in chunks:
        p = logf
        qs, ks = [q.astype(BF16)], [k.astype(BF16)]
        for l in range(nsmall):
            gref = _group_ref_rows(p, 1 << l, row)
            odd = ((row >> l) & 1) == 1
            e = jnp.exp(jnp.where(odd, p, gref - p))
            qs.append((q * e).astype(BF16))
            ks.append((k * e).astype(BF16))
            p = p + jnp.where(odd, gref, 0.0)
        near_ops.append((qs, ks, v.astype(BF16)))
        lev = []
        for l in range(nsmall, nlev):
            m = 1 << l
            pe, po = _halves(p, m)
            tot = jnp.concatenate([jnp.broadcast_to(pe[r + m - 1:r + m], (m, w))
                                   for r in range(0, c // 2, m)], axis=0)
            q_od = _halves(q, m)[1]
            k_ev = _halves(k, m)[0]
            v_ev = _halves(v, m)[0].astype(BF16)
            lev.append(((q_od * jnp.exp(po)).astype(BF16), (k_ev * jnp.exp(tot - pe)).astype(BF16), v_ev))
            p = _interleave(pe, po + tot, m)
        far_ops.append(lev)
        b = p
        b_end = b[c - 1:c, :]
        tails.append(((q * jnp.exp(b)).astype(BF16), (k * jnp.exp(b_end - b)).astype(BF16),
                      jnp.exp(b_end)))

    near_sc = [[[dot_nt(qs[j][r:r + g], ks[j][r:r + g]) for j in range(nsmall + 1)]
                for r in range(0, c, g)] for qs, ks, _ in near_ops]
    far_sc = [[dot_nt(qt, kt) for qt, kt, _ in lev] for lev in far_ops]
    upds = [lax.dot_general(vb, k_end, (((0,), (0,)), ((), ())), preferred_element_type=F32)
            for (_, _, vb), (_, k_end, _) in zip(near_ops, tails)]

    near_a = []
    for sc_chunk in near_sc:
        tiles = []
        for sc in sc_chunk:
            a = jnp.where(near == nsmall, sc[0], 0.0)
            for l in range(nsmall):
                a = jnp.where(near == l, sc[l + 1], a)
            tiles.append(a.astype(BF16))
        near_a.append(tiles)
    far_a = [[(jnp.where(far < nsmall + j, a, 0.0) if (1 << (nsmall + j)) < c // 2 else a).astype(BF16)
              for j, a in enumerate(lev)] for lev in far_sc]
    states = [st]
    for (_, _, dec), upd in zip(tails, upds):
        states.append(states[-1] * dec + upd)

    near_o = [[dot_nn(a, vb[i * g:(i + 1) * g]) for i, a in enumerate(tiles)]
              for tiles, (_, _, vb) in zip(near_a, near_ops)]
    far_o = [[dot_nn(a, ops[2]) for a, ops in zip(avs, lev)] for avs, lev in zip(far_a, far_ops)]
    inter_o = [dot_nt(qe, s.astype(BF16)) for (qe, _, _), s in zip(tails, states[:-1])]

    outs = []
    for no, fo, io in zip(near_o, far_o, inter_o):
        o = (no[0] if len(no) == 1 else jnp.concatenate(no, axis=0)) + io
        for j, x in enumerate(fo):
            m = 1 << (nsmall + j)
            o_ev, o_od = _halves(o, m)
            o = _interleave(o_ev, o_od + x, m)
        outs.append(o)
    return outs, states[-1]


def _hgrn_prompt_kernel(a_ref, wq_ref, wf_ref, wi_ref, wg_ref, p_ref, ng_ref, near_ref, far_ref,
                        o_ref, s_ref, wbf, st, *, chunk):
    i = pl.program_id(1)
    dk = wq_ref.shape[1]

    @pl.when(i == 0)
    def _():
        wbf[:, 0 * dk:1 * dk] = wq_ref[...].astype(BF16)
        wbf[:, 1 * dk:2 * dk] = wf_ref[...].astype(BF16)
        wbf[:, 2 * dk:3 * dk] = wi_ref[...].astype(BF16)
        wbf[:, 3 * dk:4 * dk] = wg_ref[...].astype(BF16)
        st[...] = jnp.zeros_like(st)

    tm = a_ref.shape[0]
    projs = [jnp.dot(a_ref[c0:c0 + chunk, :], wbf[...], preferred_element_type=F32)
             for c0 in range(0, tm, chunk)]
    lb = _lower_bound(p_ref)
    ng = ng_ref[...]
    chunks = []
    for proj in projs:
        q = _silu(proj[:, 0 * dk:1 * dk])
        logf, k = _hgrn_gates(proj[:, 1 * dk:2 * dk], lb)
        chunks.append((q, k, proj[:, 2 * dk:3 * dk], logf))
    outs, s_t = _hgrn_chunks(chunks, st[...], near_ref[...], far_ref[...])
    for j, (o, proj) in enumerate(zip(outs, projs)):
        o_ref[j * chunk:(j + 1) * chunk, :] = _group_norm_gate(o, proj[:, 3 * dk:4 * dk], ng).astype(o_ref.dtype)
    st[...] = s_t
    s_ref[0, 0] = s_t.T


def _hgrn_prompt(a, w_in, lower_bounds, norm_g, n_heads, dk):
    m, d = a.shape
    chunk = min(HG_CHUNK, m)
    tm = _pick(m, (HG_ROWS, 512, 256, 128, 64, 32, 16, 8))
    tm = max(tm, chunk)
    near, far = _level_tables(chunk)
    w_spec = lambda blk: pl.BlockSpec((d, dk), lambda h, i: (0, blk * n_heads + h))
    o, s = pl.pallas_call(
        functools.partial(_hgrn_prompt_kernel, chunk=chunk),
        out_shape=(jax.ShapeDtypeStruct((m, n_heads * dk), BF16),
                   jax.ShapeDtypeStruct((1, n_heads, dk, dk), F32)),
        grid=(n_heads, m // tm),
        in_specs=[pl.BlockSpec((tm, d), lambda h, i: (i, 0)),
                  w_spec(0), w_spec(1), w_spec(2), w_spec(3),
                  pl.BlockSpec((lower_bounds.shape[0], dk), lambda h, i: (0, h)),
                  pl.BlockSpec((1, dk), lambda h, i: (0, h)),
                  pl.BlockSpec(near.shape, lambda h, i: (0, 0)),
                  pl.BlockSpec(far.shape, lambda h, i: (0, 0))],
        out_specs=(pl.BlockSpec((tm, dk), lambda h, i: (i, h)),
                   pl.BlockSpec((1, 1, dk, dk), lambda h, i: (0, h, 0, 0))),
        scratch_shapes=[pltpu.VMEM((d, 4 * dk), BF16), pltpu.VMEM((dk, dk), F32)],
        compiler_params=_cparams(("arbitrary", "arbitrary"), V7X_VMEM_LIMIT),
        name="hgrn_prompt",
    )(a, w_in, w_in, w_in, w_in, lower_bounds, norm_g.reshape(1, -1), near, far)
    return o, s


def _hgrn_sample_kernel(qr_ref, fr_ref, ir_ref, gr_ref, p_ref, ng_ref, s0_ref, o_ref, s_ref, *, seq):
    bs = s0_ref.shape[0]
    rows = bs * seq
    lb = _lower_bound(p_ref)
    q = _silu(qr_ref[...])
    logf, k = _hgrn_gates(fr_ref[...], lb)
    v = ir_ref[...]
    row = lax.broadcasted_iota(jnp.int32, q.shape, 0)
    t = row % seq
    b = logf
    d = 1
    while d < seq:
        b = b + jnp.where(t >= d, pltpu.roll(b, d, 0), 0.0)
        d *= 2
    o = jnp.sum(q * k, axis=-1, keepdims=True) * v
    for d in range(1, seq):
        ok = t >= d
        arg = jnp.where(ok, b - pltpu.roll(b, d, 0), 0.0)
        w = jnp.sum(q * pltpu.roll(k, d, 0) * jnp.exp(arg), axis=-1, keepdims=True)
        o = o + jnp.where(ok, w, 0.0) * pltpu.roll(v, d, 0)
    qe = (q * jnp.exp(b)).astype(BF16)
    b_end = b
    d = 1
    while d < seq:
        b_end = jnp.where(t < seq - d, pltpu.roll(b_end, rows - d, 0), b_end)
        d *= 2
    k_end = k * jnp.exp(b_end - b)
    k_end_t = k_end.T
    dec_t = jnp.exp(b_end).T
    lane = lax.broadcasted_iota(jnp.int32, k_end_t.shape, 1)
    vb = v.astype(BF16)
    o_inter = []
    for s in range(bs):
        s0 = s0_ref[s, 0]
        in_seq = (lane >= s * seq) & (lane < (s + 1) * seq)
        upd = jnp.dot(jnp.where(in_seq, k_end_t, 0.0).astype(BF16), vb, preferred_element_type=F32)
        s_ref[s, 0] = s0 * dec_t[:, s * seq:s * seq + 1] + upd
        if s % (8 // seq) == 0:
            r0 = s * seq
            parts = []
        oi = jnp.dot(qe[r0:r0 + 8, :], s0.astype(BF16), preferred_element_type=F32)
        parts.append(oi)
        if s % (8 // seq) == (8 // seq) - 1:
            r8 = lax.broadcasted_iota(jnp.int32, oi.shape, 0) // seq
            acc = parts[0]
            for j in range(1, len(parts)):
                acc = jnp.where(r8 == j, parts[j], acc)
            o_inter.append(acc)
    o = o + jnp.concatenate(o_inter, axis=0)
    o_ref[...] = _group_norm_gate(o, gr_ref[...], ng_ref[...]).astype(o_ref.dtype)


def _hgrn_sample(proj, lower_bounds, norm_g, state, seq):
    b, n_heads, dk, dv = state.shape
    m = proj.shape[0]
    bs = _pick(b, (64, 32, 16, 8, 4, 2))
    rows = bs * seq
    col = lambda blk: pl.BlockSpec((rows, dk), lambda h, i: (i, blk * n_heads + h))
    o, s = pl.pallas_call(
        functools.partial(_hgrn_sample_kernel, seq=seq),
        out_shape=(jax.ShapeDtypeStruct((m, n_heads * dv), BF16),
                   jax.ShapeDtypeStruct(state.shape, F32)),
        grid=(n_heads, b // bs),
        in_specs=[col(0), col(1), col(2), col(3),
                  pl.BlockSpec((lower_bounds.shape[0], dk), lambda h, i: (0, h)),
                  pl.BlockSpec((1, dv), lambda h, i: (0, h)),
                  pl.BlockSpec((bs, 1, dk, dv), lambda h, i: (i, h, 0, 0))],
        out_specs=(pl.BlockSpec((rows, dv), lambda h, i: (i, h)),
                   pl.BlockSpec((bs, 1, dk, dv), lambda h, i: (i, h, 0, 0))),
        compiler_params=_cparams(("arbitrary", "arbitrary"), V7X_VMEM_LIMIT),
        name="hgrn_sample",
    )(proj, proj, proj, proj, lower_bounds, norm_g.reshape(1, -1), state)
    return o, s


def kernel(x_prompt, x_sample, cache_swa_k, cache_swa_v, state_hgrn, state_ffn_conv, c_prompt, c_sample,
           norm1_g, norm2_g, w_ada, b_ada, attn_w_qkv, attn_w_o, attn_sinks,
           hgrn_w_in, hgrn_lower_bounds, hgrn_norm_g, hgrn_w_o,
           ffn_w_in, ffn_conv_w, ffn_conv_b, ffn_w_out, final_norm_g):
    bp, lp, d = x_prompt.shape
    bsmp, ls, _ = x_sample.shape
    assert bp == 1, "the prompt kernels carry one sequence"
    depth = w_ada.shape[0]
    n_kv, hd = cache_swa_k.shape[2], cache_swa_k.shape[3]
    n_heads = attn_sinks.shape[0]
    dq = n_heads * hd
    dkv = n_kv * hd
    hg_heads, hg_dk = state_hgrn.shape[1], state_hgrn.shape[2]
    keep = min(WINDOW, lp)

    ns = bsmp * ls
    assert ns % 8 == 0
    assert bsmp % 8 == 0
    c_seq = jnp.concatenate([c_sample, c_prompt, jnp.zeros((7, d), F32)], axis=0)
    mod = _ada(c_seq, w_ada, b_ada, ls)
    pk = dict(sample=False, n_sample=ns)
    sk = dict(sample=True, n_sample=ns)

    xp = x_prompt.reshape(lp, d)
    xs = x_sample.reshape(ns, d)
    conv_p, conv_s = [], []
    outs = {}
    w_out_bf = ffn_w_out.astype(BF16)
    hp = hs = None
    for l in range(depth):
        if l % 2 == 0:
            if hp is None:
                qkv_p = _matmul_mod(xp, norm1_g, mod, l, 0, 1, attn_w_qkv, False, ns, tm=1024, tn=1024,
                                    name="qkv_prompt")
                qkv_s = _matmul_mod(xs, norm1_g, mod, l, 0, 1, attn_w_qkv, True, ns, tm=512, tn=512,
                                    name="qkv_sample")
            else:
                qkv_p = _matmul(hp, attn_w_qkv, tn=1024, name="qkv_prompt")
                qkv_s = _matmul(hs, attn_w_qkv, name="qkv_sample")
            op = _swa_prompt(qkv_p, attn_sinks, n_kv, hd)
            outs['swa_k_prompt'] = qkv_p[lp - keep:, dq:dq + dkv].reshape(bp, keep, n_kv, hd)
            outs['swa_v_prompt'] = qkv_p[lp - keep:, dq + dkv:].reshape(bp, keep, n_kv, hd)
            outs['swa_k_sample'] = qkv_s[:, dq:dq + dkv].reshape(bsmp, ls, n_kv, hd)
            outs['swa_v_sample'] = qkv_s[:, dq + dkv:].reshape(bsmp, ls, n_kv, hd)
            os_ = _swa_sample(qkv_s[:, :dq], qkv_s[:, dq:dq + dkv], qkv_s[:, dq + dkv:],
                              cache_swa_k, cache_swa_v, attn_sinks, n_kv, hd, ls)
            w_o = attn_w_o
        else:
            if hp is None:
                hp = _modulate(xp, norm1_g, mod, l, 0, 1, False, ns)
                hs = _modulate(xs, norm1_g, mod, l, 0, 1, True, ns)
            op, sp = _hgrn_prompt(hp, hgrn_w_in, hgrn_lower_bounds, hgrn_norm_g, hg_heads, hg_dk)
            proj_s = _matmul(hs, hgrn_w_in, name="hgrn_in_sample")
            os_, ss = _hgrn_sample(proj_s, hgrn_lower_bounds, hgrn_norm_g, state_hgrn, ls)
            outs['hgrn_state_prompt'] = sp
            outs['hgrn_state_sample'] = ss
            w_o = hgrn_w_o
        xp, hp = _mm_rows(op, w_o, xp, mod, gate=(l, 2), norm_g=norm2_g, norm_layer=l, next_mod=(l, 3, 4),
                          tm=512, name="mix_out_prompt", **pk)
        xs, hs = _mm_rows(os_, w_o, xs, mod, gate=(l, 2), norm_g=norm2_g, norm_layer=l, next_mod=(l, 3, 4),
                          tm=512, name="mix_out_sample", **sk)
        actp, bufp = _ffn_in_prompt(hp, ffn_w_in, ffn_conv_w, ffn_conv_b, l)
        acts, bufs = _ffn_in_sample(hs, ffn_w_in, ffn_conv_w, ffn_conv_b, l, state_ffn_conv, ls)
        conv_p.append(bufp[None])
        conv_s.append(bufs)
        if l + 1 < depth:
            nxt = dict(norm_g=norm1_g, norm_layer=l + 1, next_mod=(l + 1, 0, 1))
        else:
            nxt = dict(norm_g=final_norm_g, norm_layer=0, next_mod=None, keep_x=False, out_dtype=F32)
        xp, hp = _mm_rows(actp, w_out_bf, xp, mod, gate=(l, 5), w_layer=l, tm=256, name="ffn_out_prompt",
                          **nxt, **pk)
        xs, hs = _mm_rows(acts, w_out_bf, xs, mod, gate=(l, 5), w_layer=l, tm=256, name="ffn_out_sample",
                          **nxt, **sk)
    y_prompt = hp.reshape(bp, lp, d)
    y_sample = hs.reshape(bsmp, ls, d)
    return (y_prompt, y_sample,
            outs['swa_k_prompt'], outs['swa_v_prompt'], outs['swa_k_sample'], outs['swa_v_sample'],
            outs['hgrn_state_prompt'], outs['hgrn_state_sample'],
            jnp.stack(conv_p), jnp.stack(conv_s))
```

```python
import functools
import math

import numpy as np
import jax
import jax.numpy as jnp
from jax import lax
from jax.experimental import pallas as pl
from jax.experimental.pallas import tpu as pltpu

F32 = jnp.float32
BF16 = jnp.bfloat16

EPS = 1e-6
LOG2E = math.log2(math.e)
WINDOW = 128
CONV_W = 3
LANES = 128
HG_CHUNK = 256
HG_ROWS = 2048
SWA_WAVE = 8
V7X_VMEM_LIMIT = 56 * 1024 * 1024


def _cparams(sem, vmem=None):
    return pltpu.CompilerParams(dimension_semantics=sem, vmem_limit_bytes=vmem)


def _pick(n, cands):
    for c in cands:
        if n % c == 0:
            return c
    return n


def _silu(x):
    return x * jax.nn.sigmoid(x)


def _logaddexp(a, b):
    return jnp.maximum(a, b) + jnp.log1p(jnp.exp(-jnp.abs(a - b)))


def _ada_kernel(c_ref, w_ref, b_ref, o_ref, act, slab, *, n_seq, seq):
    @pl.when((pl.program_id(0) == 0) & (pl.program_id(1) == 0))
    def _():
        act[...] = _silu(c_ref[...]).astype(BF16)

    w = w_ref[0].astype(BF16)
    res = jnp.dot(act[...], w, preferred_element_type=F32) + b_ref[0]
    for c in range(slab.shape[0]):
        cs = slice(c * LANES, (c + 1) * LANES)
        for t in range(seq):
            slab[c, pl.ds(t, n_seq, stride=seq), :] = res[:n_seq, cs]
        slab[c, n_seq * seq:, :] = res[n_seq:, cs]
        o_ref[0, :, cs] = slab[c]


def _ada(c_seq, w_ada, b_ada, seq):
    depth, d, n = w_ada.shape
    r = c_seq.shape[0]
    n_seq = r - 8
    rows = n_seq * seq + 8
    tn = _pick(n, (2048, 1024, 512, 256, 128))
    return pl.pallas_call(
        functools.partial(_ada_kernel, n_seq=n_seq, seq=seq),
        out_shape=jax.ShapeDtypeStruct((depth, rows, n), F32),
        grid=(depth, n // tn),
        in_specs=[pl.BlockSpec((r, d), lambda l, j: (0, 0)),
                  pl.BlockSpec((1, d, tn), lambda l, j: (l, 0, j)),
                  pl.BlockSpec((1, 1, tn), lambda l, j: (l, 0, j))],
        out_specs=pl.BlockSpec((1, rows, tn), lambda l, j: (l, 0, j)),
        scratch_shapes=[pltpu.VMEM((r, d), BF16), pltpu.VMEM((tn // LANES, rows, LANES), F32)],
        compiler_params=_cparams(("arbitrary", "arbitrary"), V7X_VMEM_LIMIT),
        name="ada",
    )(c_seq, w_ada, b_ada.reshape(depth, 1, n))


def _mod_rows(sample, tm, n_sample):
    if sample:
        return tm, (lambda i: i)
    return 8, (lambda i: n_sample // 8)


def _modulate_kernel(x_ref, g_ref, sc_ref, sh_ref, o_ref, *, sample):
    x = x_ref[...]
    ms = jnp.mean(x * x, axis=-1, keepdims=True)
    y = x * lax.rsqrt(ms + EPS) * g_ref[...]
    sc = sc_ref[...] if sample else sc_ref[0:1, :]
    sh = sh_ref[...] if sample else sh_ref[0:1, :]
    o_ref[...] = (y * (1.0 + sc) + sh).astype(o_ref.dtype)


def _modulate(x, g_all, mod, l, shift_col, scale_col, sample, n_sample):
    m, d = x.shape
    tm = _pick(m, (512, 256, 128, 64, 32, 16, 8))
    rows, rmap = _mod_rows(sample, tm, n_sample)
    mspec = lambda col: pl.BlockSpec((None, rows, d), lambda i: (l, rmap(i), col))
    return pl.pallas_call(
        functools.partial(_modulate_kernel, sample=sample),
        out_shape=jax.ShapeDtypeStruct((m, d), BF16),
        grid=(m // tm,),
        in_specs=[pl.BlockSpec((tm, d), lambda i: (i, 0)),
                  pl.BlockSpec((None, 1, d), lambda i: (l, 0, 0)),
                  mspec(scale_col), mspec(shift_col)],
        out_specs=pl.BlockSpec((tm, d), lambda i: (i, 0)),
        compiler_params=_cparams(("arbitrary",), V7X_VMEM_LIMIT),
        name="modulate",
    )(x, g_all.reshape(g_all.shape[0], 1, d), mod, mod)


def _mm_kernel(a_ref, w_ref, o_ref, wbf):
    @pl.when(pl.program_id(1) == 0)
    def _():
        wbf[...] = w_ref[...].astype(BF16)

    o_ref[...] = jnp.dot(a_ref[...], wbf[...], preferred_element_type=F32)


def _mm_mod_kernel(x_ref, g_ref, sc_ref, sh_ref, w_ref, o_ref, wbf, *, sample):
    @pl.when(pl.program_id(1) == 0)
    def _():
        wbf[...] = w_ref[...].astype(BF16)

    rows = (lambda r: r[...]) if sample else (lambda r: r[0:1, :])
    x = x_ref[...]
    y = x * lax.rsqrt(jnp.mean(x * x, axis=-1, keepdims=True) + EPS) * g_ref[...]
    h = (y * (1.0 + rows(sc_ref)) + rows(sh_ref)).astype(BF16)
    o_ref[...] = jnp.dot(h, wbf[...], preferred_element_type=F32)


def _matmul_mod(x, g_all, mod, l, shift_col, scale_col, w, sample, n_sample, *, tm, tn, name):
    m, d = x.shape
    n = w.shape[-1]
    tm = min(tm, m)
    tn = tn if n % tn == 0 else _pick(n, (512, 256, 128))
    assert m % tm == 0 and n % tn == 0
    rows, rmap = _mod_rows(sample, tm, n_sample)
    mspec = lambda col: pl.BlockSpec((None, rows, d), lambda j, i: (l, rmap(i), col))
    return pl.pallas_call(
        functools.partial(_mm_mod_kernel, sample=sample),
        out_shape=jax.ShapeDtypeStruct((m, n), F32),
        grid=(n // tn, m // tm),
        in_specs=[pl.BlockSpec((tm, d), lambda j, i: (i, 0)),
                  pl.BlockSpec((None, 1, d), lambda j, i: (l, 0, 0)),
                  mspec(scale_col), mspec(shift_col),
                  pl.BlockSpec((d, tn), lambda j, i: (0, j))],
        out_specs=pl.BlockSpec((tm, tn), lambda j, i: (i, j)),
        scratch_shapes=[pltpu.VMEM((d, tn), BF16)],
        compiler_params=_cparams(("arbitrary", "arbitrary"), V7X_VMEM_LIMIT),
        name=name,
    )(x, g_all.reshape(g_all.shape[0], 1, d), mod, mod, w)


def _matmul(a, w, *, tm=None, tn=None, name="matmul"):
    m, k = a.shape
    n = w.shape[-1]
    tm = min(tm, m) if tm else _pick(m, (1024, 512, 256, 128, 64, 32, 16, 8))
    tn = min(tn, n) if tn else _pick(n, (512, 256, 128))
    assert m % tm == 0 and n % tn == 0
    return pl.pallas_call(
        _mm_kernel,
        out_shape=jax.ShapeDtypeStruct((m, n), F32),
        grid=(n // tn, m // tm),
        in_specs=[pl.BlockSpec((tm, k), lambda j, i: (i, 0)),
                  pl.BlockSpec((k, tn), lambda j, i: (0, j))],
        out_specs=pl.BlockSpec((tm, tn), lambda j, i: (i, j)),
        scratch_shapes=[pltpu.VMEM((k, tn), BF16)],
        compiler_params=_cparams(("arbitrary", "arbitrary"), V7X_VMEM_LIMIT),
        name=name,
    )(a, w)


def _mm_rows_kernel(*refs, cast_w, sample, modulate, keep_x):
    it = iter(refs)
    a_ref, w_ref, x_ref, gate_ref, g_ref = (next(it) for _ in range(5))
    sc_ref, sh_ref = (next(it), next(it)) if modulate else (None, None)
    xo_ref = next(it) if keep_x else None
    h_ref = next(it)
    if cast_w:
        wbf = next(it)

        @pl.when(pl.program_id(0) == 0)
        def _():
            wbf[...] = w_ref[...].astype(BF16)

        w = wbf[...]
    else:
        w = w_ref[...]
    rows = (lambda r: r[...]) if sample else (lambda r: r[0:1, :])
    x = x_ref[...] + rows(gate_ref) * jnp.dot(a_ref[...], w, preferred_element_type=F32)
    if keep_x:
        xo_ref[...] = x
    y = x * lax.rsqrt(jnp.mean(x * x, axis=-1, keepdims=True) + EPS) * g_ref[...]
    if modulate:
        y = y * (1.0 + rows(sc_ref)) + rows(sh_ref)
    h_ref[...] = y.astype(h_ref.dtype)


def _mm_rows(a, w, x, mod, *, gate, norm_g, norm_layer, next_mod=None, w_layer=None, sample, n_sample,
             tm, keep_x=True, out_dtype=None, name):
    m, k = a.shape
    n = w.shape[-1]
    tm = min(tm, m)
    assert m % tm == 0
    cast_w = w.dtype != BF16
    rows, rmap = _mod_rows(sample, tm, n_sample)
    once = pl.Buffered(1)
    if w_layer is None:
        w_spec = pl.BlockSpec((k, n), lambda i: (0, 0), pipeline_mode=once)
    else:
        w_spec = pl.BlockSpec((None, k, n), lambda i: (w_layer, 0, 0), pipeline_mode=once)
    mspec = lambda layer, col: pl.BlockSpec((None, rows, n), lambda i: (layer, rmap(i), col))
    in_specs = [pl.BlockSpec((tm, k), lambda i: (i, 0)), w_spec,
                pl.BlockSpec((tm, n), lambda i: (i, 0)),
                mspec(*gate),
                pl.BlockSpec((None, 1, n), lambda i: (norm_layer, 0, 0))]
    args = [a, w, x, mod, norm_g.reshape(-1, 1, n)]
    if next_mod is not None:
        in_specs += [mspec(next_mod[0], next_mod[2]), mspec(next_mod[0], next_mod[1])]
        args += [mod, mod]
    out_shape, out_specs = [], []
    if keep_x:
        out_shape.append(jax.ShapeDtypeStruct((m, n), F32))
        out_specs.append(pl.BlockSpec((tm, n), lambda i: (i, 0)))
    out_shape.append(jax.ShapeDtypeStruct((m, n), out_dtype or BF16))
    out_specs.append(pl.BlockSpec((tm, n), lambda i: (i, 0)))
    res = pl.pallas_call(
        functools.partial(_mm_rows_kernel, cast_w=cast_w, sample=sample, modulate=next_mod is not None,
                          keep_x=keep_x),
        out_shape=tuple(out_shape),
        grid=(m // tm,),
        in_specs=in_specs,
        out_specs=tuple(out_specs),
        scratch_shapes=[pltpu.VMEM((k, n), BF16)] if cast_w else [],
        compiler_params=_cparams(("arbitrary",), V7X_VMEM_LIMIT),
        name=name,
    )(*args)
    return res if keep_x else (None, res[0])


def _gelu_exact(a):
    return 0.5 * a * (1.0 + lax.erf(a * (1.0 / math.sqrt(2.0))))


def _ffn_in_prompt_kernel(a_ref, wu_ref, wg_ref, cw_ref, cb_ref, act_ref, tail_ref, wbf, carry):
    i = pl.program_id(1)
    tf = wu_ref.shape[1]

    @pl.when(i == 0)
    def _():
        wbf[:, :tf] = wu_ref[...].astype(BF16)
        wbf[:, tf:] = wg_ref[...].astype(BF16)
        carry[...] = jnp.zeros_like(carry)

    ug = jnp.dot(a_ref[...], wbf[...], preferred_element_type=F32)
    u = ug[:, :tf]
    g = ug[:, tf:]
    tm = u.shape[0]
    row = lax.broadcasted_iota(jnp.int32, u.shape, 0)
    c = carry[...]
    u1 = jnp.where(row == 0, c[7:8, :], pltpu.roll(u, 1, 0))
    u2 = jnp.where(row == 0, c[6:7, :], jnp.where(row == 1, c[7:8, :], pltpu.roll(u, 2, 0)))
    cw = cw_ref[...]
    conv = cw[0:1, :] * u2 + cw[1:2, :] * u1 + cw[2:3, :] * u + cb_ref[...]
    act_ref[...] = (_gelu_exact(conv) * g).astype(act_ref.dtype)
    carry[...] = u[tm - 8:, :]
    tail_ref[...] = u[tm - 8:, :]


def _ffn_in_sample_kernel(a_ref, wu_ref, wg_ref, cw_ref, cb_ref, buf_ref, act_ref, nbuf_ref,
                          us, gs, acts, *, seq):
    a = a_ref[...]
    b = buf_ref.shape[0]
    u_all = jnp.dot(a, wu_ref[...].astype(BF16), preferred_element_type=F32)
    g_all = jnp.dot(a, wg_ref[...].astype(BF16), preferred_element_type=F32)
    cw = cw_ref[...]
    cb = cb_ref[...]
    for c in range(us.shape[0]):
        cs = slice(c * LANES, (c + 1) * LANES)
        us[c] = u_all[:, cs]
        gs[c] = g_all[:, cs]
        prev2, prev1 = buf_ref[:, 0, cs], buf_ref[:, 1, cs]
        for t in range(seq):
            u = us[c, pl.ds(t, b, stride=seq), :]
            conv = cw[0:1, cs] * prev2 + cw[1:2, cs] * prev1 + cw[2:3, cs] * u + cb[:, cs]
            acts[c, pl.ds(t, b, stride=seq), :] = _gelu_exact(conv) * gs[c, pl.ds(t, b, stride=seq), :]
            prev2, prev1 = prev1, u
        act_ref[:, cs] = acts[c].astype(act_ref.dtype)
        nbuf_ref[:, 0, cs] = prev2
        nbuf_ref[:, 1, cs] = prev1


def _ffn_in_prompt(a, w_in, conv_w, conv_b, l):
    m, d = a.shape
    f = w_in.shape[-1] // 2
    tf = _pick(f, (512, 256, 128))
    tm = _pick(m, (1024, 512, 256, 128, 64, 32, 16, 8))
    nf = f // tf
    act, tail = pl.pallas_call(
        _ffn_in_prompt_kernel,
        out_shape=(jax.ShapeDtypeStruct((m, f), BF16), jax.ShapeDtypeStruct((8, f), F32)),
        grid=(nf, m // tm),
        in_specs=[pl.BlockSpec((tm, d), lambda j, i: (i, 0)),
                  pl.BlockSpec((None, d, tf), lambda j, i: (l, 0, j)),
                  pl.BlockSpec((None, d, tf), lambda j, i: (l, 0, nf + j)),
                  pl.BlockSpec((None, CONV_W, tf), lambda j, i: (l, 0, j)),
                  pl.BlockSpec((None, 1, tf), lambda j, i: (l, 0, j))],
        out_specs=(pl.BlockSpec((tm, tf), lambda j, i: (i, j)),
                   pl.BlockSpec((8, tf), lambda j, i: (0, j))),
        scratch_shapes=[pltpu.VMEM((d, 2 * tf), BF16), pltpu.VMEM((8, tf), F32)],
        compiler_params=_cparams(("arbitrary", "arbitrary"), V7X_VMEM_LIMIT),
        name="ffn_in_prompt",
    )(a, w_in, w_in, conv_w, conv_b.reshape(conv_b.shape[0], 1, f))
    return act, tail[6:8]


def _ffn_in_sample(a, w_in, conv_w, conv_b, l, state, seq):
    m, d = a.shape
    f = w_in.shape[-1] // 2
    b = state.shape[1]
    tf = _pick(f, (512, 256, 128))
    nf = f // tf
    return pl.pallas_call(
        functools.partial(_ffn_in_sample_kernel, seq=seq),
        out_shape=(jax.ShapeDtypeStruct((m, f), BF16), jax.ShapeDtypeStruct((b, CONV_W - 1, f), F32)),
        grid=(nf,),
        in_specs=[pl.BlockSpec((m, d), lambda j: (0, 0)),
                  pl.BlockSpec((None, d, tf), lambda j: (l, 0, j)),
                  pl.BlockSpec((None, d, tf), lambda j: (l, 0, nf + j)),
                  pl.BlockSpec((None, CONV_W, tf), lambda j: (l, 0, j)),
                  pl.BlockSpec((None, 1, tf), lambda j: (l, 0, j)),
                  pl.BlockSpec((None, b, CONV_W - 1, tf), lambda j: (l, 0, 0, j))],
        out_specs=(pl.BlockSpec((m, tf), lambda j: (0, j)),
                   pl.BlockSpec((b, CONV_W - 1, tf), lambda j: (0, 0, j))),
        scratch_shapes=[pltpu.VMEM((tf // LANES, m, LANES), F32)] * 3,
        compiler_params=_cparams(("arbitrary",), V7X_VMEM_LIMIT),
        name="ffn_in_sample",
    )(a, w_in, w_in, conv_w, conv_b.reshape(conv_b.shape[0], 1, f), state)


def _alibi_slope(h, n_heads):
    return 2.0 ** (-8.0 * (h + 1) / n_heads)


def _swa_prompt_bias(n_heads, tq):
    t = np.arange(tq)[:, None]
    s = np.arange(2 * tq)[None, :]
    dist = t + tq - s
    ok = (dist >= 0) & (dist < WINDOW)
    slope = (2.0 ** (-8.0 * (np.arange(n_heads) + 1) / n_heads) * LOG2E)[:, None, None]
    table = np.stack([np.where(ok & (s >= tq), slope * s, -np.inf), np.where(ok, slope * s, -np.inf)])
    return jnp.asarray(table, F32)


def _swa_prompt_kernel(sink_ref, q_ref, kp_ref, kc_ref, vp_ref, vc_ref, bias_ref, o_ref, *, n_kv, group, hd):
    n_heads = n_kv * group
    tq = q_ref.shape[0]
    trow = (lax.broadcasted_iota(jnp.int32, (tq, 1), 0) + tq).astype(F32)
    qscale = hd ** -0.5 * LOG2E
    nt = (((1,), (1,)), ((), ()))
    def scores(kv):
        ks = slice(kv * hd, (kv + 1) * hd)
        k2 = jnp.concatenate([kp_ref[:, ks], kc_ref[:, ks]], axis=0).astype(BF16)
        q4 = jnp.concatenate([(q_ref[:, h * hd:(h + 1) * hd] * qscale).astype(BF16)
                              for h in range(kv * group, (kv + 1) * group)], axis=0)
        return lax.dot_general(q4, k2, nt, preferred_element_type=F32)

    for kv0 in range(0, n_kv, SWA_WAVE):
        kvs = list(range(kv0, min(kv0 + SWA_WAVE, n_kv)))
        sc4s = [scores(kv) for kv in kvs]
        heads = [kv * group + g for kv in kvs for g in range(group)]
        slopes = [_alibi_slope(h, n_heads) * LOG2E for h in heads]
        scs = [sc4s[j // group][(j % group) * tq:(j % group + 1) * tq] + bias_ref[h]
               for j, h in enumerate(heads)]
        sinks = [sink_ref[h] * LOG2E + slopes[j] * trow for j, h in enumerate(heads)]
        mxs = [jnp.maximum(jnp.max(scs[j], axis=-1, keepdims=True), sinks[j]) for j in range(len(heads))]
        es = [jnp.exp2(scs[j] - mxs[j]) for j in range(len(heads))]
        dens = [jnp.sum(es[j], axis=-1, keepdims=True) + jnp.exp2(sinks[j] - mxs[j])
                for j in range(len(heads))]
        es = [e.astype(BF16) for e in es]
        o4s = []
        for i, kv in enumerate(kvs):
            ks = slice(kv * hd, (kv + 1) * hd)
            v2 = jnp.concatenate([vp_ref[:, ks], vc_ref[:, ks]], axis=0).astype(BF16)
            o4s.append(jnp.dot(jnp.concatenate(es[i * group:(i + 1) * group], axis=0), v2,
                               preferred_element_type=F32))
        for j, h in enumerate(heads):
            o = o4s[j // group][(j % group) * tq:(j % group + 1) * tq]
            o_ref[:, h * hd:(h + 1) * hd] = (o / dens[j]).astype(o_ref.dtype)


def _swa_prompt(qkv, sinks, n_kv, hd):
    l = qkv.shape[0]
    n_heads = sinks.shape[0]
    group = n_heads // n_kv
    dq = n_heads * hd
    dk = n_kv * hd
    nb = l // WINDOW
    kblk = dq // dk
    bias = _swa_prompt_bias(n_heads, WINDOW)
    return pl.pallas_call(
        functools.partial(_swa_prompt_kernel, n_kv=n_kv, group=group, hd=hd),
        out_shape=jax.ShapeDtypeStruct((l, dq), BF16),
        grid_spec=pltpu.PrefetchScalarGridSpec(
            num_scalar_prefetch=1,
            grid=(nb,),
            in_specs=[pl.BlockSpec((WINDOW, dq), lambda i, s: (i, 0)),
                      pl.BlockSpec((WINDOW, dk), lambda i, s: (jnp.maximum(i - 1, 0), kblk)),
                      pl.BlockSpec((WINDOW, dk), lambda i, s: (i, kblk)),
                      pl.BlockSpec((WINDOW, dk), lambda i, s: (jnp.maximum(i - 1, 0), kblk + 1)),
                      pl.BlockSpec((WINDOW, dk), lambda i, s: (i, kblk + 1)),
                      pl.BlockSpec((None,) + bias.shape[1:], lambda i, s: (jnp.minimum(i, 1), 0, 0, 0))],
            out_specs=pl.BlockSpec((WINDOW, dq), lambda i, s: (i, 0)),
        ),
        compiler_params=_cparams(("arbitrary",), V7X_VMEM_LIMIT),
        name="swa_prompt",
    )(sinks, qkv, qkv, qkv, qkv, qkv, bias)


def _swa_sample_tables(n_heads, n_kv, hd, seq, cr):
    group = n_heads // n_kv
    r = np.arange(seq * n_heads)
    t, h = r // n_heads, r % n_heads
    slope = 2.0 ** (-8.0 * (h + 1) / n_heads)
    slot = (np.arange(n_kv * hd)[None, :] // hd == (h // group)[:, None]).astype(np.float32)
    dist = cr + t[:, None] - np.arange(cr)[None, :]
    bias_c = np.where((dist >= 0) & (dist < WINDOW), -slope[:, None] * dist, -np.inf) * LOG2E
    j = np.arange(LANES)
    distn = t[:, None] - j[None, :]
    bias_n = np.where((distn >= 0) & (j[None, :] < seq), -slope[:, None] * distn, -np.inf) * LOG2E
    return jnp.asarray(slot, F32), jnp.asarray(np.concatenate([bias_c, bias_n], axis=1), F32)


def _swa_sample_kernel(q_ref, kt_ref, vt_ref, kn_ref, vn_ref, slot_ref, bias_ref, sink_ref, o_ref, *, hd):
    bs = q_ref.shape[0]
    cr = kt_ref.shape[2]
    width = slot_ref.shape[1]
    qscale = hd ** -0.5 * LOG2E
    slot = slot_ref[...]
    bias = bias_ref[...]
    sink2 = sink_ref[...] * LOG2E
    nt = (((1,), (1,)), ((), ()))

    def slotted(q):
        return jnp.concatenate([q] * (width // hd), axis=1) * slot

    qs = [slotted(q_ref[b] * qscale) for b in range(bs)]
    qb = [q.astype(BF16) for q in qs]
    zrows = jnp.zeros((LANES - kn_ref.shape[1], width), F32)
    kn = [jnp.concatenate([kn_ref[b], zrows], axis=0).astype(BF16) for b in range(bs)]
    vn = [jnp.concatenate([vn_ref[b], zrows], axis=0).astype(BF16) for b in range(bs)]
    sc = [jnp.concatenate([jnp.dot(qb[b], kt_ref[b].astype(BF16), preferred_element_type=F32),
                           lax.dot_general(qb[b], kn[b], nt, preferred_element_type=F32)], axis=1) + bias
          for b in range(bs)]
    mx = [jnp.maximum(jnp.max(s, axis=-1, keepdims=True), sink2) for s in sc]
    es = [jnp.exp2(s - m) for s, m in zip(sc, mx)]
    den = [jnp.sum(e, axis=-1, keepdims=True) + jnp.exp2(sink2 - m) for e, m in zip(es, mx)]
    es = [e.astype(BF16) for e in es]
    o_w = [lax.dot_general(es[b][:, :cr], vt_ref[b].astype(BF16), nt, preferred_element_type=F32)
           + jnp.dot(es[b][:, cr:], vn[b], preferred_element_type=F32) for b in range(bs)]
    for b in range(bs):
        o = o_w[b] * slot
        o = functools.reduce(jnp.add, [o[:, c * hd:(c + 1) * hd] for c in range(width // hd)])
        o_ref[b] = (o / den[b]).astype(o_ref.dtype)


def _swa_sample(q, k_new, v_new, cache_k, cache_v, sinks, n_kv, hd, seq):
    b, cr = cache_k.shape[0], cache_k.shape[1]
    n_heads = sinks.shape[0]
    rows = seq * n_heads
    width = n_kv * hd
    slot, bias = _swa_sample_tables(n_heads, n_kv, hd, seq, cr)
    kt = cache_k.transpose(0, 2, 3, 1).reshape(b, width, cr)
    vt = cache_v.transpose(0, 2, 3, 1).reshape(b, width, cr)
    new_rows = -(-seq // 8) * 8
    pad_new = lambda x: jnp.pad(x.reshape(b, seq, width), ((0, 0), (0, new_rows - seq), (0, 0)))
    bs = _pick(b, (32, 16, 8, 4, 2, 1))
    const = lambda shape: pl.BlockSpec(shape, lambda i: (0, 0))
    o = pl.pallas_call(
        functools.partial(_swa_sample_kernel, hd=hd),
        out_shape=jax.ShapeDtypeStruct((b, rows, hd), BF16),
        grid=(b // bs,),
        in_specs=[pl.BlockSpec((bs, rows, hd), lambda i: (i, 0, 0)),
                  pl.BlockSpec((bs, width, cr), lambda i: (i, 0, 0)),
                  pl.BlockSpec((bs, width, cr), lambda i: (i, 0, 0)),
                  pl.BlockSpec((bs, new_rows, width), lambda i: (i, 0, 0)),
                  pl.BlockSpec((bs, new_rows, width), lambda i: (i, 0, 0)),
                  const(slot.shape), const(bias.shape), const((rows, 1))],
        out_specs=pl.BlockSpec((bs, rows, hd), lambda i: (i, 0, 0)),
        compiler_params=_cparams(("arbitrary",), V7X_VMEM_LIMIT),
        name="swa_sample",
    )(q.reshape(b, rows, hd), kt, vt, pad_new(k_new), pad_new(v_new),
      slot, bias, jnp.tile(sinks, seq).reshape(rows, 1))
    return o.reshape(b * seq, n_heads * hd)


def _lower_bound(p_ref):
    p0 = p_ref[0:1, :]
    p1 = p_ref[1:2, :]
    mx = jnp.maximum(p0, p1)
    e0 = jnp.exp(p0 - mx)
    e1 = jnp.exp(p1 - mx)
    return e1 / (e0 + e1)


def _hgrn_gates(fr, lb):
    e = jnp.exp(-jnp.abs(fr))
    log_sig = jnp.minimum(fr, 0.0) - jnp.log1p(e)
    logf = _logaddexp(jnp.log(lb), jnp.log1p(-lb) + log_sig)
    r = 1.0 / (1.0 + e)
    k = (1.0 - lb) * jnp.where(fr >= 0.0, e * r, r)
    return logf, k


def _group_norm_gate(o, gr, ng):
    o = o * lax.rsqrt(jnp.mean(o * o, axis=-1, keepdims=True) + EPS) * ng
    return o * _silu(gr)


SMALL = 8

def _level_tables(c):
    g = min(c, 128)
    t = np.arange(g)[:, None]
    s = np.arange(g)[None, :]
    hb = np.floor(np.log2(np.maximum(t ^ s, 1))).astype(np.int32)
    same = (t // SMALL) == (s // SMALL)
    near = np.where(same & (s < t), hb, np.where(s == t, int(math.log2(SMALL)), -1))
    i = np.arange(c // 2)[:, None]
    j = np.arange(c // 2)[None, :]
    far = np.where(i == j, -1, np.floor(np.log2(np.maximum(i ^ j, 1)))).astype(np.int32)
    return jnp.asarray(near, dtype=jnp.int32), jnp.asarray(far, dtype=jnp.int32)


def _group_ref_rows(p, m, row):
    c, w = p.shape
    if m == 1:
        return jnp.where((row & 1) == 1, pltpu.roll(p, 1, 0), p)
    if m == 2:
        r4 = row & 3
        return jnp.where(r4 == 0, pltpu.roll(p, c - 1, 0),
                         jnp.where(r4 == 1, p,
                                   jnp.where(r4 == 2, pltpu.roll(p, 1, 0), pltpu.roll(p, 2, 0))))
    p3 = p.reshape(c // 8, 8, w)
    return jnp.broadcast_to(p3[:, 3:4, :], (c // 8, 8, w)).reshape(c, w)


def _halves(x, m):
    c = x.shape[0]
    ev = [x[r:r + m] for r in range(0, c, 2 * m)]
    od = [x[r + m:r + 2 * m] for r in range(0, c, 2 * m)]
    cat = lambda xs: xs[0] if len(xs) == 1 else jnp.concatenate(xs, axis=0)
    return cat(ev), cat(od)


def _interleave(ev, od, m):
    parts = []
    for r in range(0, ev.shape[0], m):
        parts += [ev[r:r + m], od[r:r + m]]
    return jnp.concatenate(parts, axis=0)


def _hgrn_chunks(chunks, st, near, far):
    c, w = chunks[0][0].shape
    nlev = int(math.log2(c))
    nsmall = int(math.log2(SMALL))
    g = near.shape[0]
    row = lax.broadcasted_iota(jnp.int32, (c, w), 0)
    nt = (((1,), (1,)), ((), ()))
    dot_nt = lambda x, y: lax.dot_general(x, y, nt, preferred_element_type=F32)
    dot_nn = lambda x, y: jnp.dot(x, y, preferred_element_type=F32)

    near_ops, far_ops, tails = [], [], []
    for q, k, v, logf in chunks:
        p = logf
        qs, ks = [q.astype(BF16)], [k.astype(BF16)]
        for l in range(nsmall):
            gref = _group_ref_rows(p, 1 << l, row)
            odd = ((row >> l) & 1) == 1
            e = jnp.exp(jnp.where(odd, p, gref - p))
            qs.append((q * e).astype(BF16))
            ks.append((k * e).astype(BF16))
            p = p + jnp.where(odd, gref, 0.0)
        near_ops.append((qs, ks, v.astype(BF16)))
        lev = []
        for l in range(nsmall, nlev):
            m = 1 << l
            pe, po = _halves(p, m)
            tot = jnp.concatenate([jnp.broadcast_to(pe[r + m - 1:r + m], (m, w))
                                   for r in range(0, c // 2, m)], axis=0)
            q_od = _halves(q, m)[1]
            k_ev = _halves(k, m)[0]
            v_ev = _halves(v, m)[0].astype(BF16)
            lev.append(((q_od * jnp.exp(po)).astype(BF16), (k_ev * jnp.exp(tot - pe)).astype(BF16), v_ev))
            p = _interleave(pe, po + tot, m)
        far_ops.append(lev)
        b = p
        b_end = b[c - 1:c, :]
        tails.append(((q * jnp.exp(b)).astype(BF16), (k * jnp.exp(b_end - b)).astype(BF16),
                      jnp.exp(b_end)))

    near_sc = [[[dot_nt(qs[j][r:r + g], ks[j][r:r + g]) for j in range(nsmall + 1)]
                for r in range(0, c, g)] for qs, ks, _ in near_ops]
    far_sc = [[dot_nt(qt, kt) for qt, kt, _ in lev] for lev in far_ops]
    upds = [lax.dot_general(vb, k_end, (((0,), (0,)), ((), ())), preferred_element_type=F32)
            for (_, _, vb), (_, k_end, _) in zip(near_ops, tails)]

    near_a = []
    for sc_chunk in near_sc:
        tiles = []
        for sc in sc_chunk:
            a = jnp.where(near == nsmall, sc[0], 0.0)
            for l in range(nsmall):
                a = jnp.where(near == l, sc[l + 1], a)
            tiles.append(a.astype(BF16))
        near_a.append(tiles)
    far_a = [[(jnp.where(far < nsmall + j, a, 0.0) if (1 << (nsmall + j)) < c // 2 else a).astype(BF16)
              for j, a in enumerate(lev)] for lev in far_sc]
    states = [st]
    for (_, _, dec), upd in zip(tails, upds):
        states.append(states[-1] * dec + upd)

    near_o = [[dot_nn(a, vb[i * g:(i + 1) * g]) for i, a in enumerate(tiles)]
              for tiles, (_, _, vb) in zip(near_a, near_ops)]
    far_o = [[dot_nn(a, ops[2]) for a, ops in zip(avs, lev)] for avs, lev in zip(far_a, far_ops)]
    inter_o = [dot_nt(qe, s.astype(BF16)) for (qe, _, _), s in zip(tails, states[:-1])]

    outs = []
    for no, fo, io in zip(near_o, far_o, inter_o):
        o = (no[0] if len(no) == 1 else jnp.concatenate(no, axis=0)) + io
        for j, x in enumerate(fo):
            m = 1 << (nsmall + j)
            o_ev, o_od = _halves(o, m)
            o = _interleave(o_ev, o_od + x, m)
        outs.append(o)
    return outs, states[-1]


def _hgrn_prompt_kernel(a_ref, wq_ref, wf_ref, wi_ref, wg_ref, p_ref, ng_ref, near_ref, far_ref,
                        o_ref, s_ref, wbf, st, *, chunk):
    i = pl.program_id(1)
    dk = wq_ref.shape[1]

    @pl.when(i == 0)
    def _():
        wbf[:, 0 * dk:1 * dk] = wq_ref[...].astype(BF16)
        wbf[:, 1 * dk:2 * dk] = wf_ref[...].astype(BF16)
        wbf[:, 2 * dk:3 * dk] = wi_ref[...].astype(BF16)
        wbf[:, 3 * dk:4 * dk] = wg_ref[...].astype(BF16)
        st[...] = jnp.zeros_like(st)

    tm = a_ref.shape[0]
    projs = [jnp.dot(a_ref[c0:c0 + chunk, :], wbf[...], preferred_element_type=F32)
             for c0 in range(0, tm, chunk)]
    lb = _lower_bound(p_ref)
    ng = ng_ref[...]
    chunks = []
    for proj in projs:
        q = _silu(proj[:, 0 * dk:1 * dk])
        logf, k = _hgrn_gates(proj[:, 1 * dk:2 * dk], lb)
        chunks.append((q, k, proj[:, 2 * dk:3 * dk], logf))
    outs, s_t = _hgrn_chunks(chunks, st[...], near_ref[...], far_ref[...])
    for j, (o, proj) in enumerate(zip(outs, projs)):
        o_ref[j * chunk:(j + 1) * chunk, :] = _group_norm_gate(o, proj[:, 3 * dk:4 * dk], ng).astype(o_ref.dtype)
    st[...] = s_t
    s_ref[0, 0] = s_t.T


def _hgrn_prompt(a, w_in, lower_bounds, norm_g, n_heads, dk):
    m, d = a.shape
    chunk = min(HG_CHUNK, m)
    tm = _pick(m, (HG_ROWS, 512, 256, 128, 64, 32, 16, 8))
    tm = max(tm, chunk)
    near, far = _level_tables(chunk)
    w_spec = lambda blk: pl.BlockSpec((d, dk), lambda h, i: (0, blk * n_heads + h))
    o, s = pl.pallas_call(
        functools.partial(_hgrn_prompt_kernel, chunk=chunk),
        out_shape=(jax.ShapeDtypeStruct((m, n_heads * dk), BF16),
                   jax.ShapeDtypeStruct((1, n_heads, dk, dk), F32)),
        grid=(n_heads, m // tm),
        in_specs=[pl.BlockSpec((tm, d), lambda h, i: (i, 0)),
                  w_spec(0), w_spec(1), w_spec(2), w_spec(3),
                  pl.BlockSpec((lower_bounds.shape[0], dk), lambda h, i: (0, h)),
                  pl.BlockSpec((1, dk), lambda h, i: (0, h)),
                  pl.BlockSpec(near.shape, lambda h, i: (0, 0)),
                  pl.BlockSpec(far.shape, lambda h, i: (0, 0))],
        out_specs=(pl.BlockSpec((tm, dk), lambda h, i: (i, h)),
                   pl.BlockSpec((1, 1, dk, dk), lambda h, i: (0, h, 0, 0))),
        scratch_shapes=[pltpu.VMEM((d, 4 * dk), BF16), pltpu.VMEM((dk, dk), F32)],
        compiler_params=_cparams(("arbitrary", "arbitrary"), V7X_VMEM_LIMIT),
        name="hgrn_prompt",
    )(a, w_in, w_in, w_in, w_in, lower_bounds, norm_g.reshape(1, -1), near, far)
    return o, s


def _hgrn_sample_kernel(qr_ref, fr_ref, ir_ref, gr_ref, p_ref, ng_ref, s0_ref, o_ref, s_ref, *, seq):
    bs = s0_ref.shape[0]
    rows = bs * seq
    lb = _lower_bound(p_ref)
    q = _silu(qr_ref[...])
    logf, k = _hgrn_gates(fr_ref[...], lb)
    v = ir_ref[...]
    row = lax.broadcasted_iota(jnp.int32, q.shape, 0)
    t = row % seq
    b = logf
    d = 1
    while d < seq:
        b = b + jnp.where(t >= d, pltpu.roll(b, d, 0), 0.0)
        d *= 2
    o = jnp.sum(q * k, axis=-1, keepdims=True) * v
    for d in range(1, seq):
        ok = t >= d
        arg = jnp.where(ok, b - pltpu.roll(b, d, 0), 0.0)
        w = jnp.sum(q * pltpu.roll(k, d, 0) * jnp.exp(arg), axis=-1, keepdims=True)
        o = o + jnp.where(ok, w, 0.0) * pltpu.roll(v, d, 0)
    qe = (q * jnp.exp(b)).astype(BF16)
    b_end = b
    d = 1
    while d < seq:
        b_end = jnp.where(t < seq - d, pltpu.roll(b_end, rows - d, 0), b_end)
        d *= 2
    k_end = k * jnp.exp(b_end - b)
    k_end_t = k_end.T
    dec_t = jnp.exp(b_end).T
    lane = lax.broadcasted_iota(jnp.int32, k_end_t.shape, 1)
    vb = v.astype(BF16)
    o_inter = []
    for s in range(bs):
        s0 = s0_ref[s, 0]
        in_seq = (lane >= s * seq) & (lane < (s + 1) * seq)
        upd = jnp.dot(jnp.where(in_seq, k_end_t, 0.0).astype(BF16), vb, preferred_element_type=F32)
        s_ref[s, 0] = s0 * dec_t[:, s * seq:s * seq + 1] + upd
        if s % (8 // seq) == 0:
            r0 = s * seq
            parts = []
        oi = jnp.dot(qe[r0:r0 + 8, :], s0.astype(BF16), preferred_element_type=F32)
        parts.append(oi)
        if s % (8 // seq) == (8 // seq) - 1:
            r8 = lax.broadcasted_iota(jnp.int32, oi.shape, 0) // seq
            acc = parts[0]
            for j in range(1, len(parts)):
                acc = jnp.where(r8 == j, parts[j], acc)
            o_inter.append(acc)
    o = o + jnp.concatenate(o_inter, axis=0)
    o_ref[...] = _group_norm_gate(o, gr_ref[...], ng_ref[...]).astype(o_ref.dtype)


def _hgrn_sample(proj, lower_bounds, norm_g, state, seq):
    b, n_heads, dk, dv = state.shape
    m = proj.shape[0]
    bs = _pick(b, (64, 32, 16, 8, 4, 2))
    rows = bs * seq
    col = lambda blk: pl.BlockSpec((rows, dk), lambda h, i: (i, blk * n_heads + h))
    o, s = pl.pallas_call(
        functools.partial(_hgrn_sample_kernel, seq=seq),
        out_shape=(jax.ShapeDtypeStruct((m, n_heads * dv), BF16),
                   jax.ShapeDtypeStruct(state.shape, F32)),
        grid=(n_heads, b // bs),
        in_specs=[col(0), col(1), col(2), col(3),
                  pl.BlockSpec((lower_bounds.shape[0], dk), lambda h, i: (0, h)),
                  pl.BlockSpec((1, dv), lambda h, i: (0, h)),
                  pl.BlockSpec((bs, 1, dk, dv), lambda h, i: (i, h, 0, 0))],
        out_specs=(pl.BlockSpec((rows, dv), lambda h, i: (i, h)),
                   pl.BlockSpec((bs, 1, dk, dv), lambda h, i: (i, h, 0, 0))),
        compiler_params=_cparams(("arbitrary", "arbitrary"), V7X_VMEM_LIMIT),
        name="hgrn_sample",
    )(proj, proj, proj, proj, lower_bounds, norm_g.reshape(1, -1), state)
    return o, s


def kernel(x_prompt, x_sample, cache_swa_k, cache_swa_v, state_hgrn, state_ffn_conv, c_prompt, c_sample,
           norm1_g, norm2_g, w_ada, b_ada, attn_w_qkv, attn_w_o, attn_sinks,
           hgrn_w_in, hgrn_lower_bounds, hgrn_norm_g, hgrn_w_o,
           ffn_w_in, ffn_conv_w, ffn_conv_b, ffn_w_out, final_norm_g):
    bp, lp, d = x_prompt.shape
    bsmp, ls, _ = x_sample.shape
    assert bp == 1, "the prompt kernels carry one sequence"
    depth = w_ada.shape[0]
    n_kv, hd = cache_swa_k.shape[2], cache_swa_k.shape[3]
    n_heads = attn_sinks.shape[0]
    dq = n_heads * hd
    dkv = n_kv * hd
    hg_heads, hg_dk = state_hgrn.shape[1], state_hgrn.shape[2]
    keep = min(WINDOW, lp)

    ns = bsmp * ls
    assert ns % 8 == 0
    assert bsmp % 8 == 0
    c_seq = jnp.concatenate([c_sample, c_prompt, jnp.zeros((7, d), F32)], axis=0)
    mod = _ada(c_seq, w_ada, b_ada, ls)
    pk = dict(sample=False, n_sample=ns)
    sk = dict(sample=True, n_sample=ns)

    xp = x_prompt.reshape(lp, d)
    xs = x_sample.reshape(ns, d)
    conv_p, conv_s = [], []
    outs = {}
    w_out_bf = ffn_w_out.astype(BF16)
    hp = hs = None
    for l in range(depth):
        if l % 2 == 0:
            if hp is None:
                qkv_p = _matmul_mod(xp, norm1_g, mod, l, 0, 1, attn_w_qkv, False, ns, tm=1024, tn=1024,
                                    name="qkv_prompt")
                qkv_s = _matmul_mod(xs, norm1_g, mod, l, 0, 1, attn_w_qkv, True, ns, tm=512, tn=512,
                                    name="qkv_sample")
            else:
                qkv_p = _matmul(hp, attn_w_qkv, tn=1024, name="qkv_prompt")
                qkv_s = _matmul(hs, attn_w_qkv, name="qkv_sample")
            op = _swa_prompt(qkv_p, attn_sinks, n_kv, hd)
            outs['swa_k_prompt'] = qkv_p[lp - keep:, dq:dq + dkv].reshape(bp, keep, n_kv, hd)
            outs['swa_v_prompt'] = qkv_p[lp - keep:, dq + dkv:].reshape(bp, keep, n_kv, hd)
            outs['swa_k_sample'] = qkv_s[:, dq:dq + dkv].reshape(bsmp, ls, n_kv, hd)
            outs['swa_v_sample'] = qkv_s[:, dq + dkv:].reshape(bsmp, ls, n_kv, hd)
            os_ = _swa_sample(qkv_s[:, :dq], qkv_s[:, dq:dq + dkv], qkv_s[:, dq + dkv:],
                              cache_swa_k, cache_swa_v, attn_sinks, n_kv, hd, ls)
            w_o = attn_w_o
        else:
            if hp is None:
                hp = _modulate(xp, norm1_g, mod, l, 0, 1, False, ns)
                hs = _modulate(xs, norm1_g, mod, l, 0, 1, True, ns)
            op, sp = _hgrn_prompt(hp, hgrn_w_in, hgrn_lower_bounds, hgrn_norm_g, hg_heads, hg_dk)
            proj_s = _matmul(hs, hgrn_w_in, name="hgrn_in_sample")
            os_, ss = _hgrn_sample(proj_s, hgrn_lower_bounds, hgrn_norm_g, state_hgrn, ls)
            outs['hgrn_state_prompt'] = sp
            outs['hgrn_state_sample'] = ss
            w_o = hgrn_w_o
        xp, hp = _mm_rows(op, w_o, xp, mod, gate=(l, 2), norm_g=norm2_g, norm_layer=l, next_mod=(l, 3, 4),
                          tm=512, name="mix_out_prompt", **pk)
        xs, hs = _mm_rows(os_, w_o, xs, mod, gate=(l, 2), norm_g=norm2_g, norm_layer=l, next_mod=(l, 3, 4),
                          tm=512, name="mix_out_sample", **sk)
        actp, bufp = _ffn_in_prompt(hp, ffn_w_in, ffn_conv_w, ffn_conv_b, l)
        acts, bufs = _ffn_in_sample(hs, ffn_w_in, ffn_conv_w, ffn_conv_b, l, state_ffn_conv, ls)
        conv_p.append(bufp[None])
        conv_s.append(bufs)
        if l + 1 < depth:
            nxt = dict(norm_g=norm1_g, norm_layer=l + 1, next_mod=(l + 1, 0, 1))
        else:
            nxt = dict(norm_g=final_norm_g, norm_layer=0, next_mod=None, keep_x=False, out_dtype=F32)
        xp, hp = _mm_rows(actp, w_out_bf, xp, mod, gate=(l, 5), w_layer=l, tm=256, name="ffn_out_prompt",
                          **nxt, **pk)
        xs, hs = _mm_rows(acts, w_out_bf, xs, mod, gate=(l, 5), w_layer=l, tm=256, name="ffn_out_sample",
                          **nxt, **sk)
    y_prompt = hp.reshape(bp, lp, d)
    y_sample = hs.reshape(bsmp, ls, d)
    return (y_prompt, y_sample,
            outs['swa_k_prompt'], outs['swa_v_prompt'], outs['swa_k_sample'], outs['swa_v_sample'],
            outs['hgrn_state_prompt'], outs['hgrn_state_sample'],
            jnp.stack(conv_p), jnp.stack(conv_s))
```

```python
import functools
import math

import numpy as np
import jax
import jax.numpy as jnp
from jax import lax
from jax.experimental import pallas as pl
from jax.experimental.pallas import tpu as pltpu

F32 = jnp.float32
BF16 = jnp.bfloat16

EPS = 1e-6
LOG2E = math.log2(math.e)
WINDOW = 128
CONV_W = 3
LANES = 128
HG_CHUNK = 256
HG_ROWS = 2048
SWA_WAVE = 8
V7X_VMEM_LIMIT = 56 * 1024 * 1024


def _cparams(sem, vmem=None):
    return pltpu.CompilerParams(dimension_semantics=sem, vmem_limit_bytes=vmem)


def _pick(n, cands):
    for c in cands:
        if n % c == 0:
            return c
    return n


def _silu(x):
    return x * jax.nn.sigmoid(x)


def _logaddexp(a, b):
    return jnp.maximum(a, b) + jnp.log1p(jnp.exp(-jnp.abs(a - b)))


def _ada_kernel(c_ref, w_ref, b_ref, o_ref, act, slab, *, n_seq, seq):
    @pl.when((pl.program_id(0) == 0) & (pl.program_id(1) == 0))
    def _():
        act[...] = _silu(c_ref[...]).astype(BF16)

    w = w_ref[0].astype(BF16)
    res = jnp.dot(act[...], w, preferred_element_type=F32) + b_ref[0]
    for c in range(slab.shape[0]):
        cs = slice(c * LANES, (c + 1) * LANES)
        for t in range(seq):
            slab[c, pl.ds(t, n_seq, stride=seq), :] = res[:n_seq, cs]
        slab[c, n_seq * seq:, :] = res[n_seq:, cs]
        o_ref[0, :, cs] = slab[c]


def _ada(c_seq, w_ada, b_ada, seq):
    depth, d, n = w_ada.shape
    r = c_seq.shape[0]
    n_seq = r - 8
    rows = n_seq * seq + 8
    tn = _pick(n, (2048, 1024, 512, 256, 128))
    return pl.pallas_call(
        functools.partial(_ada_kernel, n_seq=n_seq, seq=seq),
        out_shape=jax.ShapeDtypeStruct((depth, rows, n), F32),
        grid=(depth, n // tn),
        in_specs=[pl.BlockSpec((r, d), lambda l, j: (0, 0)),
                  pl.BlockSpec((1, d, tn), lambda l, j: (l, 0, j)),
                  pl.BlockSpec((1, 1, tn), lambda l, j: (l, 0, j))],
        out_specs=pl.BlockSpec((1, rows, tn), lambda l, j: (l, 0, j)),
        scratch_shapes=[pltpu.VMEM((r, d), BF16), pltpu.VMEM((tn // LANES, rows, LANES), F32)],
        compiler_params=_cparams(("arbitrary", "arbitrary"), V7X_VMEM_LIMIT),
        name="ada",
    )(c_seq, w_ada, b_ada.reshape(depth, 1, n))


def _mod_rows(sample, tm, n_sample):
    if sample:
        return tm, (lambda i: i)
    return 8, (lambda i: n_sample // 8)


def _modulate_kernel(x_ref, g_ref, sc_ref, sh_ref, o_ref, *, sample):
    x = x_ref[...]
    ms = jnp.mean(x * x, axis=-1, keepdims=True)
    y = x * lax.rsqrt(ms + EPS) * g_ref[...]
    sc = sc_ref[...] if sample else sc_ref[0:1, :]
    sh = sh_ref[...] if sample else sh_ref[0:1, :]
    o_ref[...] = (y * (1.0 + sc) + sh).astype(o_ref.dtype)


def _modulate(x, g_all, mod, l, shift_col, scale_col, sample, n_sample):
    m, d = x.shape
    tm = _pick(m, (512, 256, 128, 64, 32, 16, 8))
    rows, rmap = _mod_rows(sample, tm, n_sample)
    mspec = lambda col: pl.BlockSpec((None, rows, d), lambda i: (l, rmap(i), col))
    return pl.pallas_call(
        functools.partial(_modulate_kernel, sample=sample),
        out_shape=jax.ShapeDtypeStruct((m, d), BF16),
        grid=(m // tm,),
        in_specs=[pl.BlockSpec((tm, d), lambda i: (i, 0)),
                  pl.BlockSpec((None, 1, d), lambda i: (l, 0, 0)),
                  mspec(scale_col), mspec(shift_col)],
        out_specs=pl.BlockSpec((tm, d), lambda i: (i, 0)),
        compiler_params=_cparams(("arbitrary",), V7X_VMEM_LIMIT),
        name="modulate",
    )(x, g_all.reshape(g_all.shape[0], 1, d), mod, mod)


def _mm_kernel(a_ref, w_ref, o_ref, wbf):
    @pl.when(pl.program_id(1) == 0)
    def _():
        wbf[...] = w_ref[...].astype(BF16)

    o_ref[...] = jnp.dot(a_ref[...], wbf[...], preferred_element_type=F32)


def _mm_mod_kernel(x_ref, g_ref, sc_ref, sh_ref, w_ref, o_ref, wbf, *, sample):
    @pl.when(pl.program_id(1) == 0)
    def _():
        wbf[...] = w_ref[...].astype(BF16)

    rows = (lambda r: r[...]) if sample else (lambda r: r[0:1, :])
    x = x_ref[...]
    y = x * lax.rsqrt(jnp.mean(x * x, axis=-1, keepdims=True) + EPS) * g_ref[...]
    h = (y * (1.0 + rows(sc_ref)) + rows(sh_ref)).astype(BF16)
    o_ref[...] = jnp.dot(h, wbf[...], preferred_element_type=F32)


def _matmul_mod(x, g_all, mod, l, shift_col, scale_col, w, sample, n_sample, *, tm, tn, name):
    m, d = x.shape
    n = w.shape[-1]
    tm = min(tm, m)
    tn = tn if n % tn == 0 else _pick(n, (512, 256, 128))
    assert m % tm == 0 and n % tn == 0
    rows, rmap = _mod_rows(sample, tm, n_sample)
    mspec = lambda col: pl.BlockSpec((None, rows, d), lambda j, i: (l, rmap(i), col))
    return pl.pallas_call(
        functools.partial(_mm_mod_kernel, sample=sample),
        out_shape=jax.ShapeDtypeStruct((m, n), F32),
        grid=(n // tn, m // tm),
        in_specs=[pl.BlockSpec((tm, d), lambda j, i: (i, 0)),
                  pl.BlockSpec((None, 1, d), lambda j, i: (l, 0, 0)),
                  mspec(scale_col), mspec(shift_col),
                  pl.BlockSpec((d, tn), lambda j, i: (0, j))],
        out_specs=pl.BlockSpec((tm, tn), lambda j, i: (i, j)),
        scratch_shapes=[pltpu.VMEM((d, tn), BF16)],
        compiler_params=_cparams(("arbitrary", "arbitrary"), V7X_VMEM_LIMIT),
        name=name,
    )(x, g_all.reshape(g_all.shape[0], 1, d), mod, mod, w)


def _matmul(a, w, *, tm=None, tn=None, name="matmul"):
    m, k = a.shape
    n = w.shape[-1]
    tm = min(tm, m) if tm else _pick(m, (1024, 512, 256, 128, 64, 32, 16, 8))
    tn = min(tn, n) if tn else _pick(n, (512, 256, 128))
    assert m % tm == 0 and n % tn == 0
    return pl.pallas_call(
        _mm_kernel,
        out_shape=jax.ShapeDtypeStruct((m, n), F32),
        grid=(n // tn, m // tm),
        in_specs=[pl.BlockSpec((tm, k), lambda j, i: (i, 0)),
                  pl.BlockSpec((k, tn), lambda j, i: (0, j))],
        out_specs=pl.BlockSpec((tm, tn), lambda j, i: (i, j)),
        scratch_shapes=[pltpu.VMEM((k, tn), BF16)],
        compiler_params=_cparams(("arbitrary", "arbitrary"), V7X_VMEM_LIMIT),
        name=name,
    )(a, w)


def _mm_rows_kernel(*refs, cast_w, sample, modulate, keep_x):
    it = iter(refs)
    a_ref, w_ref, x_ref, gate_ref, g_ref = (next(it) for _ in range(5))
    sc_ref, sh_ref = (next(it), next(it)) if modulate else (None, None)
    xo_ref = next(it) if keep_x else None
    h_ref = next(it)
    if cast_w:
        wbf = next(it)

        @pl.when(pl.program_id(0) == 0)
        def _():
            wbf[...] = w_ref[...].astype(BF16)

        w = wbf[...]
    else:
        w = w_ref[...]
    rows = (lambda r: r[...]) if sample else (lambda r: r[0:1, :])
    x = x_ref[...] + rows(gate_ref) * jnp.dot(a_ref[...], w, preferred_element_type=F32)
    if keep_x:
        xo_ref[...] = x
    y = x * lax.rsqrt(jnp.mean(x * x, axis=-1, keepdims=True) + EPS) * g_ref[...]
    if modulate:
        y = y * (1.0 + rows(sc_ref)) + rows(sh_ref)
    h_ref[...] = y.astype(h_ref.dtype)


def _mm_rows(a, w, x, mod, *, gate, norm_g, norm_layer, next_mod=None, w_layer=None, sample, n_sample,
             tm, keep_x=True, out_dtype=None, name):
    m, k = a.shape
    n = w.shape[-1]
    tm = min(tm, m)
    assert m % tm == 0
    cast_w = w.dtype != BF16
    rows, rmap = _mod_rows(sample, tm, n_sample)
    once = pl.Buffered(1)
    if w_layer is None:
        w_spec = pl.BlockSpec((k, n), lambda i: (0, 0), pipeline_mode=once)
    else:
        w_spec = pl.BlockSpec((None, k, n), lambda i: (w_layer, 0, 0), pipeline_mode=once)
    mspec = lambda layer, col: pl.BlockSpec((None, rows, n), lambda i: (layer, rmap(i), col))
    in_specs = [pl.BlockSpec((tm, k), lambda i: (i, 0)), w_spec,
                pl.BlockSpec((tm, n), lambda i: (i, 0)),
                mspec(*gate),
                pl.BlockSpec((None, 1, n), lambda i: (norm_layer, 0, 0))]
    args = [a, w, x, mod, norm_g.reshape(-1, 1, n)]
    if next_mod is not None:
        in_specs += [mspec(next_mod[0], next_mod[2]), mspec(next_mod[0], next_mod[1])]
        args += [mod, mod]
    out_shape, out_specs = [], []
    if keep_x:
        out_shape.append(jax.ShapeDtypeStruct((m, n), F32))
        out_specs.append(pl.BlockSpec((tm, n), lambda i: (i, 0)))
    out_shape.append(jax.ShapeDtypeStruct((m, n), out_dtype or BF16))
    out_specs.append(pl.BlockSpec((tm, n), lambda i: (i, 0)))
    res = pl.pallas_call(
        functools.partial(_mm_rows_kernel, cast_w=cast_w, sample=sample, modulate=next_mod is not None,
                          keep_x=keep_x),
        out_shape=tuple(out_shape),
        grid=(m // tm,),
        in_specs=in_specs,
        out_specs=tuple(out_specs),
        scratch_shapes=[pltpu.VMEM((k, n), BF16)] if cast_w else [],
        compiler_params=_cparams(("arbitrary",), V7X_VMEM_LIMIT),
        name=name,
    )(*args)
    return res if keep_x else (None, res[0])


def _gelu_exact(a):
    return 0.5 * a * (1.0 + lax.erf(a * (1.0 / math.sqrt(2.0))))


def _ffn_in_prompt_kernel(a_ref, wu_ref, wg_ref, cw_ref, cb_ref, act_ref, tail_ref, wbf, carry):
    i = pl.program_id(1)
    tf = wu_ref.shape[1]

    @pl.when(i == 0)
    def _():
        wbf[:, :tf] = wu_ref[...].astype(BF16)
        wbf[:, tf:] = wg_ref[...].astype(BF16)
        carry[...] = jnp.zeros_like(carry)

    ug = jnp.dot(a_ref[...], wbf[...], preferred_element_type=F32)
    u = ug[:, :tf]
    g = ug[:, tf:]
    tm = u.shape[0]
    row = lax.broadcasted_iota(jnp.int32, u.shape, 0)
    c = carry[...]
    u1 = jnp.where(row == 0, c[7:8, :], pltpu.roll(u, 1, 0))
    u2 = jnp.where(row == 0, c[6:7, :], jnp.where(row == 1, c[7:8, :], pltpu.roll(u, 2, 0)))
    cw = cw_ref[...]
    conv = cw[0:1, :] * u2 + cw[1:2, :] * u1 + cw[2:3, :] * u + cb_ref[...]
    act_ref[...] = (_gelu_exact(conv) * g).astype(act_ref.dtype)
    carry[...] = u[tm - 8:, :]
    tail_ref[...] = u[tm - 8:, :]


def _ffn_in_sample_kernel(a_ref, wu_ref, wg_ref, cw_ref, cb_ref, buf_ref, act_ref, nbuf_ref,
                          us, gs, acts, *, seq):
    a = a_ref[...]
    b = buf_ref.shape[0]
    u_all = jnp.dot(a, wu_ref[...].astype(BF16), preferred_element_type=F32)
    g_all = jnp.dot(a, wg_ref[...].astype(BF16), preferred_element_type=F32)
    cw = cw_ref[...]
    cb = cb_ref[...]
    for c in range(us.shape[0]):
        cs = slice(c * LANES, (c + 1) * LANES)
        us[c] = u_all[:, cs]
        gs[c] = g_all[:, cs]
        prev2, prev1 = buf_ref[:, 0, cs], buf_ref[:, 1, cs]
        for t in range(seq):
            u = us[c, pl.ds(t, b, stride=seq), :]
            conv = cw[0:1, cs] * prev2 + cw[1:2, cs] * prev1 + cw[2:3, cs] * u + cb[:, cs]
            acts[c, pl.ds(t, b, stride=seq), :] = _gelu_exact(conv) * gs[c, pl.ds(t, b, stride=seq), :]
            prev2, prev1 = prev1, u
        act_ref[:, cs] = acts[c].astype(act_ref.dtype)
        nbuf_ref[:, 0, cs] = prev2
        nbuf_ref[:, 1, cs] = prev1


def _ffn_in_prompt(a, w_in, conv_w, conv_b, l):
    m, d = a.shape
    f = w_in.shape[-1] // 2
    tf = _pick(f, (512, 256, 128))
    tm = _pick(m, (1024, 512, 256, 128, 64, 32, 16, 8))
    nf = f // tf
    act, tail = pl.pallas_call(
        _ffn_in_prompt_kernel,
        out_shape=(jax.ShapeDtypeStruct((m, f), BF16), jax.ShapeDtypeStruct((8, f), F32)),
        grid=(nf, m // tm),
        in_specs=[pl.BlockSpec((tm, d), lambda j, i: (i, 0)),
                  pl.BlockSpec((None, d, tf), lambda j, i: (l, 0, j)),
                  pl.BlockSpec((None, d, tf), lambda j, i: (l, 0, nf + j)),
                  pl.BlockSpec((None, CONV_W, tf), lambda j, i: (l, 0, j)),
                  pl.BlockSpec((None, 1, tf), lambda j, i: (l, 0, j))],
        out_specs=(pl.BlockSpec((tm, tf), lambda j, i: (i, j)),
                   pl.BlockSpec((8, tf), lambda j, i: (0, j))),
        scratch_shapes=[pltpu.VMEM((d, 2 * tf), BF16), pltpu.VMEM((8, tf), F32)],
        compiler_params=_cparams(("arbitrary", "arbitrary"), V7X_VMEM_LIMIT),
        name="ffn_in_prompt",
    )(a, w_in, w_in, conv_w, conv_b.reshape(conv_b.shape[0], 1, f))
    return act, tail[6:8]


def _ffn_in_sample(a, w_in, conv_w, conv_b, l, state, seq):
    m, d = a.shape
    f = w_in.shape[-1] // 2
    b = state.shape[1]
    tf = _pick(f, (512, 256, 128))
    nf = f // tf
    return pl.pallas_call(
        functools.partial(_ffn_in_sample_kernel, seq=seq),
        out_shape=(jax.ShapeDtypeStruct((m, f), BF16), jax.ShapeDtypeStruct((b, CONV_W - 1, f), F32)),
        grid=(nf,),
        in_specs=[pl.BlockSpec((m, d), lambda j: (0, 0)),
                  pl.BlockSpec((None, d, tf), lambda j: (l, 0, j)),
                  pl.BlockSpec((None, d, tf), lambda j: (l, 0, nf + j)),
                  pl.BlockSpec((None, CONV_W, tf), lambda j: (l, 0, j)),
                  pl.BlockSpec((None, 1, tf), lambda j: (l, 0, j)),
                  pl.BlockSpec((None, b, CONV_W - 1, tf), lambda j: (l, 0, 0, j))],
        out_specs=(pl.BlockSpec((m, tf), lambda j: (0, j)),
                   pl.BlockSpec((b, CONV_W - 1, tf), lambda j: (0, 0, j))),
        scratch_shapes=[pltpu.VMEM((tf // LANES, m, LANES), F32)] * 3,
        compiler_params=_cparams(("arbitrary",), V7X_VMEM_LIMIT),
        name="ffn_in_sample",
    )(a, w_in, w_in, conv_w, conv_b.reshape(conv_b.shape[0], 1, f), state)


def _alibi_slope(h, n_heads):
    return 2.0 ** (-8.0 * (h + 1) / n_heads)


def _swa_prompt_bias(n_heads, tq):
    t = np.arange(tq)[:, None]
    s = np.arange(2 * tq)[None, :]
    dist = t + tq - s
    ok = (dist >= 0) & (dist < WINDOW)
    slope = (2.0 ** (-8.0 * (np.arange(n_heads) + 1) / n_heads) * LOG2E)[:, None, None]
    table = np.stack([np.where(ok & (s >= tq), slope * s, -np.inf), np.where(ok, slope * s, -np.inf)])
    return jnp.asarray(table, F32)


def _swa_prompt_kernel(sink_ref, q_ref, kp_ref, kc_ref, vp_ref, vc_ref, bias_ref, o_ref, *, n_kv, group, hd):
    n_heads = n_kv * group
    tq = q_ref.shape[0]
    trow = (lax.broadcasted_iota(jnp.int32, (tq, 1), 0) + tq).astype(F32)
    qscale = hd ** -0.5 * LOG2E
    nt = (((1,), (1,)), ((), ()))
    def scores(kv):
        ks = slice(kv * hd, (kv + 1) * hd)
        k2 = jnp.concatenate([kp_ref[:, ks], kc_ref[:, ks]], axis=0).astype(BF16)
        q4 = jnp.concatenate([(q_ref[:, h * hd:(h + 1) * hd] * qscale).astype(BF16)
                              for h in range(kv * group, (kv + 1) * group)], axis=0)
        return lax.dot_general(q4, k2, nt, preferred_element_type=F32)

    for kv0 in range(0, n_kv, SWA_WAVE):
        kvs = list(range(kv0, min(kv0 + SWA_WAVE, n_kv)))
        sc4s = [scores(kv) for kv in kvs]
        heads = [kv * group + g for kv in kvs for g in range(group)]
        slopes = [_alibi_slope(h, n_heads) * LOG2E for h in heads]
        scs = [sc4s[j // group][(j % group) * tq:(j % group + 1) * tq] + bias_ref[h]
               for j, h in enumerate(heads)]
        sinks = [sink_ref[h] * LOG2E + slopes[j] * trow for j, h in enumerate(heads)]
        mxs = [jnp.maximum(jnp.max(scs[j], axis=-1, keepdims=True), sinks[j]) for j in range(len(heads))]
        es = [jnp.exp2(scs[j] - mxs[j]) for j in range(len(heads))]
        dens = [jnp.sum(es[j], axis=-1, keepdims=True) + jnp.exp2(sinks[j] - mxs[j])
                for j in range(len(heads))]
        es = [e.astype(BF16) for e in es]
        o4s = []
        for i, kv in enumerate(kvs):
            ks = slice(kv * hd, (kv + 1) * hd)
            v2 = jnp.concatenate([vp_ref[:, ks], vc_ref[:, ks]], axis=0).astype(BF16)
            o4s.append(jnp.dot(jnp.concatenate(es[i * group:(i + 1) * group], axis=0), v2,
                               preferred_element_type=F32))
        for j, h in enumerate(heads):
            o = o4s[j // group][(j % group) * tq:(j % group + 1) * tq]
            o_ref[:, h * hd:(h + 1) * hd] = (o / dens[j]).astype(o_ref.dtype)


def _swa_prompt(qkv, sinks, n_kv, hd):
    l = qkv.shape[0]
    n_heads = sinks.shape[0]
    group = n_heads // n_kv
    dq = n_heads * hd
    dk = n_kv * hd
    nb = l // WINDOW
    kblk = dq // dk
    bias = _swa_prompt_bias(n_heads, WINDOW)
    return pl.pallas_call(
        functools.partial(_swa_prompt_kernel, n_kv=n_kv, group=group, hd=hd),
        out_shape=jax.ShapeDtypeStruct((l, dq), BF16),
        grid_spec=pltpu.PrefetchScalarGridSpec(
            num_scalar_prefetch=1,
            grid=(nb,),
            in_specs=[pl.BlockSpec((WINDOW, dq), lambda i, s: (i, 0)),
                      pl.BlockSpec((WINDOW, dk), lambda i, s: (jnp.maximum(i - 1, 0), kblk)),
                      pl.BlockSpec((WINDOW, dk), lambda i, s: (i, kblk)),
                      pl.BlockSpec((WINDOW, dk), lambda i, s: (jnp.maximum(i - 1, 0), kblk + 1)),
                      pl.BlockSpec((WINDOW, dk), lambda i, s: (i, kblk + 1)),
                      pl.BlockSpec((None,) + bias.shape[1:], lambda i, s: (jnp.minimum(i, 1), 0, 0, 0))],
            out_specs=pl.BlockSpec((WINDOW, dq), lambda i, s: (i, 0)),
        ),
        compiler_params=_cparams(("arbitrary",), V7X_VMEM_LIMIT),
        name="swa_prompt",
    )(sinks, qkv, qkv, qkv, qkv, qkv, bias)


def _swa_sample_tables(n_heads, n_kv, hd, seq, cr):
    group = n_heads // n_kv
    r = np.arange(seq * n_heads)
    t, h = r // n_heads, r % n_heads
    slope = 2.0 ** (-8.0 * (h + 1) / n_heads)
    slot = (np.arange(n_kv * hd)[None, :] // hd == (h // group)[:, None]).astype(np.float32)
    dist = cr + t[:, None] - np.arange(cr)[None, :]
    bias_c = np.where((dist >= 0) & (dist < WINDOW), -slope[:, None] * dist, -np.inf) * LOG2E
    j = np.arange(LANES)
    distn = t[:, None] - j[None, :]
    bias_n = np.where((distn >= 0) & (j[None, :] < seq), -slope[:, None] * distn, -np.inf) * LOG2E
    return jnp.asarray(slot, F32), jnp.asarray(np.concatenate([bias_c, bias_n], axis=1), F32)


def _swa_sample_kernel(q_ref, kt_ref, vt_ref, kn_ref, vn_ref, slot_ref, bias_ref, sink_ref, o_ref, *, hd):
    bs = q_ref.shape[0]
    cr = kt_ref.shape[2]
    width = slot_ref.shape[1]
    qscale = hd ** -0.5 * LOG2E
    slot = slot_ref[...]
    bias = bias_ref[...]
    sink2 = sink_ref[...] * LOG2E
    nt = (((1,), (1,)), ((), ()))

    def slotted(q):
        return jnp.concatenate([q] * (width // hd), axis=1) * slot

    qs = [slotted(q_ref[b] * qscale) for b in range(bs)]
    qb = [q.astype(BF16) for q in qs]
    zrows = jnp.zeros((LANES - kn_ref.shape[1], width), F32)
    kn = [jnp.concatenate([kn_ref[b], zrows], axis=0).astype(BF16) for b in range(bs)]
    vn = [jnp.concatenate([vn_ref[b], zrows], axis=0).astype(BF16) for b in range(bs)]
    sc = [jnp.concatenate([jnp.dot(qb[b], kt_ref[b].astype(BF16), preferred_element_type=F32),
                           lax.dot_general(qb[b], kn[b], nt, preferred_element_type=F32)], axis=1) + bias
          for b in range(bs)]
    mx = [jnp.maximum(jnp.max(s, axis=-1, keepdims=True), sink2) for s in sc]
    es = [jnp.exp2(s - m) for s, m in zip(sc, mx)]
    den = [jnp.sum(e, axis=-1, keepdims=True) + jnp.exp2(sink2 - m) for e, m in zip(es, mx)]
    es = [e.astype(BF16) for e in es]
    o_w = [lax.dot_general(es[b][:, :cr], vt_ref[b].astype(BF16), nt, preferred_element_type=F32)
           + jnp.dot(es[b][:, cr:], vn[b], preferred_element_type=F32) for b in range(bs)]
    for b in range(bs):
        o = o_w[b] * slot
        o = functools.reduce(jnp.add, [o[:, c * hd:(c + 1) * hd] for c in range(width // hd)])
        o_ref[b] = (o / den[b]).astype(o_ref.dtype)


def _swa_sample(q, k_new, v_new, cache_k, cache_v, sinks, n_kv, hd, seq):
    b, cr = cache_k.shape[0], cache_k.shape[1]
    n_heads = sinks.shape[0]
    rows = seq * n_heads
    width = n_kv * hd
    slot, bias = _swa_sample_tables(n_heads, n_kv, hd, seq, cr)
    kt = cache_k.transpose(0, 2, 3, 1).reshape(b, width, cr)
    vt = cache_v.transpose(0, 2, 3, 1).reshape(b, width, cr)
    new_rows = -(-seq // 8) * 8
    pad_new = lambda x: jnp.pad(x.reshape(b, seq, width), ((0, 0), (0, new_rows - seq), (0, 0)))
    bs = _pick(b, (32, 16, 8, 4, 2, 1))
    const = lambda shape: pl.BlockSpec(shape, lambda i: (0, 0))
    o = pl.pallas_call(
        functools.partial(_swa_sample_kernel, hd=hd),
        out_shape=jax.ShapeDtypeStruct((b, rows, hd), BF16),
        grid=(b // bs,),
        in_specs=[pl.BlockSpec((bs, rows, hd), lambda i: (i, 0, 0)),
                  pl.BlockSpec((bs, width, cr), lambda i: (i, 0, 0)),
                  pl.BlockSpec((bs, width, cr), lambda i: (i, 0, 0)),
                  pl.BlockSpec((bs, new_rows, width), lambda i: (i, 0, 0)),
                  pl.BlockSpec((bs, new_rows, width), lambda i: (i, 0, 0)),
                  const(slot.shape), const(bias.shape), const((rows, 1))],
        out_specs=pl.BlockSpec((bs, rows, hd), lambda i: (i, 0, 0)),
        compiler_params=_cparams(("arbitrary",), V7X_VMEM_LIMIT),
        name="swa_sample",
    )(q.reshape(b, rows, hd), kt, vt, pad_new(k_new), pad_new(v_new),
      slot, bias, jnp.tile(sinks, seq).reshape(rows, 1))
    return o.reshape(b * seq, n_heads * hd)


def _lower_bound(p_ref):
    p0 = p_ref[0:1, :]
    p1 = p_ref[1:2, :]
    mx = jnp.maximum(p0, p1)
    e0 = jnp.exp(p0 - mx)
    e1 = jnp.exp(p1 - mx)
    return e1 / (e0 + e1)


def _hgrn_gates(fr, lb):
    e = jnp.exp(-jnp.abs(fr))
    log_sig = jnp.minimum(fr, 0.0) - jnp.log1p(e)
    logf = _logaddexp(jnp.log(lb), jnp.log1p(-lb) + log_sig)
    r = 1.0 / (1.0 + e)
    k = (1.0 - lb) * jnp.where(fr >= 0.0, e * r, r)
    return logf, k


def _group_norm_gate(o, gr, ng):
    o = o * lax.rsqrt(jnp.mean(o * o, axis=-1, keepdims=True) + EPS) * ng
    return o * _silu(gr)


SMALL = 8

def _level_tables(c):
    g = min(c, 128)
    t = np.arange(g)[:, None]
    s = np.arange(g)[None, :]
    hb = np.floor(np.log2(np.maximum(t ^ s, 1))).astype(np.int32)
    same = (t // SMALL) == (s // SMALL)
    near = np.where(same & (s < t), hb, np.where(s == t, int(math.log2(SMALL)), -1))
    i = np.arange(c // 2)[:, None]
    j = np.arange(c // 2)[None, :]
    far = np.where(i == j, -1, np.floor(np.log2(np.maximum(i ^ j, 1)))).astype(np.int32)
    return jnp.asarray(near, dtype=jnp.int32), jnp.asarray(far, dtype=jnp.int32)


def _group_ref_rows(p, m, row):
    c, w = p.shape
    if m == 1:
        return jnp.where((row & 1) == 1, pltpu.roll(p, 1, 0), p)
    if m == 2:
        r4 = row & 3
        return jnp.where(r4 == 0, pltpu.roll(p, c - 1, 0),
                         jnp.where(r4 == 1, p,
                                   jnp.where(r4 == 2, pltpu.roll(p, 1, 0), pltpu.roll(p, 2, 0))))
    p3 = p.reshape(c // 8, 8, w)
    return jnp.broadcast_to(p3[:, 3:4, :], (c // 8, 8, w)).reshape(c, w)


def _halves(x, m):
    c = x.shape[0]
    ev = [x[r:r + m] for r in range(0, c, 2 * m)]
    od = [x[r + m:r + 2 * m] for r in range(0, c, 2 * m)]
    cat = lambda xs: xs[0] if len(xs) == 1 else jnp.concatenate(xs, axis=0)
    return cat(ev), cat(od)


def _interleave(ev, od, m):
    parts = []
    for r in range(0, ev.shape[0], m):
        parts += [ev[r:r + m], od[r:r + m]]
    return jnp.concatenate(parts, axis=0)


def _hgrn_chunks(chunks, st, near, far):
    c, w = chunks[0][0].shape
    nlev = int(math.log2(c))
    nsmall = int(math.log2(SMALL))
    g = near.shape[0]
    row = lax.broadcasted_iota(jnp.int32, (c, w), 0)
    nt = (((1,), (1,)), ((), ()))
    dot_nt = lambda x, y: lax.dot_general(x, y, nt, preferred_element_type=F32)
    dot_nn = lambda x, y: jnp.dot(x, y, preferred_element_type=F32)

    near_ops, far_ops, tails = [], [], []
    for q, k, v, logf in chunks:
        p = logf
        qs, ks = [q.astype(BF16)], [k.astype(BF16)]
        for l in range(nsmall):
            gref = _group_ref_rows(p, 1 << l, row)
            odd = ((row >> l) & 1) == 1
            e = jnp.exp(jnp.where(odd, p, gref - p))
            qs.append((q * e).astype(BF16))
            ks.append((k * e).astype(BF16))
            p = p + jnp.where(odd, gref, 0.0)
        near_ops.append((qs, ks, v.astype(BF16)))
        lev = []
        for l in range(nsmall, nlev):
            m = 1 << l
            pe, po = _halves(p, m)
            tot = jnp.concatenate([jnp.broadcast_to(pe[r + m - 1:r + m], (m, w))
                                   for r in range(0, c // 2, m)], axis=0)
            q_od = _halves(q, m)[1]
            k_ev = _halves(k, m)[0]
            v_ev = _halves(v, m)[0].astype(BF16)
            lev.append(((q_od * jnp.exp(po)).astype(BF16), (k_ev * jnp.exp(tot - pe)).astype(BF16), v_ev))
            p = _interleave(pe, po + tot, m)
        far_ops.append(lev)
        b = p
        b_end = b[c - 1:c, :]
        tails.append(((q * jnp.exp(b)).astype(BF16), (k * jnp.exp(b_end - b)).astype(BF16),
                      jnp.exp(b_end)))

    near_sc = [[[dot_nt(qs[j][r:r + g], ks[j][r:r + g]) for j in range(nsmall + 1)]
                for r in range(0, c, g)] for qs, ks, _ in near_ops]
    far_sc = [[dot_nt(qt, kt) for qt, kt, _ in lev] for lev in far_ops]
    upds = [lax.dot_general(vb, k_end, (((0,), (0,)), ((), ())), preferred_element_type=F32)
            for (_, _, vb), (_, k_end, _) in zip(near_ops, tails)]

    near_a = []
    for sc_chunk in near_sc:
        tiles = []
        for sc in sc_chunk:
            a = jnp.where(near == nsmall, sc[0], 0.0)
            for l in range(nsmall):
                a = jnp.where(near == l, sc[l + 1], a)
            tiles.append(a.astype(BF16))
        near_a.append(tiles)
    far_a = [[(jnp.where(far < nsmall + j, a, 0.0) if (1 << (nsmall + j)) < c // 2 else a).astype(BF16)
              for j, a in enumerate(lev)] for lev in far_sc]
    states = [st]
    for (_, _, dec), upd in zip(tails, upds):
        states.append(states[-1] * dec + upd)

    near_o = [[dot_nn(a, vb[i * g:(i + 1) * g]) for i, a in enumerate(tiles)]
              for tiles, (_, _, vb) in zip(near_a, near_ops)]
    far_o = [[dot_nn(a, ops[2]) for a, ops in zip(avs, lev)] for avs, lev in zip(far_a, far_ops)]
    inter_o = [dot_nt(qe, s.astype(BF16)) for (qe, _, _), s in zip(tails, states[:-1])]

    outs = []
    for no, fo, io in zip(near_o, far_o, inter_o):
        o = (no[0] if len(no) == 1 else jnp.concatenate(no, axis=0)) + io
        for j, x in enumerate(fo):
            m = 1 << (nsmall + j)
            o_ev, o_od = _halves(o, m)
            o = _interleave(o_ev, o_od + x, m)
        outs.append(o)
    return outs, states[-1]


def _hgrn_prompt_kernel(a_ref, wq_ref, wf_ref, wi_ref, wg_ref, p_ref, ng_ref, near_ref, far_ref,
                        o_ref, s_ref, wbf, st, *, chunk):
    i = pl.program_id(1)
    dk = wq_ref.shape[1]

    @pl.when(i == 0)
    def _():
        wbf[:, 0 * dk:1 * dk] = wq_ref[...].astype(BF16)
        wbf[:, 1 * dk:2 * dk] = wf_ref[...].astype(BF16)
        wbf[:, 2 * dk:3 * dk] = wi_ref[...].astype(BF16)
        wbf[:, 3 * dk:4 * dk] = wg_ref[...].astype(BF16)
        st[...] = jnp.zeros_like(st)

    tm = a_ref.shape[0]
    projs = [jnp.dot(a_ref[c0:c0 + chunk, :], wbf[...], preferred_element_type=F32)
             for c0 in range(0, tm, chunk)]
    lb = _lower_bound(p_ref)
    ng = ng_ref[...]
    chunks = []
    for proj in projs:
        q = _silu(proj[:, 0 * dk:1 * dk])
        logf, k = _hgrn_gates(proj[:, 1 * dk:2 * dk], lb)
        chunks.append((q, k, proj[:, 2 * dk:3 * dk], logf))
    outs, s_t = _hgrn_chunks(chunks, st[...], near_ref[...], far_ref[...])
    for j, (o, proj) in enumerate(zip(outs, projs)):
        o_ref[j * chunk:(j + 1) * chunk, :] = _group_norm_gate(o, proj[:, 3 * dk:4 * dk], ng).astype(o_ref.dtype)
    st[...] = s_t
    s_ref[0, 0] = s_t.T


def _hgrn_prompt(a, w_in, lower_bounds, norm_g, n_heads, dk):
    m, d = a.shape
    chunk = min(HG_CHUNK, m)
    tm = _pick(m, (HG_ROWS, 512, 256, 128, 64, 32, 16, 8))
    tm = max(tm, chunk)
    near, far = _level_tables(chunk)
    w_spec = lambda blk: pl.BlockSpec((d, dk), lambda h, i: (0, blk * n_heads + h))
    o, s = pl.pallas_call(
        functools.partial(_hgrn_prompt_kernel, chunk=chunk),
        out_shape=(jax.ShapeDtypeStruct((m, n_heads * dk), BF16),
                   jax.ShapeDtypeStruct((1, n_heads, dk, dk), F32)),
        grid=(n_heads, m // tm),
        in_specs=[pl.BlockSpec((tm, d), lambda h, i: (i, 0)),
                  w_spec(0), w_spec(1), w_spec(2), w_spec(3),
                  pl.BlockSpec((lower_bounds.shape[0], dk), lambda h, i: (0, h)),
                  pl.BlockSpec((1, dk), lambda h, i: (0, h)),
                  pl.BlockSpec(near.shape, lambda h, i: (0, 0)),
                  pl.BlockSpec(far.shape, lambda h, i: (0, 0))],
        out_specs=(pl.BlockSpec((tm, dk), lambda h, i: (i, h)),
                   pl.BlockSpec((1, 1, dk, dk), lambda h, i: (0, h, 0, 0))),
        scratch_shapes=[pltpu.VMEM((d, 4 * dk), BF16), pltpu.VMEM((dk, dk), F32)],
        compiler_params=_cparams(("arbitrary", "arbitrary"), V7X_VMEM_LIMIT),
        name="hgrn_prompt",
    )(a, w_in, w_in, w_in, w_in, lower_bounds, norm_g.reshape(1, -1), near, far)
    return o, s


def _hgrn_sample_kernel(qr_ref, fr_ref, ir_ref, gr_ref, p_ref, ng_ref, s0_ref, o_ref, s_ref, *, seq):
    bs = s0_ref.shape[0]
    rows = bs * seq
    lb = _lower_bound(p_ref)
    q = _silu(qr_ref[...])
    logf, k = _hgrn_gates(fr_ref[...], lb)
    v = ir_ref[...]
    row = lax.broadcasted_iota(jnp.int32, q.shape, 0)
    t = row % seq
    b = logf
    d = 1
    while d < seq:
        b = b + jnp.where(t >= d, pltpu.roll(b, d, 0), 0.0)
        d *= 2
    o = jnp.sum(q * k, axis=-1, keepdims=True) * v
    for d in range(1, seq):
        ok = t >= d
        arg = jnp.where(ok, b - pltpu.roll(b, d, 0), 0.0)
        w = jnp.sum(q * pltpu.roll(k, d, 0) * jnp.exp(arg), axis=-1, keepdims=True)
        o = o + jnp.where(ok, w, 0.0) * pltpu.roll(v, d, 0)
    qe = (q * jnp.exp(b)).astype(BF16)
    b_end = b
    d = 1
    while d < seq:
        b_end = jnp.where(t < seq - d, pltpu.roll(b_end, rows - d, 0), b_end)
        d *= 2
    k_end = k * jnp.exp(b_end - b)
    k_end_t = k_end.T
    dec_t = jnp.exp(b_end).T
    lane = lax.broadcasted_iota(jnp.int32, k_end_t.shape, 1)
    vb = v.astype(BF16)
    per8 = 8 // seq
    ois = [jnp.dot(qe[(s // per8) * 8:(s // per8) * 8 + 8, :], s0_ref[s, 0].astype(BF16),
                   preferred_element_type=F32) for s in range(bs)]
    upds = [jnp.dot(jnp.where((lane >= s * seq) & (lane < (s + 1) * seq), k_end_t, 0.0).astype(BF16), vb,
                    preferred_element_type=F32) for s in range(bs)]
    for s in range(bs):
        s_ref[s, 0] = s0_ref[s, 0] * dec_t[:, s * seq:s * seq + 1] + upds[s]
    r8 = lax.broadcasted_iota(jnp.int32, ois[0].shape, 0) // seq
    o_inter = []
    for g in range(0, bs, per8):
        acc = ois[g]
        for j in range(1, per8):
            acc = jnp.where(r8 == j, ois[g + j], acc)
        o_inter.append(acc)
    o = o + jnp.concatenate(o_inter, axis=0)
    o_ref[...] = _group_norm_gate(o, gr_ref[...], ng_ref[...]).astype(o_ref.dtype)


def _hgrn_sample(proj, lower_bounds, norm_g, state, seq):
    b, n_heads, dk, dv = state.shape
    m = proj.shape[0]
    bs = _pick(b, (64, 32, 16, 8, 4, 2))
    rows = bs * seq
    col = lambda blk: pl.BlockSpec((rows, dk), lambda h, i: (i, blk * n_heads + h))
    o, s = pl.pallas_call(
        functools.partial(_hgrn_sample_kernel, seq=seq),
        out_shape=(jax.ShapeDtypeStruct((m, n_heads * dv), BF16),
                   jax.ShapeDtypeStruct(state.shape, F32)),
        grid=(n_heads, b // bs),
        in_specs=[col(0), col(1), col(2), col(3),
                  pl.BlockSpec((lower_bounds.shape[0], dk), lambda h, i: (0, h)),
                  pl.BlockSpec((1, dv), lambda h, i: (0, h)),
                  pl.BlockSpec((bs, 1, dk, dv), lambda h, i: (i, h, 0, 0))],
        out_specs=(pl.BlockSpec((rows, dv), lambda h, i: (i, h)),
                   pl.BlockSpec((bs, 1, dk, dv), lambda h, i: (i, h, 0, 0))),
        compiler_params=_cparams(("arbitrary", "arbitrary"), V7X_VMEM_LIMIT),
        name="hgrn_sample",
    )(proj, proj, proj, proj, lower_bounds, norm_g.reshape(1, -1), state)
    return o, s


def kernel(x_prompt, x_sample, cache_swa_k, cache_swa_v, state_hgrn, state_ffn_conv, c_prompt, c_sample,
           norm1_g, norm2_g, w_ada, b_ada, attn_w_qkv, attn_w_o, attn_sinks,
           hgrn_w_in, hgrn_lower_bounds, hgrn_norm_g, hgrn_w_o,
           ffn_w_in, ffn_conv_w, ffn_conv_b, ffn_w_out, final_norm_g):
    bp, lp, d = x_prompt.shape
    bsmp, ls, _ = x_sample.shape
    assert bp == 1, "the prompt kernels carry one sequence"
    depth = w_ada.shape[0]
    n_kv, hd = cache_swa_k.shape[2], cache_swa_k.shape[3]
    n_heads = attn_sinks.shape[0]
    dq = n_heads * hd
    dkv = n_kv * hd
    hg_heads, hg_dk = state_hgrn.shape[1], state_hgrn.shape[2]
    keep = min(WINDOW, lp)

    ns = bsmp * ls
    assert ns % 8 == 0
    assert bsmp % 8 == 0
    c_seq = jnp.concatenate([c_sample, c_prompt, jnp.zeros((7, d), F32)], axis=0)
    mod = _ada(c_seq, w_ada, b_ada, ls)
    pk = dict(sample=False, n_sample=ns)
    sk = dict(sample=True, n_sample=ns)

    xp = x_prompt.reshape(lp, d)
    xs = x_sample.reshape(ns, d)
    conv_p, conv_s = [], []
    outs = {}
    w_out_bf = ffn_w_out.astype(BF16)
    hp = hs = None
    for l in range(depth):
        if l % 2 == 0:
            if hp is None:
                qkv_p = _matmul_mod(xp, norm1_g, mod, l, 0, 1, attn_w_qkv, False, ns, tm=1024, tn=1024,
                                    name="qkv_prompt")
                qkv_s = _matmul_mod(xs, norm1_g, mod, l, 0, 1, attn_w_qkv, True, ns, tm=512, tn=512,
                                    name="qkv_sample")
            else:
                qkv_p = _matmul(hp, attn_w_qkv, tn=1024, name="qkv_prompt")
                qkv_s = _matmul(hs, attn_w_qkv, name="qkv_sample")
            op = _swa_prompt(qkv_p, attn_sinks, n_kv, hd)
            outs['swa_k_prompt'] = qkv_p[lp - keep:, dq:dq + dkv].reshape(bp, keep, n_kv, hd)
            outs['swa_v_prompt'] = qkv_p[lp - keep:, dq + dkv:].reshape(bp, keep, n_kv, hd)
            outs['swa_k_sample'] = qkv_s[:, dq:dq + dkv].reshape(bsmp, ls, n_kv, hd)
            outs['swa_v_sample'] = qkv_s[:, dq + dkv:].reshape(bsmp, ls, n_kv, hd)
            os_ = _swa_sample(qkv_s[:, :dq], qkv_s[:, dq:dq + dkv], qkv_s[:, dq + dkv:],
                              cache_swa_k, cache_swa_v, attn_sinks, n_kv, hd, ls)
            w_o = attn_w_o
        else:
            if hp is None:
                hp = _modulate(xp, norm1_g, mod, l, 0, 1, False, ns)
                hs = _modulate(xs, norm1_g, mod, l, 0, 1, True, ns)
            op, sp = _hgrn_prompt(hp, hgrn_w_in, hgrn_lower_bounds, hgrn_norm_g, hg_heads, hg_dk)
            proj_s = _matmul(hs, hgrn_w_in, name="hgrn_in_sample")
            os_, ss = _hgrn_sample(proj_s, hgrn_lower_bounds, hgrn_norm_g, state_hgrn, ls)
            outs['hgrn_state_prompt'] = sp
            outs['hgrn_state_sample'] = ss
            w_o = hgrn_w_o
        xp, hp = _mm_rows(op, w_o, xp, mod, gate=(l, 2), norm_g=norm2_g, norm_layer=l, next_mod=(l, 3, 4),
                          tm=512, name="mix_out_prompt", **pk)
        xs, hs = _mm_rows(os_, w_o, xs, mod, gate=(l, 2), norm_g=norm2_g, norm_layer=l, next_mod=(l, 3, 4),
                          tm=512, name="mix_out_sample", **sk)
        actp, bufp = _ffn_in_prompt(hp, ffn_w_in, ffn_conv_w, ffn_conv_b, l)
        acts, bufs = _ffn_in_sample(hs, ffn_w_in, ffn_conv_w, ffn_conv_b, l, state_ffn_conv, ls)
        conv_p.append(bufp[None])
        conv_s.append(bufs)
        if l + 1 < depth:
            nxt = dict(norm_g=norm1_g, norm_layer=l + 1, next_mod=(l + 1, 0, 1))
        else:
            nxt = dict(norm_g=final_norm_g, norm_layer=0, next_mod=None, keep_x=False, out_dtype=F32)
        xp, hp = _mm_rows(actp, w_out_bf, xp, mod, gate=(l, 5), w_layer=l, tm=256, name="ffn_out_prompt",
                          **nxt, **pk)
        xs, hs = _mm_rows(acts, w_out_bf, xs, mod, gate=(l, 5), w_layer=l, tm=256, name="ffn_out_sample",
                          **nxt, **sk)
    y_prompt = hp.reshape(bp, lp, d)
    y_sample = hs.reshape(bsmp, ls, d)
    return (y_prompt, y_sample,
            outs['swa_k_prompt'], outs['swa_v_prompt'], outs['swa_k_sample'], outs['swa_v_sample'],
            outs['hgrn_state_prompt'], outs['hgrn_state_sample'],
            jnp.stack(conv_p), jnp.stack(conv_s))
```
